```python
import jax, jax.numpy as jnp
from jax import lax
import numpy as np

D_MODEL = 2048
BATCH = 16
SEQ = 2048
DEPTH = 4

H_A = 8
HD_A = 128
KVH_A = 1
H_I = 16
D_I = 64
D_I_ROPE = 32
K_MAX = 256
H_B = 16
HD_B = 64
KVH_B = 2
WINDOW = 128
BLOCK = 128
D_FF = 5632
ROPE_THETA = 10000.0
EPS = 1e-6
NEG = -1e30
N_MOD = 9

SPLIT_SIZES = (H_A * HD_A, KVH_A * HD_A, KVH_A * HD_A,
               H_I * D_I, D_I, H_I,
               H_B * HD_B, KVH_B * HD_B, KVH_B * HD_B,
               D_MODEL, D_MODEL)
N_IN = sum(SPLIT_SIZES)

kernel_name = "hybrid_dsa_swa_sink_macaron_adaln"


def _split_points():
    pts, acc = [], 0
    for s in SPLIT_SIZES[:-1]:
        acc += s
        pts.append(acc)
    return pts


def rms_norm(x, g):
    x32 = x.astype(jnp.float32)
    y = x32 * lax.rsqrt(jnp.mean(x32 * x32, axis=-1, keepdims=True) + EPS)
    return (y * g.astype(jnp.float32)).astype(x.dtype)


def rope_cos_sin(positions, dim, dtype):
    inv_freq = ROPE_THETA ** (-jnp.arange(0, dim, 2, dtype=jnp.float32) / dim)
    ang = positions.astype(jnp.float32)[..., None] * inv_freq
    return jnp.cos(ang)[:, :, None, :].astype(dtype), jnp.sin(ang)[:, :, None, :].astype(dtype)


def apply_rope(x, cos, sin):
    x1, x2 = jnp.split(x, 2, axis=-1)
    return jnp.concatenate([x1 * cos - x2 * sin, x2 * cos + x1 * sin], axis=-1)


def modulate(x, shift, scale):
    return x * (1.0 + scale[:, None, :]) + shift[:, None, :]


def swiglu(u, w_gate, w_up, w_down):
    return (jax.nn.silu(u @ w_gate) * (u @ w_up)) @ w_down


def sparse_indexed_attention(q, k, v, q_idx, k_idx, w_idx):
    B, S, H, Dh = q.shape
    n_sel = min(K_MAX, S // 4)
    nb = S // BLOCK
    key_pos = jnp.arange(S)
    gather = jax.vmap(lambda a, idx: a[idx])

    def block(i):
        t0 = i * BLOCK
        qa = lax.dynamic_slice_in_dim(q, t0, BLOCK, axis=1)
        qi = lax.dynamic_slice_in_dim(q_idx, t0, BLOCK, axis=1)
        wi = lax.dynamic_slice_in_dim(w_idx, t0, BLOCK, axis=1)
        tpos = t0 + jnp.arange(BLOCK)
        dots = jnp.einsum('bqhd,bsd->bqhs', qi, k_idx).astype(jnp.float32) * (D_I ** -0.5)
        score = jnp.einsum('bqhs,bqh->bqs', jax.nn.relu(dots), wi.astype(jnp.float32))
        causal = key_pos[None, :] <= tpos[:, None]
        score = jnp.where(causal[None], score, NEG)
        _, sel = lax.top_k(score, n_sel)
        k_sel = gather(k, sel)
        v_sel = gather(v, sel)
        logits = jnp.einsum('bqhd,bqkd->bqhk', qa, k_sel).astype(jnp.float32) * (Dh ** -0.5)
        valid = sel <= tpos[None, :, None]
        logits = jnp.where(valid[:, :, None, :], logits, NEG)
        p = jax.nn.softmax(logits, axis=-1).astype(v.dtype)
        return jnp.einsum('bqhk,bqkd->bqhd', p, v_sel)

    out = lax.map(block, jnp.arange(nb))
    return jnp.moveaxis(out, 0, 1).reshape(B, S, H * Dh)


def sliding_window_attention(q, k, v, sinks):
    B, S, H, Dh = q.shape
    G = k.shape[2]
    R = H // G
    nb = S // BLOCK
    qb = q.reshape(B, nb, BLOCK, G, R, Dh)
    kb = k.reshape(B, nb, BLOCK, G, Dh)
    vb = v.reshape(B, nb, BLOCK, G, Dh)
    pad = jnp.zeros_like(kb[:, :1])
    k_band = jnp.concatenate([jnp.concatenate([pad, kb[:, :-1]], axis=1), kb], axis=2)
    v_band = jnp.concatenate([jnp.concatenate([pad, vb[:, :-1]], axis=1), vb], axis=2)
    logits = jnp.einsum('bnqgrd,bnkgd->bngrqk', qb, k_band).astype(jnp.float32) * (Dh ** -0.5)
    i = jnp.arange(BLOCK)[:, None]
    j = jnp.arange(2 * BLOCK)[None, :]
    rel = i + BLOCK - j
    band = (rel >= 0) & (rel < WINDOW)
    blk = jnp.arange(nb)[:, None, None]
    mask = band[None] & ((blk > 0) | (j >= BLOCK)[None])
    logits = jnp.where(mask[None, :, None, None], logits, NEG)
    sink = sinks.astype(jnp.float32).reshape(G, R)[None, None, :, :, None, None]
    m = jnp.maximum(jnp.max(logits, axis=-1, keepdims=True), sink)
    e = jnp.exp(logits - m)
    p = (e / (jnp.sum(e, axis=-1, keepdims=True) + jnp.exp(sink - m))).astype(v.dtype)
    out = jnp.einsum('bngrqk,bnkgd->bnqgrd', p, v_band)
    return out.reshape(B, S, H * Dh)


def token_mixing(u, positions, w_in, qn_a_g, kn_a_g, qn_b_g, kn_b_g, sinks, w_o_a, w_o_b, w_out):
    B, S, _ = u.shape
    proj = u @ w_in
    qa, ka, va, qi, ki, wi, qb, kb, vb, ga, gb = jnp.split(proj, _split_points(), axis=-1)
    cos_a, sin_a = rope_cos_sin(positions, HD_A, u.dtype)
    cos_b, sin_b = rope_cos_sin(positions, HD_B, u.dtype)
    cos_i, sin_i = rope_cos_sin(positions, D_I_ROPE, u.dtype)
    qa = apply_rope(rms_norm(qa.reshape(B, S, H_A, HD_A), qn_a_g), cos_a, sin_a)
    ka = apply_rope(rms_norm(ka.reshape(B, S, KVH_A, HD_A), kn_a_g), cos_a, sin_a)[:, :, 0]
    qi = qi.reshape(B, S, H_I, D_I)
    qi = jnp.concatenate([apply_rope(qi[..., :D_I_ROPE], cos_i, sin_i), qi[..., D_I_ROPE:]], axis=-1)
    ki = ki.reshape(B, S, 1, D_I)
    ki = jnp.concatenate([apply_rope(ki[..., :D_I_ROPE], cos_i, sin_i), ki[..., D_I_ROPE:]], axis=-1)[:, :, 0]
    wi = wi * (H_I ** -0.5)
    y_a = sparse_indexed_attention(qa, ka, va, qi, ki, wi)
    qb = apply_rope(rms_norm(qb.reshape(B, S, H_B, HD_B), qn_b_g), cos_b, sin_b)
    kb = apply_rope(rms_norm(kb.reshape(B, S, KVH_B, HD_B), kn_b_g), cos_b, sin_b)
    vb = vb.reshape(B, S, KVH_B, HD_B)
    y_b = sliding_window_attention(qb, kb, vb, sinks)
    merged = jax.nn.sigmoid(ga) * (y_a @ w_o_a) + jax.nn.sigmoid(gb) * (y_b @ w_o_b)
    return merged @ w_out


def setup_inputs(seed: int = 0) -> dict:
    key = jax.random.key(seed)
    ks = jax.random.split(key, 24)
    f32 = jnp.float32

    def nrm(k, shape, scale):
        return jax.random.normal(k, shape, f32) * scale

    def gain(k, shape):
        return 1.0 + 0.02 * jax.random.normal(k, shape, f32)

    L, D, F = DEPTH, D_MODEL, D_FF
    return {
        "x": nrm(ks[0], (BATCH, SEQ, D), 1.0),
        "c": nrm(ks[1], (BATCH, D), 1.0),
        "positions": jnp.broadcast_to(jnp.arange(SEQ, dtype=jnp.int32)[None, :], (BATCH, SEQ)),
        "ada_w": nrm(ks[2], (L, D, N_MOD * D), 0.5 * D ** -0.5),
        "ada_b": nrm(ks[3], (L, N_MOD * D), 0.02),
        "norm_ffn1_g": gain(ks[4], (L, D)),
        "ffn1_w_gate": nrm(ks[5], (L, D, F), D ** -0.5),
        "ffn1_w_up": nrm(ks[6], (L, D, F), D ** -0.5),
        "ffn1_w_down": nrm(ks[7], (L, F, D), F ** -0.5),
        "norm_mix_g": gain(ks[8], (L, D)),
        "w_in": nrm(ks[9], (L, D, N_IN), D ** -0.5),
        "qn_a_g": gain(ks[10], (L, HD_A)),
        "kn_a_g": gain(ks[11], (L, HD_A)),
        "qn_b_g": gain(ks[12], (L, HD_B)),
        "kn_b_g": gain(ks[13], (L, HD_B)),
        "sinks": nrm(ks[14], (L, H_B), 1.0),
        "w_o_a": nrm(ks[15], (L, H_A * HD_A, D), (H_A * HD_A) ** -0.5),
        "w_o_b": nrm(ks[16], (L, H_B * HD_B, D), (H_B * HD_B) ** -0.5),
        "w_out": nrm(ks[17], (L, D, D), D ** -0.5),
        "norm_ffn2_g": gain(ks[18], (L, D)),
        "ffn2_w_gate": nrm(ks[19], (L, D, F), D ** -0.5),
        "ffn2_w_up": nrm(ks[20], (L, D, F), D ** -0.5),
        "ffn2_w_down": nrm(ks[21], (L, F, D), F ** -0.5),
    }


def reference(x, c, positions, ada_w, ada_b, norm_ffn1_g, ffn1_w_gate, ffn1_w_up, ffn1_w_down,
              norm_mix_g, w_in, qn_a_g, kn_a_g, qn_b_g, kn_b_g, sinks, w_o_a, w_o_b, w_out,
              norm_ffn2_g, ffn2_w_gate, ffn2_w_up, ffn2_w_down):
    h = x
    c_act = jax.nn.silu(c)
    for l in range(DEPTH):
        mod = c_act @ ada_w[l] + ada_b[l]
        sh1, sc1, g1, sh2, sc2, g2, sh3, sc3, g3 = jnp.split(mod, N_MOD, axis=-1)
        u = modulate(rms_norm(h, norm_ffn1_g[l]), sh1, sc1)
        h = h + 0.5 * g1[:, None, :] * swiglu(u, ffn1_w_gate[l], ffn1_w_up[l], ffn1_w_down[l])
        u = modulate(rms_norm(h, norm_mix_g[l]), sh2, sc2)
        h = h + g2[:, None, :] * token_mixing(u, positions, w_in[l], qn_a_g[l], kn_a_g[l],
                                              qn_b_g[l], kn_b_g[l], sinks[l],
                                              w_o_a[l], w_o_b[l], w_out[l])
        u = modulate(rms_norm(h, norm_ffn2_g[l]), sh3, sc3)
        h = h + 0.5 * g3[:, None, :] * swiglu(u, ffn2_w_gate[l], ffn2_w_up[l], ffn2_w_down[l])
    return h
```

```python
import functools

import jax
import jax.numpy as jnp
from jax import lax
from jax.experimental import pallas as pl
from jax.experimental.pallas import tpu as pltpu

H_A, HD_A = 8, 128
H_I, D_I, D_I_ROPE = 16, 64, 32
K_MAX = 256
H_B, HD_B, KVH_B = 16, 64, 2
BLOCK = 128
ROPE_THETA = 10000.0
EPS = 1e-6
NEG = -1e30
N_MOD = 9

LANES = 128
V7X_VMEM_LIMIT_BYTES = 56 * 1024 * 1024
MXU_DTYPE = jnp.bfloat16

INT_MIN = -2147483648
M_INIT = -1e29
LOG2E = 1.4426950408889634
SEL_COLS = 768
N_BIG = 3 * 1024

f32 = jnp.float32
i32 = jnp.int32


def _dot(a, b):
    return jnp.dot(a, b, preferred_element_type=f32)


def _params(*sem):
    return pltpu.CompilerParams(dimension_semantics=sem, vmem_limit_bytes=V7X_VMEM_LIMIT_BYTES)


def _ada_kernel(c_ref, w_ref, b_ref, o_ref):
    c = c_ref[...]
    act = (c * jax.nn.sigmoid(c)).astype(MXU_DTYPE)
    o_ref[0] = _dot(act, w_ref[0].astype(MXU_DTYPE)) + b_ref[0]


def _ada_mod(c, ada_w, ada_b):
    L, D, N = ada_w.shape
    B = c.shape[0]
    tn = min(N, 1024)
    return pl.pallas_call(
        _ada_kernel,
        grid=(L, N // tn),
        in_specs=[pl.BlockSpec((B, D), lambda l, j: (0, 0)),
                  pl.BlockSpec((1, D, tn), lambda l, j: (l, 0, j)),
                  pl.BlockSpec((1, 1, tn), lambda l, j: (l, 0, j))],
        out_specs=pl.BlockSpec((1, B, tn), lambda l, j: (l, 0, j)),
        out_shape=jax.ShapeDtypeStruct((L, B, N), f32),
        compiler_params=_params("arbitrary", "arbitrary"),
        name="ada_mod",
    )(c, ada_w, ada_b.reshape(L, 1, N))


def _rope_kernel(pos_ref, freq_ref, sign_ref, o_ref):
    pos = pos_ref[...].astype(f32)
    for k in range(3):
        ang = pos * freq_ref[k:k + 1, :]
        o_ref[:, (2 * k) * LANES:(2 * k + 1) * LANES] = jnp.cos(ang)
        o_ref[:, (2 * k + 1) * LANES:(2 * k + 2) * LANES] = jnp.sin(ang) * sign_ref[k:k + 1, :]


def _rope_rows():
    lane = jnp.arange(LANES)

    def inv_freq(dim):
        return ROPE_THETA ** (-jnp.arange(0, dim, 2, dtype=f32) / dim)

    fa = inv_freq(HD_A)[lane % (HD_A // 2)]
    sa = jnp.where(lane % HD_A < HD_A // 2, -1.0, 1.0)
    fb = inv_freq(HD_B)[lane % (HD_B // 2)]
    sb = jnp.where(lane % HD_B < HD_B // 2, -1.0, 1.0)
    in_rope = lane % D_I < D_I_ROPE
    fi = jnp.where(in_rope, inv_freq(D_I_ROPE)[lane % (D_I_ROPE // 2)], 0.0)
    si = jnp.where(lane % D_I < D_I_ROPE // 2, -1.0, 1.0)
    return jnp.stack([fa, fb, fi]).astype(f32), jnp.stack([sa, sb, si]).astype(f32)


def _rope_tables(positions):
    n = positions.size
    tm = min(n, 1024)
    freq, sign = _rope_rows()
    return pl.pallas_call(
        _rope_kernel,
        grid=(n // tm,),
        in_specs=[pl.BlockSpec((tm, 1), lambda i: (i, 0)),
                  pl.BlockSpec((3, LANES), lambda i: (0, 0)),
                  pl.BlockSpec((3, LANES), lambda i: (0, 0))],
        out_specs=pl.BlockSpec((tm, 6 * LANES), lambda i: (i, 0)),
        out_shape=jax.ShapeDtypeStruct((n, 6 * LANES), f32),
        compiler_params=_params("arbitrary"),
        name="rope_tables",
    )(positions.reshape(n, 1), freq, sign)


def _norm_modulate(x, g, shift, scale):
    ms = jnp.mean(x * x, axis=-1, keepdims=True)
    y = x * lax.rsqrt(ms + EPS) * g
    return y * (1.0 + scale) + shift


def _head_norm(x, gain, lane, head_dim):
    sq = x * x
    if head_dim == LANES:
        ms = jnp.mean(sq, axis=-1, keepdims=True)
    else:
        lo = lane < head_dim
        s_lo = jnp.sum(jnp.where(lo, sq, 0.0), axis=-1, keepdims=True)
        s_hi = jnp.sum(jnp.where(lo, 0.0, sq), axis=-1, keepdims=True)
        ms = jnp.where(lo, s_lo, s_hi) * (1.0 / head_dim)
    return x * lax.rsqrt(ms + EPS) * gain


def _rotate(y, cos, sin, lane, half, period):
    if 2 * half == LANES:
        partner = pltpu.roll(y, half, 1)
    else:
        fwd = pltpu.roll(y, LANES - half, 1)
        bwd = pltpu.roll(y, half, 1)
        partner = jnp.where(lane % period < half, fwd, bwd)
    return y * cos + partner * sin


def _ffn_kernel(h_ref, sh_ref, sc_ref, gt_ref, g_ref, wg_ref, wu_ref, wd_ref, o_ref, u_ref):
    j = pl.program_id(1)

    @pl.when(j == 0)
    def _():
        x = h_ref[...]
        u_ref[...] = _norm_modulate(x, g_ref[...], sh_ref[0], sc_ref[0]).astype(MXU_DTYPE)
        o_ref[...] = x

    u = u_ref[...]
    a = _dot(u, wg_ref[...])
    b = _dot(u, wu_ref[...])
    act = (a * jax.nn.sigmoid(a) * b).astype(MXU_DTYPE)
    o_ref[...] += (0.5 * gt_ref[0]) * _dot(act, wd_ref[...])


def _ffn(h, shift, scale, gate, g, wg, wu, wd, *, seq, tm, tf):
    n, d = h.shape
    f = wg.shape[1]
    per_b = seq // tm
    row = pl.BlockSpec((1, 1, d), lambda i, j: (i // per_b, 0, 0))
    return pl.pallas_call(
        _ffn_kernel,
        grid=(n // tm, f // tf),
        in_specs=[pl.BlockSpec((tm, d), lambda i, j: (i, 0)), row, row, row,
                  pl.BlockSpec((1, d), lambda i, j: (0, 0)),
                  pl.BlockSpec((d, tf), lambda i, j: (0, j)),
                  pl.BlockSpec((d, tf), lambda i, j: (0, j)),
                  pl.BlockSpec((tf, d), lambda i, j: (j, 0))],
        out_specs=pl.BlockSpec((tm, d), lambda i, j: (i, 0)),
        out_shape=jax.ShapeDtypeStruct((n, d), f32),
        scratch_shapes=[pltpu.VMEM((tm, d), MXU_DTYPE)],
        compiler_params=_params("arbitrary", "arbitrary"),
        name="ffn",
    )(h, shift, scale, gate, g, wg, wu, wd)


def _proj_kernel(h_ref, sh_ref, sc_ref, g_ref, wbig_ref, wsel_ref, gains_ref, rope_ref,
                 qa_ref, qi_ref, qb_ref, gates_ref, kv_ref, ki_ref, wi_ref, u_ref):
    j = pl.program_id(1)
    tm = h_ref.shape[0]
    lane = lax.broadcasted_iota(i32, (tm, LANES), 1)

    def table(k):
        return rope_ref[:, k * LANES:(k + 1) * LANES]

    def gain(k):
        return gains_ref[k:k + 1, :]

    def rope_a(y):
        return _rotate(y, table(0), table(1), lane, HD_A // 2, HD_A)

    def rope_b(y):
        return _rotate(y, table(2), table(3), lane, HD_B // 2, HD_B)

    def rope_i(y):
        return _rotate(y, table(4), table(5), lane, D_I_ROPE // 2, D_I)

    @pl.when(j == 0)
    def _():
        u_ref[...] = _norm_modulate(h_ref[...], g_ref[...], sh_ref[0], sc_ref[0]).astype(MXU_DTYPE)
        sel = _dot(u_ref[...], wsel_ref[...])
        ka = rope_a(_head_norm(sel[:, 0:128], gain(1), lane, HD_A))
        kb = rope_b(_head_norm(sel[:, 256:384], gain(3), lane, HD_B))
        kv_ref[:, 0:128] = ka.astype(MXU_DTYPE)
        kv_ref[:, 128:256] = sel[:, 128:256].astype(MXU_DTYPE)
        kv_ref[:, 256:384] = kb.astype(MXU_DTYPE)
        kv_ref[:, 384:512] = sel[:, 384:512].astype(MXU_DTYPE)
        ki_ref[...] = rope_i(sel[:, 512:640]).astype(MXU_DTYPE)
        wi_ref[...] = sel[:, 640:768] * (H_I ** -0.5 * D_I ** -0.5)

    res = _dot(u_ref[...], wbig_ref[...])

    @pl.when(j == 0)
    def _():
        for k in range(8):
            y = _head_norm(res[:, k * LANES:(k + 1) * LANES], gain(0), lane, HD_A)
            qa_ref[:, k * LANES:(k + 1) * LANES] = rope_a(y).astype(MXU_DTYPE)

    @pl.when(j == 1)
    def _():
        for k in range(8):
            qi_ref[:, k * LANES:(k + 1) * LANES] = rope_i(res[:, k * LANES:(k + 1) * LANES]).astype(MXU_DTYPE)

    @pl.when(j == 2)
    def _():
        for k in range(8):
            y = _head_norm(res[:, k * LANES:(k + 1) * LANES], gain(2), lane, HD_B)
            qb_ref[:, k * LANES:(k + 1) * LANES] = (rope_b(y) * (HD_B ** -0.5)).astype(MXU_DTYPE)

    @pl.when(j >= 3)
    def _():
        gates_ref[...] = jax.nn.sigmoid(res).astype(MXU_DTYPE)


def _proj(h, shift, scale, g, wbig, wsel, gains, rope, *, seq, tm):
    n, d = h.shape
    nbig = wbig.shape[1]
    d_gate = (nbig - N_BIG) // 2
    per_b = seq // tm
    tn = 1024
    row = pl.BlockSpec((1, 1, d), lambda i, j: (i // per_b, 0, 0))
    tile = lambda w: pl.BlockSpec((tm, w), lambda i, j: (i, 0))
    bf = lambda w: jax.ShapeDtypeStruct((n, w), MXU_DTYPE)
    return pl.pallas_call(
        _proj_kernel,
        grid=(n // tm, nbig // tn),
        in_specs=[tile(d), row, row,
                  pl.BlockSpec((1, d), lambda i, j: (0, 0)),
                  pl.BlockSpec((d, tn), lambda i, j: (0, j)),
                  pl.BlockSpec((d, SEL_COLS), lambda i, j: (0, 0)),
                  pl.BlockSpec((4, LANES), lambda i, j: (0, 0)),
                  tile(6 * LANES)],
        out_specs=[tile(1024), tile(1024), tile(1024),
                   pl.BlockSpec((tm, tn), lambda i, j: (i, jnp.maximum(j - 3, 0))),
                   tile(512), tile(LANES), tile(LANES)],
        out_shape=[bf(1024), bf(1024), bf(1024), bf(2 * d_gate), bf(512), bf(LANES),
                   jax.ShapeDtypeStruct((n, LANES), f32)],
        scratch_shapes=[pltpu.VMEM((tm, d), MXU_DTYPE)],
        compiler_params=_params("arbitrary", "arbitrary"),
        name="mix_proj",
    )(h, shift, scale, g, wbig, wsel, gains, rope)


def _attn_kernel(qa_ref, qi_ref, qb_ref, wi_ref, kv_ref, ki_ref, sink_ref,
                 ya_ref, yb_ref,
                 vat_ref, vbt_ref, qat_ref, qit_ref, key_ref, acc_ref, m_ref, l_ref, *, n_sel):
    i = pl.program_id(1)
    nb = pl.num_programs(1)
    nchunk = i + 1

    def rows(c):
        return pl.ds(pl.multiple_of(c * BLOCK, BLOCK), BLOCK)

    @pl.when(i == 0)
    def _():
        def body(c, carry):
            vat_ref[c] = kv_ref[rows(c), 128:256].astype(f32).T.astype(MXU_DTYPE)
            vbt_ref[c] = kv_ref[rows(c), 384:512].astype(f32).T.astype(MXU_DTYPE)
            return carry
        lax.fori_loop(0, nb, body, 0)

    qa = qa_ref[...].astype(f32)
    qi = qi_ref[...].astype(f32)
    for k in range(8):
        sl = slice(k * LANES, (k + 1) * LANES)
        qat_ref[:, sl] = qa[:, sl].T.astype(MXU_DTYPE)
        qit_ref[sl, :] = qi[:, sl].T.astype(MXU_DTYPE)
    w_t = wi_ref[...].T

    q_lane = lax.broadcasted_iota(i32, (BLOCK, LANES), 1)
    k_row = lax.broadcasted_iota(i32, (BLOCK, LANES), 0)
    q_pos = i * BLOCK + q_lane

    def idx_body(c, carry):
        kc = ki_ref[rows(c), 0:D_I]
        acc = jnp.zeros((BLOCK, LANES), f32)
        for h in range(H_I):
            d = _dot(kc, qit_ref[h * D_I:(h + 1) * D_I, :])
            acc = acc + w_t[h:h + 1, :] * jnp.maximum(d, 0.0)
        bits = lax.bitcast_convert_type(acc + 0.0, i32)
        key = jnp.where(bits >= 0, bits, bits ^ 0x7FFFFFFF)
        causal = (c * BLOCK + k_row) <= q_pos
        key_ref[rows(c), :] = jnp.where(causal, key, INT_MIN)
        return carry
    lax.fori_loop(0, nchunk, idx_body, 0)

    need = jnp.minimum(n_sel, q_pos[0:1, :] + 1)

    def bit_body(k, thr):
        cand = thr ^ lax.shift_left(jnp.int32(1), 31 - k)

        def cnt_body(c, cnt):
            ind = jnp.where(key_ref[rows(c), :] >= cand, 1, 0)
            return cnt + jnp.sum(ind.reshape(BLOCK // 8, 8, LANES), axis=0)
        cnt = lax.fori_loop(0, nchunk, cnt_body, jnp.zeros((8, LANES), i32))
        cnt = jnp.sum(cnt, axis=0, keepdims=True)
        return jnp.where(cnt >= need, cand, thr)
    thr = lax.fori_loop(0, 32, bit_body, jnp.full((1, LANES), INT_MIN, i32))

    m_ref[...] = jnp.full(m_ref.shape, M_INIT, f32)
    l_ref[...] = jnp.zeros(l_ref.shape, f32)
    acc_ref[...] = jnp.zeros(acc_ref.shape, f32)
    c_a = HD_A ** -0.5 * LOG2E

    def att_body(c, carry):
        s = _dot(kv_ref[rows(c), 0:128], qat_ref[...])
        bias = jnp.where(key_ref[rows(c), :] >= thr, 0.0, NEG)
        z = s + jnp.concatenate([bias] * H_A, axis=1)
        m_old = m_ref[...]
        m_new = jnp.maximum(m_old, jnp.max(z, axis=0, keepdims=True))
        alpha = jnp.exp2((m_old - m_new) * c_a)
        p = jnp.exp2((z - m_new) * c_a)
        l_ref[...] = alpha * l_ref[...] + jnp.sum(p, axis=0, keepdims=True)
        acc_ref[...] = alpha * acc_ref[...] + _dot(vat_ref[c], p.astype(MXU_DTYPE))
        m_ref[...] = m_new
        return carry
    lax.fori_loop(0, nchunk, att_body, 0)

    ya_t = acc_ref[...] * (1.0 / l_ref[...])
    for k in range(H_A):
        sl = slice(k * LANES, (k + 1) * LANES)
        ya_ref[:, sl] = ya_t[:, sl].T.astype(MXU_DTYPE)

    ip = jnp.maximum(i - 1, 0)
    kband = jnp.concatenate([kv_ref[rows(ip), 256:384], kv_ref[rows(i), 256:384]], axis=0)
    qb = qb_ref[...].astype(f32)
    zero = jnp.zeros((HD_B, LANES), MXU_DTYPE)
    ext = []
    for k in range(8):
        t = qb[:, k * LANES:(k + 1) * LANES].T.astype(MXU_DTYPE)
        for e in range(2):
            piece = t[e * HD_B:(e + 1) * HD_B, :]
            grp = (2 * k + e) // (H_B // KVH_B)
            ext.append(jnp.concatenate([piece, zero] if grp == 0 else [zero, piece], axis=0))
    s = _dot(kband, jnp.concatenate(ext, axis=1))
    prev_open = jnp.where(i > 0, 0.0, NEG)
    bias = jnp.concatenate([jnp.where(k_row > q_lane, prev_open, NEG),
                            jnp.where(k_row <= q_lane, 0.0, NEG)], axis=0)
    z = s + jnp.concatenate([bias] * H_B, axis=1)
    sink = sink_ref[...]
    m = jnp.maximum(jnp.max(z, axis=0, keepdims=True), sink)
    e = jnp.exp(z - m)
    inv = 1.0 / (jnp.sum(e, axis=0, keepdims=True) + jnp.exp(sink - m))
    pb = e.astype(MXU_DTYPE)
    per_g = H_B // KVH_B
    pieces = []
    for g in range(KVH_B):
        gs = slice(g * HD_B, (g + 1) * HD_B)
        cols = slice(g * per_g * LANES, (g + 1) * per_g * LANES)
        vband = jnp.concatenate([vbt_ref[ip][gs, :], vbt_ref[i][gs, :]], axis=1)
        o = _dot(vband, pb[:, cols]) * inv[:, cols]
        pieces += [o[:, hh * LANES:(hh + 1) * LANES] for hh in range(per_g)]
    for k in range(8):
        yb_ref[:, k * LANES:(k + 1) * LANES] = (
            jnp.concatenate([pieces[2 * k], pieces[2 * k + 1]], axis=0).T.astype(MXU_DTYPE))


def _attn(qa, qi, qb, wi, kv, ki, sink_row, *, batch, seq):
    n = qa.shape[0]
    nb = seq // BLOCK
    n_sel = min(K_MAX, seq // 4)
    blk = lambda w: pl.BlockSpec((BLOCK, w), lambda b, i: (b * nb + i, 0))
    per_b = lambda w: pl.BlockSpec((seq, w), lambda b, i: (b, 0))
    return pl.pallas_call(
        functools.partial(_attn_kernel, n_sel=n_sel),
        grid=(batch, nb),
        in_specs=[blk(1024), blk(1024), blk(1024), blk(LANES), per_b(512), per_b(LANES),
                  pl.BlockSpec((1, H_B * LANES), lambda b, i: (0, 0))],
        out_specs=[blk(1024), blk(1024)],
        out_shape=[jax.ShapeDtypeStruct((n, 1024), MXU_DTYPE)] * 2,
        scratch_shapes=[pltpu.VMEM((nb, LANES, BLOCK), MXU_DTYPE),
                        pltpu.VMEM((nb, LANES, BLOCK), MXU_DTYPE),
                        pltpu.VMEM((HD_A, H_A * LANES), MXU_DTYPE),
                        pltpu.VMEM((H_I * D_I, LANES), MXU_DTYPE),
                        pltpu.VMEM((seq, LANES), i32),
                        pltpu.VMEM((HD_A, H_A * LANES), f32),
                        pltpu.VMEM((1, H_A * LANES), f32),
                        pltpu.VMEM((1, H_A * LANES), f32)],
        compiler_params=_params("arbitrary", "arbitrary"),
        name="mixers",
    )(qa, qi, qb, wi, kv, ki, sink_row)


def _out_kernel(h_ref, ya_ref, yb_ref, ga_ref, gb_ref, gt_ref, woa_ref, wob_ref, wout_ref, o_ref):
    @pl.when(pl.program_id(1) == 0)
    def _():
        o_ref[...] = h_ref[...]

    merged = (ga_ref[...].astype(f32) * _dot(ya_ref[...], woa_ref[...])
              + gb_ref[...].astype(f32) * _dot(yb_ref[...], wob_ref[...]))
    o_ref[...] += gt_ref[0] * _dot(merged.astype(MXU_DTYPE), wout_ref[...])


def _out_proj(h, ya, yb, gates, gate, woa, wob, wout, *, seq, tm, tn):
    n, d = h.shape
    k = ya.shape[1]
    per_b = seq // tm
    nj = d // tn
    return pl.pallas_call(
        _out_kernel,
        grid=(n // tm, nj),
        in_specs=[pl.BlockSpec((tm, d), lambda i, j: (i, 0)),
                  pl.BlockSpec((tm, k), lambda i, j: (i, 0)),
                  pl.BlockSpec((tm, k), lambda i, j: (i, 0)),
                  pl.BlockSpec((tm, tn), lambda i, j: (i, j)),
                  pl.BlockSpec((tm, tn), lambda i, j: (i, nj + j)),
                  pl.BlockSpec((1, 1, d), lambda i, j: (i // per_b, 0, 0)),
                  pl.BlockSpec((k, tn), lambda i, j: (0, j)),
                  pl.BlockSpec((k, tn), lambda i, j: (0, j)),
                  pl.BlockSpec((tn, d), lambda i, j: (j, 0))],
        out_specs=pl.BlockSpec((tm, d), lambda i, j: (i, 0)),
        out_shape=jax.ShapeDtypeStruct((n, d), f32),
        compiler_params=_params("arbitrary", "arbitrary"),
        name="out_proj",
    )(h, ya, yb, gates, gates, gate, woa, wob, wout)


def _tiles(seq, d_ff):
    tm = min(seq, 512)
    tf = 512 if d_ff % 512 == 0 else 256
    return tm, tf


def _split_w_in(w_in, d):
    sizes = (H_A * HD_A, HD_A, HD_A, H_I * D_I, D_I, H_I, H_B * HD_B, KVH_B * HD_B, KVH_B * HD_B, d, d)
    offs = [0]
    for s in sizes:
        offs.append(offs[-1] + s)
    seg = lambda k: w_in[:, :, offs[k]:offs[k + 1]]
    pad = lambda a, w: jnp.pad(a, ((0, 0), (0, 0), (0, w - a.shape[-1])))
    big = jnp.concatenate([seg(0), seg(3), seg(6), seg(9), seg(10)], axis=-1)
    sel = jnp.concatenate([seg(1), seg(2), seg(7), seg(8), pad(seg(4), LANES), pad(seg(5), LANES)], axis=-1)
    return big.astype(MXU_DTYPE), sel.astype(MXU_DTYPE)


def kernel(x, c, positions, ada_w, ada_b, norm_ffn1_g, ffn1_w_gate, ffn1_w_up, ffn1_w_down, norm_mix_g, w_in, qn_a_g, kn_a_g, qn_b_g, kn_b_g, sinks, w_o_a, w_o_b, w_out, norm_ffn2_g, ffn2_w_gate, ffn2_w_up, ffn2_w_down):
    batch, seq, d = x.shape
    depth = ada_w.shape[0]
    d_ff = ffn1_w_gate.shape[-1]
    n = batch * seq
    assert seq % BLOCK == 0 and d % 1024 == 0
    tm, tf = _tiles(seq, d_ff)
    tn = min(d, 512)

    mod = _ada_mod(c, ada_w, ada_b).reshape(depth, batch, N_MOD, 1, d)
    rope = _rope_tables(positions)
    wbig, wsel = _split_w_in(w_in, d)
    cast = lambda w: w.astype(MXU_DTYPE)
    w1 = (cast(ffn1_w_gate), cast(ffn1_w_up), cast(ffn1_w_down))
    w2 = (cast(ffn2_w_gate), cast(ffn2_w_up), cast(ffn2_w_down))
    woa, wob, wout = cast(w_o_a), cast(w_o_b), cast(w_out)
    gains = jnp.stack([qn_a_g, kn_a_g, jnp.tile(qn_b_g, (1, LANES // HD_B)),
                       jnp.tile(kn_b_g, (1, LANES // HD_B))], axis=1)
    sink_rows = jnp.repeat(sinks, LANES, axis=1).reshape(depth, 1, H_B * LANES)

    h = x.reshape(n, d)
    for l in range(depth):
        m = [mod[l, :, k] for k in range(N_MOD)]
        h = _ffn(h, m[0], m[1], m[2], norm_ffn1_g[l][None], w1[0][l], w1[1][l], w1[2][l],
                 seq=seq, tm=tm, tf=tf)
        qa, qi, qb, gates, kv, ki, wi = _proj(h, m[3], m[4], norm_mix_g[l][None], wbig[l], wsel[l],
                                              gains[l], rope, seq=seq, tm=tm)
        ya, yb = _attn(qa, qi, qb, wi, kv, ki, sink_rows[l], batch=batch, seq=seq)
        h = _out_proj(h, ya, yb, gates, m[5], woa[l], wob[l], wout[l], seq=seq, tm=tm, tn=tn)
        h = _ffn(h, m[6], m[7], m[8], norm_ffn2_g[l][None], w2[0][l], w2[1][l], w2[2][l],
                 seq=seq, tm=tm, tf=tf)
    return h.reshape(batch, seq, d)
```

```python
import functools

import jax
import jax.numpy as jnp
from jax import lax
from jax.experimental import pallas as pl
from jax.experimental.pallas import tpu as pltpu

H_A, HD_A = 8, 128
H_I, D_I, D_I_ROPE = 16, 64, 32
K_MAX = 256
H_B, HD_B, KVH_B = 16, 64, 2
BLOCK = 128
ROPE_THETA = 10000.0
EPS = 1e-6
NEG = -1e30
N_MOD = 9

LANES = 128
V7X_VMEM_LIMIT_BYTES = 56 * 1024 * 1024
MXU_DTYPE = jnp.bfloat16

INT_MIN = -2147483648
M_INIT = -1e29
LOG2E = 1.4426950408889634
SEL_COLS = 768
N_BIG = 3 * 1024

f32 = jnp.float32
i32 = jnp.int32


def _dot(a, b):
    return jnp.dot(a, b, preferred_element_type=f32)


def _params(*sem):
    return pltpu.CompilerParams(dimension_semantics=sem, vmem_limit_bytes=V7X_VMEM_LIMIT_BYTES)


def _ada_kernel(c_ref, w_ref, b_ref, o_ref):
    c = c_ref[...]
    act = (c * jax.nn.sigmoid(c)).astype(MXU_DTYPE)
    o_ref[0] = _dot(act, w_ref[0].astype(MXU_DTYPE)) + b_ref[0]


def _ada_mod(c, ada_w, ada_b):
    L, D, N = ada_w.shape
    B = c.shape[0]
    tn = min(N, 1024)
    return pl.pallas_call(
        _ada_kernel,
        grid=(L, N // tn),
        in_specs=[pl.BlockSpec((B, D), lambda l, j: (0, 0)),
                  pl.BlockSpec((1, D, tn), lambda l, j: (l, 0, j)),
                  pl.BlockSpec((1, 1, tn), lambda l, j: (l, 0, j))],
        out_specs=pl.BlockSpec((1, B, tn), lambda l, j: (l, 0, j)),
        out_shape=jax.ShapeDtypeStruct((L, B, N), f32),
        compiler_params=_params("arbitrary", "arbitrary"),
        name="ada_mod",
    )(c, ada_w, ada_b.reshape(L, 1, N))


def _rope_kernel(pos_ref, freq_ref, sign_ref, o_ref):
    pos = pos_ref[...].astype(f32)
    for k in range(3):
        ang = pos * freq_ref[k:k + 1, :]
        o_ref[:, (2 * k) * LANES:(2 * k + 1) * LANES] = jnp.cos(ang)
        o_ref[:, (2 * k + 1) * LANES:(2 * k + 2) * LANES] = jnp.sin(ang) * sign_ref[k:k + 1, :]


def _rope_rows():
    lane = jnp.arange(LANES)

    def inv_freq(dim):
        return ROPE_THETA ** (-jnp.arange(0, dim, 2, dtype=f32) / dim)

    fa = inv_freq(HD_A)[lane % (HD_A // 2)]
    sa = jnp.where(lane % HD_A < HD_A // 2, -1.0, 1.0)
    fb = inv_freq(HD_B)[lane % (HD_B // 2)]
    sb = jnp.where(lane % HD_B < HD_B // 2, -1.0, 1.0)
    in_rope = lane % D_I < D_I_ROPE
    fi = jnp.where(in_rope, inv_freq(D_I_ROPE)[lane % (D_I_ROPE // 2)], 0.0)
    si = jnp.where(lane % D_I < D_I_ROPE // 2, -1.0, 1.0)
    return jnp.stack([fa, fb, fi]).astype(f32), jnp.stack([sa, sb, si]).astype(f32)


def _rope_tables(positions):
    n = positions.size
    tm = min(n, 1024)
    freq, sign = _rope_rows()
    return pl.pallas_call(
        _rope_kernel,
        grid=(n // tm,),
        in_specs=[pl.BlockSpec((tm, 1), lambda i: (i, 0)),
                  pl.BlockSpec((3, LANES), lambda i: (0, 0)),
                  pl.BlockSpec((3, LANES), lambda i: (0, 0))],
        out_specs=pl.BlockSpec((tm, 6 * LANES), lambda i: (i, 0)),
        out_shape=jax.ShapeDtypeStruct((n, 6 * LANES), f32),
        compiler_params=_params("arbitrary"),
        name="rope_tables",
    )(positions.reshape(n, 1), freq, sign)


def _norm_modulate(x, g, shift, scale):
    ms = jnp.mean(x * x, axis=-1, keepdims=True)
    y = x * lax.rsqrt(ms + EPS) * g
    return y * (1.0 + scale) + shift


def _head_norm(x, gain, lane, head_dim):
    sq = x * x
    if head_dim == LANES:
        ms = jnp.mean(sq, axis=-1, keepdims=True)
    else:
        lo = lane < head_dim
        s_lo = jnp.sum(jnp.where(lo, sq, 0.0), axis=-1, keepdims=True)
        s_hi = jnp.sum(jnp.where(lo, 0.0, sq), axis=-1, keepdims=True)
        ms = jnp.where(lo, s_lo, s_hi) * (1.0 / head_dim)
    return x * lax.rsqrt(ms + EPS) * gain


def _rotate(y, cos, sin, lane, half, period):
    if 2 * half == LANES:
        partner = pltpu.roll(y, half, 1)
    else:
        fwd = pltpu.roll(y, LANES - half, 1)
        bwd = pltpu.roll(y, half, 1)
        partner = jnp.where(lane % period < half, fwd, bwd)
    return y * cos + partner * sin


def _ffn_kernel(h_ref, hc_ref, sh_ref, sc_ref, gt_ref, g_ref, wg_ref, wu_ref, wd_ref, o_ref,
                u_ref, act_ref, *, nf, tf):
    j = pl.program_id(1)

    @pl.when(j == 0)
    def _():
        u_ref[...] = _norm_modulate(h_ref[...], g_ref[...], sh_ref[0], sc_ref[0]).astype(MXU_DTYPE)

    @pl.when(j < nf)
    def _():
        u = u_ref[...]
        a = _dot(u, wg_ref[...])
        b = _dot(u, wu_ref[...])
        act = (a * jax.nn.sigmoid(a) * b).astype(MXU_DTYPE)
        act_ref[:, pl.ds(pl.multiple_of(j * tf, tf), tf)] = act

    @pl.when(j >= nf)
    def _():
        o_ref[...] = hc_ref[...] + (0.5 * gt_ref[0]) * _dot(act_ref[...], wd_ref[...])


def _ffn(h, shift, scale, gate, g, wg, wu, wd, *, seq, tm, tf, tn):
    n, d = h.shape
    f = wg.shape[1]
    nf = f // tf
    per_b = seq // tm
    row = pl.BlockSpec((1, 1, d), lambda i, j: (i // per_b, 0, 0))
    up = lambda i, j: (0, jnp.minimum(j, nf - 1))
    down = lambda j: jnp.maximum(j - nf, 0)
    return pl.pallas_call(
        functools.partial(_ffn_kernel, nf=nf, tf=tf),
        grid=(n // tm, nf + d // tn),
        in_specs=[pl.BlockSpec((tm, d), lambda i, j: (i, 0)),
                  pl.BlockSpec((tm, tn), lambda i, j: (i, down(j))),
                  row, row,
                  pl.BlockSpec((1, 1, tn), lambda i, j: (i // per_b, 0, down(j))),
                  pl.BlockSpec((1, d), lambda i, j: (0, 0)),
                  pl.BlockSpec((d, tf), up),
                  pl.BlockSpec((d, tf), up),
                  pl.BlockSpec((f, tn), lambda i, j: (0, down(j)))],
        out_specs=pl.BlockSpec((tm, tn), lambda i, j: (i, down(j))),
        out_shape=jax.ShapeDtypeStruct((n, d), f32),
        scratch_shapes=[pltpu.VMEM((tm, d), MXU_DTYPE), pltpu.VMEM((tm, f), MXU_DTYPE)],
        compiler_params=_params("arbitrary", "arbitrary"),
        name="ffn",
    )(h, h, shift, scale, gate, g, wg, wu, wd)


def _proj_kernel(h_ref, sh_ref, sc_ref, g_ref, wbig_ref, wsel_ref, gains_ref, rope_ref,
                 qa_ref, qi_ref, qb_ref, gates_ref, kv_ref, ki_ref, wi_ref, u_ref, ra_ref, rb_ref):
    j = pl.program_id(1)
    tm = h_ref.shape[0]
    lane = lax.broadcasted_iota(i32, (tm, LANES), 1)

    def table(k):
        return rope_ref[:, k * LANES:(k + 1) * LANES]

    def gain(k):
        return gains_ref[k:k + 1, :]

    def rope_a(y):
        return _rotate(y, table(0), table(1), lane, HD_A // 2, HD_A)

    def rope_b(y):
        return _rotate(y, table(2), table(3), lane, HD_B // 2, HD_B)

    def rope_i(y):
        return _rotate(y, table(4), table(5), lane, D_I_ROPE // 2, D_I)

    def big_dot():
        return _dot(u_ref[...], wbig_ref[...])

    @pl.when(j == 0)
    def _():
        u_ref[...] = _norm_modulate(h_ref[...], g_ref[...], sh_ref[0], sc_ref[0]).astype(MXU_DTYPE)
        sel = _dot(u_ref[...], wsel_ref[...])
        ra_ref[...] = big_dot()
        ka = rope_a(_head_norm(sel[:, 0:128], gain(1), lane, HD_A))
        kb = rope_b(_head_norm(sel[:, 256:384], gain(3), lane, HD_B))
        kv_ref[:, 0:128] = ka.astype(MXU_DTYPE)
        kv_ref[:, 128:256] = sel[:, 128:256].astype(MXU_DTYPE)
        kv_ref[:, 256:384] = kb.astype(MXU_DTYPE)
        kv_ref[:, 384:512] = sel[:, 384:512].astype(MXU_DTYPE)
        ki_ref[...] = rope_i(sel[:, 512:640]).astype(MXU_DTYPE)
        wi_ref[...] = sel[:, 640:768] * (H_I ** -0.5 * D_I ** -0.5)

    @pl.when(j == 1)
    def _():
        rb_ref[...] = big_dot()
        for k in range(8):
            y = _head_norm(ra_ref[:, k * LANES:(k + 1) * LANES], gain(0), lane, HD_A)
            qa_ref[:, k * LANES:(k + 1) * LANES] = rope_a(y).astype(MXU_DTYPE)

    @pl.when(j == 2)
    def _():
        ra_ref[...] = big_dot()
        for k in range(8):
            qi_ref[:, k * LANES:(k + 1) * LANES] = rope_i(rb_ref[:, k * LANES:(k + 1) * LANES]).astype(MXU_DTYPE)

    @pl.when(j == 3)
    def _():
        gates_ref[...] = jax.nn.sigmoid(big_dot()).astype(MXU_DTYPE)
        for k in range(8):
            y = _head_norm(ra_ref[:, k * LANES:(k + 1) * LANES], gain(2), lane, HD_B)
            qb_ref[:, k * LANES:(k + 1) * LANES] = (rope_b(y) * (HD_B ** -0.5)).astype(MXU_DTYPE)

    @pl.when(j >= 4)
    def _():
        gates_ref[...] = jax.nn.sigmoid(big_dot()).astype(MXU_DTYPE)


def _proj(h, shift, scale, g, wbig, wsel, gains, rope, *, seq, tm):
    n, d = h.shape
    nbig = wbig.shape[1]
    d_gate = (nbig - N_BIG) // 2
    per_b = seq // tm
    tn = 1024
    row = pl.BlockSpec((1, 1, d), lambda i, j: (i // per_b, 0, 0))
    tile = lambda w: pl.BlockSpec((tm, w), lambda i, j: (i, 0))
    bf = lambda w: jax.ShapeDtypeStruct((n, w), MXU_DTYPE)
    return pl.pallas_call(
        _proj_kernel,
        grid=(n // tm, nbig // tn),
        in_specs=[tile(d), row, row,
                  pl.BlockSpec((1, d), lambda i, j: (0, 0)),
                  pl.BlockSpec((d, tn), lambda i, j: (0, j)),
                  pl.BlockSpec((d, SEL_COLS), lambda i, j: (0, 0)),
                  pl.BlockSpec((4, LANES), lambda i, j: (0, 0)),
                  tile(6 * LANES)],
        out_specs=[tile(1024), tile(1024), tile(1024),
                   pl.BlockSpec((tm, tn), lambda i, j: (i, jnp.maximum(j - 3, 0))),
                   tile(512), tile(LANES), tile(LANES)],
        out_shape=[bf(1024), bf(1024), bf(1024), bf(2 * d_gate), bf(512), bf(LANES),
                   jax.ShapeDtypeStruct((n, LANES), f32)],
        scratch_shapes=[pltpu.VMEM((tm, d), MXU_DTYPE),
                        pltpu.VMEM((tm, tn), f32), pltpu.VMEM((tm, tn), f32)],
        compiler_params=_params("arbitrary", "arbitrary"),
        name="mix_proj",
    )(h, shift, scale, g, wbig, wsel, gains, rope)


QB = 256
KC = 256
I16_MIN = -32768
PACK = 16


def _attn_kernel(qa_ref, qi_ref, qb_ref, wi_ref, kv_ref, ki_ref, sink_ref,
                 ya_ref, yb_ref,
                 vat_ref, vbt_ref, qat_ref, qit_ref, key_ref, hi_ref, lo_ref, m_ref, l_ref, *head_refs,
                 n_sel):
    acc_refs, za_refs, zb_refs = head_refs[:H_A], head_refs[H_A:2 * H_A], head_refs[2 * H_A:]
    i = pl.program_id(1)
    nstep = pl.num_programs(1)
    nchunk = i + 1
    halves = (slice(0, BLOCK), slice(BLOCK, QB))

    def rows(c, size=KC):
        return pl.ds(pl.multiple_of(c * size, size), size)

    def t128(x):
        return x.astype(f32).T.astype(MXU_DTYPE)

    @pl.when(i == 0)
    def _():
        def body(c, carry):
            b0, b1 = rows(2 * c, BLOCK), rows(2 * c + 1, BLOCK)
            vat_ref[c] = jnp.concatenate([t128(kv_ref[b0, 128:256]), t128(kv_ref[b1, 128:256])], axis=1)
            vbt_ref[2 * c] = t128(kv_ref[b0, 384:512])
            vbt_ref[2 * c + 1] = t128(kv_ref[b1, 384:512])
            return carry
        lax.fori_loop(0, nstep, body, 0)

    for k in range(8):
        sl = slice(k * LANES, (k + 1) * LANES)
        for e, hs in enumerate(halves):
            qat_ref[:, k * QB + e * BLOCK:k * QB + (e + 1) * BLOCK] = t128(qa_ref[hs, sl])
            qit_ref[sl, hs] = t128(qi_ref[hs, sl])
    w_t = jnp.concatenate([wi_ref[hs, :].T for hs in halves], axis=1)

    q_lane = lax.broadcasted_iota(i32, (BLOCK, QB), 1)
    k_row = lax.broadcasted_iota(i32, (BLOCK, QB), 0)
    q_pos = i * QB + q_lane

    def idx_body(cc, carry):
        r = rows(cc, BLOCK)
        kc = ki_ref[r, 0:D_I]
        acc = jnp.zeros((BLOCK, QB), f32)
        for h in range(H_I):
            d = _dot(kc, qit_ref[h * D_I:(h + 1) * D_I, :])
            acc = acc + w_t[h:h + 1, :] * jnp.maximum(d, 0.0)
        bits = lax.bitcast_convert_type(acc + 0.0, i32)
        key = jnp.where(bits >= 0, bits, bits ^ 0x7FFFFFFF)
        key = jnp.where(cc * BLOCK + k_row <= q_pos, key, INT_MIN)
        key_ref[r, :] = key
        hi_ref[r, :] = (key >> 16).astype(jnp.int16)
        lo_ref[r, :] = ((key & 0xFFFF) + I16_MIN).astype(jnp.int16)
        return carry
    lax.fori_loop(0, 2 * nchunk, idx_body, 0)

    need = jnp.minimum(n_sel, q_pos[0:1, :] + 1).astype(f32)
    one, zero = jnp.ones((), jnp.bfloat16), jnp.zeros((), jnp.bfloat16)

    def count_ge(ref, cand):
        def body(c, cnt):
            ind = jnp.where(ref[rows(c), :] >= cand, one, zero)
            parts = [ind[g * PACK:(g + 1) * PACK] for g in range(KC // PACK)]
            while len(parts) > 1:
                parts = [a + b for a, b in zip(parts[0::2], parts[1::2])]
            return cnt + parts[0]
        cnt = lax.fori_loop(0, nchunk, body, jnp.zeros((PACK, QB), jnp.bfloat16))
        return jnp.sum(cnt.astype(f32), axis=0, keepdims=True)

    def bisect(ref, want):
        def body(k, t_u):
            cand_u = t_u | lax.shift_left(jnp.int32(1), 15 - k)
            ok = count_ge(ref, (cand_u + I16_MIN).astype(jnp.int16)) >= want
            return jnp.where(ok, cand_u, t_u)
        return lax.fori_loop(0, 16, body, jnp.zeros((1, QB), i32))

    hi_u = bisect(hi_ref, need)
    hi_t = (hi_u + I16_MIN).astype(jnp.int16)
    above = jnp.where(hi_u == 0xFFFF, 0.0,
                      count_ge(hi_ref, (jnp.minimum(hi_u + 1, 0xFFFF) + I16_MIN).astype(jnp.int16)))

    def tie_body(c, carry):
        lo_ref[rows(c), :] = jnp.where(hi_ref[rows(c), :] == hi_t, lo_ref[rows(c), :], jnp.int16(I16_MIN))
        return carry
    lax.fori_loop(0, nchunk, tie_body, 0)
    lo_u = bisect(lo_ref, need - above)
    thr = (hi_u + I16_MIN) * 65536 + lo_u

    m_ref[...] = jnp.full(m_ref.shape, M_INIT, f32)
    l_ref[...] = jnp.zeros(l_ref.shape, f32)
    for acc_ref in acc_refs:
        acc_ref[...] = jnp.zeros(acc_ref.shape, f32)
    c_a = HD_A ** -0.5 * LOG2E

    def logits_of(c, h):
        return _dot(kv_ref[rows(c), 0:128], qat_ref[:, h * QB:(h + 1) * QB])

    def zero_after(x):
        u = lax.bitcast_convert_type(x, jnp.uint32)
        u = lax.shift_right_logical(lax.shift_right_logical(u, jnp.uint32(16)), jnp.uint32(16))
        return lax.bitcast_convert_type(u, f32)

    def chunk_step(c, z_in, z_out):
        nxt = jnp.minimum(c + 1, nchunk - 1)
        vt = vat_ref[c]
        bias = jnp.where(key_ref[rows(c), :] >= thr, 0.0, NEG)
        m_all, l_all = m_ref[...], l_ref[...]
        m_out, l_out = [], []
        gate = None
        for h in range(H_A):
            hs = slice(h * QB, (h + 1) * QB)
            z = z_in[h][...] + bias
            m_old = m_all[:, hs]
            m_new = jnp.maximum(m_old, jnp.max(z, axis=0, keepdims=True))
            if gate is not None:
                m_new = m_new + gate
            z_next = logits_of(nxt, h)
            z_out[h][...] = z_next
            gate = zero_after(z_next[0:1, :])
            alpha = jnp.exp2((m_old - m_new) * c_a)
            p = jnp.exp2((z - m_new) * c_a)
            l_out.append(alpha * l_all[:, hs] + jnp.sum(p, axis=0, keepdims=True))
            acc_refs[h][...] = alpha * acc_refs[h][...] + _dot(vt, p.astype(MXU_DTYPE))
            m_out.append(m_new)
        m_ref[...] = jnp.concatenate(m_out, axis=1)
        l_ref[...] = jnp.concatenate(l_out, axis=1)

    for h in range(H_A):
        za_refs[h][...] = logits_of(0, h)

    def att_body(k2, carry):
        chunk_step(2 * k2, za_refs, zb_refs)

        @pl.when(2 * k2 + 1 < nchunk)
        def _():
            chunk_step(2 * k2 + 1, zb_refs, za_refs)
        return carry
    lax.fori_loop(0, (nchunk + 1) // 2, att_body, 0)

    inv_l = 1.0 / l_ref[...]
    for k in range(H_A):
        ya_t = acc_refs[k][...] * inv_l[:, k * QB:(k + 1) * QB]
        for e, hs in enumerate(halves):
            ya_ref[hs, k * LANES:(k + 1) * LANES] = ya_t[:, e * BLOCK:(e + 1) * BLOCK].T.astype(MXU_DTYPE)

    b_lane = lax.broadcasted_iota(i32, (BLOCK, LANES), 1)
    b_row = lax.broadcasted_iota(i32, (BLOCK, LANES), 0)
    sink = sink_ref[...]
    zero_pad = jnp.zeros((HD_B, LANES), MXU_DTYPE)
    per_g = H_B // KVH_B
    for e, hs in enumerate(halves):
        blk = 2 * i + e
        ip = jnp.maximum(blk - 1, 0)
        kband = jnp.concatenate([kv_ref[rows(ip, BLOCK), 256:384], kv_ref[rows(blk, BLOCK), 256:384]], axis=0)
        ext = []
        for k in range(8):
            t = t128(qb_ref[hs, k * LANES:(k + 1) * LANES])
            for o in range(2):
                piece = t[o * HD_B:(o + 1) * HD_B, :]
                grp = (2 * k + o) // per_g
                ext.append(jnp.concatenate([piece, zero_pad] if grp == 0 else [zero_pad, piece], axis=0))
        s = _dot(kband, jnp.concatenate(ext, axis=1))
        prev_open = jnp.where(blk > 0, 0.0, NEG)
        bias = jnp.concatenate([jnp.where(b_row > b_lane, prev_open, NEG),
                                jnp.where(b_row <= b_lane, 0.0, NEG)], axis=0)
        z = s + jnp.concatenate([bias] * H_B, axis=1)
        m = jnp.maximum(jnp.max(z, axis=0, keepdims=True), sink)
        ex = jnp.exp(z - m)
        inv = 1.0 / (jnp.sum(ex, axis=0, keepdims=True) + jnp.exp(sink - m))
        pb = ex.astype(MXU_DTYPE)
        pieces = []
        for g in range(KVH_B):
            gs = slice(g * HD_B, (g + 1) * HD_B)
            cols = slice(g * per_g * LANES, (g + 1) * per_g * LANES)
            vband = jnp.concatenate([vbt_ref[ip][gs, :], vbt_ref[blk][gs, :]], axis=1)
            og = _dot(vband, pb[:, cols]) * inv[:, cols]
            pieces += [og[:, hh * LANES:(hh + 1) * LANES] for hh in range(per_g)]
        for k in range(8):
            yb_ref[hs, k * LANES:(k + 1) * LANES] = (
                jnp.concatenate([pieces[2 * k], pieces[2 * k + 1]], axis=0).T.astype(MXU_DTYPE))


def _attn(qa, qi, qb, wi, kv, ki, sink_row, *, batch, seq):
    n = qa.shape[0]
    nstep = seq // QB
    n_sel = min(K_MAX, seq // 4)
    blk = lambda w: pl.BlockSpec((QB, w), lambda b, i: (b * nstep + i, 0))
    per_b = lambda w: pl.BlockSpec((seq, w), lambda b, i: (b, 0))
    return pl.pallas_call(
        functools.partial(_attn_kernel, n_sel=n_sel),
        grid=(batch, nstep),
        in_specs=[blk(1024), blk(1024), blk(1024), blk(LANES), per_b(512), per_b(LANES),
                  pl.BlockSpec((1, H_B * LANES), lambda b, i: (0, 0))],
        out_specs=[blk(1024), blk(1024)],
        out_shape=[jax.ShapeDtypeStruct((n, 1024), MXU_DTYPE)] * 2,
        scratch_shapes=[pltpu.VMEM((nstep, HD_A, KC), MXU_DTYPE),
                        pltpu.VMEM((seq // BLOCK, LANES, BLOCK), MXU_DTYPE),
                        pltpu.VMEM((HD_A, H_A * QB), MXU_DTYPE),
                        pltpu.VMEM((H_I * D_I, QB), MXU_DTYPE),
                        pltpu.VMEM((seq, QB), i32),
                        pltpu.VMEM((seq, QB), jnp.int16),
                        pltpu.VMEM((seq, QB), jnp.int16),
                        pltpu.VMEM((1, H_A * QB), f32),
                        pltpu.VMEM((1, H_A * QB), f32)]
                       + [pltpu.VMEM((HD_A, QB), f32)] * H_A
                       + [pltpu.VMEM((KC, QB), f32)] * (2 * H_A),

        compiler_params=_params("arbitrary", "arbitrary"),
        name="mixers",
    )(qa, qi, qb, wi, kv, ki, sink_row)


def _out_kernel(h_ref, ya_ref, yb_ref, ga_ref, gb_ref, gt_ref, woa_ref, wob_ref, wout_ref, o_ref):
    @pl.when(pl.program_id(1) == 0)
    def _():
        o_ref[...] = h_ref[...]

    merged = (ga_ref[...].astype(f32) * _dot(ya_ref[...], woa_ref[...])
              + gb_ref[...].astype(f32) * _dot(yb_ref[...], wob_ref[...]))
    o_ref[...] += gt_ref[0] * _dot(merged.astype(MXU_DTYPE), wout_ref[...])


def _out_proj(h, ya, yb, gates, gate, woa, wob, wout, *, seq, tm, tn):
    n, d = h.shape
    k = ya.shape[1]
    per_b = seq // tm
    nj = d // tn
    return pl.pallas_call(
        _out_kernel,
        grid=(n // tm, nj),
        in_specs=[pl.BlockSpec((tm, d), lambda i, j: (i, 0)),
                  pl.BlockSpec((tm, k), lambda i, j: (i, 0)),
                  pl.BlockSpec((tm, k), lambda i, j: (i, 0)),
                  pl.BlockSpec((tm, tn), lambda i, j: (i, j)),
                  pl.BlockSpec((tm, tn), lambda i, j: (i, nj + j)),
                  pl.BlockSpec((1, 1, d), lambda i, j: (i // per_b, 0, 0)),
                  pl.BlockSpec((k, tn), lambda i, j: (0, j)),
                  pl.BlockSpec((k, tn), lambda i, j: (0, j)),
                  pl.BlockSpec((tn, d), lambda i, j: (j, 0))],
        out_specs=pl.BlockSpec((tm, d), lambda i, j: (i, 0)),
        out_shape=jax.ShapeDtypeStruct((n, d), f32),
        compiler_params=_params("arbitrary", "arbitrary"),
        name="out_proj",
    )(h, ya, yb, gates, gates, gate, woa, wob, wout)


def _tiles(seq, d_ff):
    tm = min(seq, 512)
    tf = 512 if d_ff % 512 == 0 else 256
    return tm, tf


def _split_w_in(w_in, d):
    sizes = (H_A * HD_A, HD_A, HD_A, H_I * D_I, D_I, H_I, H_B * HD_B, KVH_B * HD_B, KVH_B * HD_B, d, d)
    offs = [0]
    for s in sizes:
        offs.append(offs[-1] + s)
    seg = lambda k: w_in[:, :, offs[k]:offs[k + 1]]
    pad = lambda a, w: jnp.pad(a, ((0, 0), (0, 0), (0, w - a.shape[-1])))
    big = jnp.concatenate([seg(0), seg(3), seg(6), seg(9), seg(10)], axis=-1)
    sel = jnp.concatenate([seg(1), seg(2), seg(7), seg(8), pad(seg(4), LANES), pad(seg(5), LANES)], axis=-1)
    return big.astype(MXU_DTYPE), sel.astype(MXU_DTYPE)


def kernel(x, c, positions, ada_w, ada_b, norm_ffn1_g, ffn1_w_gate, ffn1_w_up, ffn1_w_down, norm_mix_g, w_in, qn_a_g, kn_a_g, qn_b_g, kn_b_g, sinks, w_o_a, w_o_b, w_out, norm_ffn2_g, ffn2_w_gate, ffn2_w_up, ffn2_w_down):
    batch, seq, d = x.shape
    depth = ada_w.shape[0]
    d_ff = ffn1_w_gate.shape[-1]
    n = batch * seq
    assert seq % QB == 0 and d % 1024 == 0
    tm, tf = _tiles(seq, d_ff)
    tn = min(d, 512)

    mod = _ada_mod(c, ada_w, ada_b).reshape(depth, batch, N_MOD, 1, d)
    rope = _rope_tables(positions)
    wbig, wsel = _split_w_in(w_in, d)
    cast = lambda w: w.astype(MXU_DTYPE)
    w1 = (cast(ffn1_w_gate), cast(ffn1_w_up), cast(ffn1_w_down))
    w2 = (cast(ffn2_w_gate), cast(ffn2_w_up), cast(ffn2_w_down))
    woa, wob, wout = cast(w_o_a), cast(w_o_b), cast(w_out)
    gains = jnp.stack([qn_a_g, kn_a_g, jnp.tile(qn_b_g, (1, LANES // HD_B)),
                       jnp.tile(kn_b_g, (1, LANES // HD_B))], axis=1)
    sink_rows = jnp.repeat(sinks, LANES, axis=1).reshape(depth, 1, H_B * LANES)

    h = x.reshape(n, d)
    for l in range(depth):
        m = [mod[l, :, k] for k in range(N_MOD)]
        h = _ffn(h, m[0], m[1], m[2], norm_ffn1_g[l][None], w1[0][l], w1[1][l], w1[2][l],
                 seq=seq, tm=tm, tf=tf, tn=tn)
        qa, qi, qb, gates, kv, ki, wi = _proj(h, m[3], m[4], norm_mix_g[l][None], wbig[l], wsel[l],
                                              gains[l], rope, seq=seq, tm=tm)
        ya, yb = _attn(qa, qi, qb, wi, kv, ki, sink_rows[l], batch=batch, seq=seq)
        h = _out_proj(h, ya, yb, gates, m[5], woa[l], wob[l], wout[l], seq=seq, tm=tm, tn=tn)
        h = _ffn(h, m[6], m[7], m[8], norm_ffn2_g[l][None], w2[0][l], w2[1][l], w2[2][l],
                 seq=seq, tm=tm, tf=tf, tn=tn)
    return h.reshape(batch, seq, d)
```

```python
import functools

import jax
import jax.numpy as jnp
from jax import lax
from jax.experimental import pallas as pl
from jax.experimental.pallas import tpu as pltpu

H_A, HD_A = 8, 128
H_I, D_I, D_I_ROPE = 16, 64, 32
K_MAX = 256
H_B, HD_B, KVH_B = 16, 64, 2
BLOCK = 128
ROPE_THETA = 10000.0
EPS = 1e-6
NEG = -1e30
N_MOD = 9

LANES = 128
V7X_VMEM_LIMIT_BYTES = 60 * 1024 * 1024
MXU_DTYPE = jnp.bfloat16

INT_MIN = -2147483648
M_INIT = -1e29
LOG2E = 1.4426950408889634
SEL_COLS = 768
N_BIG = 3 * 1024

f32 = jnp.float32
i32 = jnp.int32


def _dot(a, b):
    return jnp.dot(a, b, preferred_element_type=f32)


def _params(*sem):
    return pltpu.CompilerParams(dimension_semantics=sem, vmem_limit_bytes=V7X_VMEM_LIMIT_BYTES)


def _ada_kernel(c_ref, w_ref, b_ref, o_ref):
    c = c_ref[...]
    act = (c * jax.nn.sigmoid(c)).astype(MXU_DTYPE)
    o_ref[0] = _dot(act, w_ref[0].astype(MXU_DTYPE)) + b_ref[0]


def _ada_mod(c, ada_w, ada_b):
    L, D, N = ada_w.shape
    B = c.shape[0]
    tn = min(N, 1024)
    return pl.pallas_call(
        _ada_kernel,
        grid=(L, N // tn),
        in_specs=[pl.BlockSpec((B, D), lambda l, j: (0, 0)),
                  pl.BlockSpec((1, D, tn), lambda l, j: (l, 0, j)),
                  pl.BlockSpec((1, 1, tn), lambda l, j: (l, 0, j))],
        out_specs=pl.BlockSpec((1, B, tn), lambda l, j: (l, 0, j)),
        out_shape=jax.ShapeDtypeStruct((L, B, N), f32),
        compiler_params=_params("arbitrary", "arbitrary"),
        name="ada_mod",
    )(c, ada_w, ada_b.reshape(L, 1, N))


def _rope_kernel(pos_ref, freq_ref, sign_ref, o_ref):
    pos = pos_ref[...].astype(f32)
    for k in range(3):
        ang = pos * freq_ref[k:k + 1, :]
        o_ref[:, (2 * k) * LANES:(2 * k + 1) * LANES] = jnp.cos(ang)
        o_ref[:, (2 * k + 1) * LANES:(2 * k + 2) * LANES] = jnp.sin(ang) * sign_ref[k:k + 1, :]


def _rope_rows():
    lane = jnp.arange(LANES)

    def inv_freq(dim):
        return ROPE_THETA ** (-jnp.arange(0, dim, 2, dtype=f32) / dim)

    fa = inv_freq(HD_A)[lane % (HD_A // 2)]
    sa = jnp.where(lane % HD_A < HD_A // 2, -1.0, 1.0)
    fb = inv_freq(HD_B)[lane % (HD_B // 2)]
    sb = jnp.where(lane % HD_B < HD_B // 2, -1.0, 1.0)
    in_rope = lane % D_I < D_I_ROPE
    fi = jnp.where(in_rope, inv_freq(D_I_ROPE)[lane % (D_I_ROPE // 2)], 0.0)
    si = jnp.where(lane % D_I < D_I_ROPE // 2, -1.0, 1.0)
    return jnp.stack([fa, fb, fi]).astype(f32), jnp.stack([sa, sb, si]).astype(f32)


def _rope_tables(positions):
    n = positions.size
    tm = min(n, 1024)
    freq, sign = _rope_rows()
    return pl.pallas_call(
        _rope_kernel,
        grid=(n // tm,),
        in_specs=[pl.BlockSpec((tm, 1), lambda i: (i, 0)),
                  pl.BlockSpec((3, LANES), lambda i: (0, 0)),
                  pl.BlockSpec((3, LANES), lambda i: (0, 0))],
        out_specs=pl.BlockSpec((tm, 6 * LANES), lambda i: (i, 0)),
        out_shape=jax.ShapeDtypeStruct((n, 6 * LANES), f32),
        compiler_params=_params("arbitrary"),
        name="rope_tables",
    )(positions.reshape(n, 1), freq, sign)


def _norm_modulate(x, g, shift, scale):
    ms = jnp.mean(x * x, axis=-1, keepdims=True)
    return x * lax.rsqrt(ms + EPS) * (g * (1.0 + scale)) + shift


def _head_norm(x, gain, lane, head_dim):
    sq = x * x
    if head_dim == LANES:
        ms = jnp.mean(sq, axis=-1, keepdims=True)
    else:
        lo = lane < head_dim
        s_lo = jnp.sum(jnp.where(lo, sq, 0.0), axis=-1, keepdims=True)
        s_hi = jnp.sum(jnp.where(lo, 0.0, sq), axis=-1, keepdims=True)
        ms = jnp.where(lo, s_lo, s_hi) * (1.0 / head_dim)
    return x * lax.rsqrt(ms + EPS) * gain


def _rotate(y, cos, sin, lane, half, period):
    if 2 * half == LANES:
        partner = pltpu.roll(y, half, 1)
    else:
        fwd = pltpu.roll(y, LANES - half, 1)
        bwd = pltpu.roll(y, half, 1)
        partner = jnp.where(lane % period < half, fwd, bwd)
    return y * cos + partner * sin


def _ffn_kernel(h_ref, sh_ref, sc_ref, gt_ref, g_ref, wg_ref, wu_ref, wd_ref, o_ref, u_ref, *, rows, tn):
    j = pl.program_id(1)

    @pl.when(j == 0)
    def _():
        x = h_ref[...]
        u_ref[...] = _norm_modulate(x, g_ref[...], sh_ref[0], sc_ref[0]).astype(MXU_DTYPE)
        o_ref[...] = x

    half_gate = 0.5 * gt_ref[0]
    for r in range(u_ref.shape[0] // rows):
        rs = slice(r * rows, (r + 1) * rows)
        u = u_ref[rs, :]
        a = _dot(u, wg_ref[...])
        b = _dot(u, wu_ref[...])
        act = (a * jax.nn.sigmoid(a) * b).astype(MXU_DTYPE)
        for c in range(o_ref.shape[1] // tn):
            sl = slice(c * tn, (c + 1) * tn)
            o_ref[rs, sl] += half_gate[:, sl] * _dot(act, wd_ref[:, sl])


def _ffn(h, shift, scale, gate, g, wg, wu, wd, layer, *, seq, tm, rows, tf, tn):
    n, d = h.shape
    f = wg.shape[-1]
    per_b = seq // tm
    row = pl.BlockSpec((1, 1, d), lambda i, j: (i // per_b, 0, 0))
    return pl.pallas_call(
        functools.partial(_ffn_kernel, rows=rows, tn=tn),
        grid=(n // tm, f // tf),
        in_specs=[pl.BlockSpec((tm, d), lambda i, j: (i, 0)), row, row, row,
                  pl.BlockSpec((1, d), lambda i, j: (0, 0)),
                  pl.BlockSpec((None, d, tf), lambda i, j: (layer, 0, j)),
                  pl.BlockSpec((None, d, tf), lambda i, j: (layer, 0, j)),
                  pl.BlockSpec((None, tf, d), lambda i, j: (layer, j, 0))],
        out_specs=pl.BlockSpec((tm, d), lambda i, j: (i, 0)),
        out_shape=jax.ShapeDtypeStruct((n, d), f32),
        scratch_shapes=[pltpu.VMEM((tm, d), MXU_DTYPE)],
        compiler_params=_params("arbitrary", "arbitrary"),
        name="ffn",
    )(h, shift, scale, gate, g, wg, wu, wd)


def _proj_kernel(h_ref, sh_ref, sc_ref, g_ref, wbig_ref, wsel_ref, gains_ref, rope_ref,
                 qa_ref, qi_ref, qb_ref, gates_ref, kv_ref, ki_ref, wi_ref, u_ref, ra_ref, rb_ref):
    j = pl.program_id(1)
    tm = h_ref.shape[0]
    lane = lax.broadcasted_iota(i32, (tm, LANES), 1)

    def table(k):
        return rope_ref[:, k * LANES:(k + 1) * LANES]

    def gain(k):
        return gains_ref[k:k + 1, :]

    def rope_a(y):
        return _rotate(y, table(0), table(1), lane, HD_A // 2, HD_A)

    def rope_b(y):
        return _rotate(y, table(2), table(3), lane, HD_B // 2, HD_B)

    def rope_i(y):
        return _rotate(y, table(4), table(5), lane, D_I_ROPE // 2, D_I)

    def big_dot():
        return _dot(u_ref[...], wbig_ref[...])

    @pl.when(j == 0)
    def _():
        u_ref[...] = _norm_modulate(h_ref[...], g_ref[...], sh_ref[0], sc_ref[0]).astype(MXU_DTYPE)
        sel = _dot(u_ref[...], wsel_ref[...])
        ra_ref[...] = big_dot()
        ka = rope_a(_head_norm(sel[:, 0:128], gain(1), lane, HD_A))
        kb = rope_b(_head_norm(sel[:, 256:384], gain(3), lane, HD_B))
        kv_ref[:, 0:128] = ka.astype(MXU_DTYPE)
        kv_ref[:, 128:256] = sel[:, 128:256].astype(MXU_DTYPE)
        kv_ref[:, 256:384] = kb.astype(MXU_DTYPE)
        kv_ref[:, 384:512] = sel[:, 384:512].astype(MXU_DTYPE)
        ki_ref[...] = rope_i(sel[:, 512:640]).astype(MXU_DTYPE)
        wi_ref[...] = sel[:, 640:768] * (H_I ** -0.5 * D_I ** -0.5)

    @pl.when(j == 1)
    def _():
        rb_ref[...] = big_dot()
        for k in range(8):
            y = _head_norm(ra_ref[:, k * LANES:(k + 1) * LANES], gain(0), lane, HD_A)
            qa_ref[:, k * LANES:(k + 1) * LANES] = rope_a(y).astype(MXU_DTYPE)

    @pl.when(j == 2)
    def _():
        ra_ref[...] = big_dot()
        for k in range(8):
            qi_ref[:, k * LANES:(k + 1) * LANES] = rope_i(rb_ref[:, k * LANES:(k + 1) * LANES]).astype(MXU_DTYPE)

    @pl.when(j == 3)
    def _():
        rb_ref[...] = big_dot()
        for k in range(8):
            y = _head_norm(ra_ref[:, k * LANES:(k + 1) * LANES], gain(2), lane, HD_B)
            qb_ref[:, k * LANES:(k + 1) * LANES] = (rope_b(y) * (HD_B ** -0.5)).astype(MXU_DTYPE)

    def gate_cols(c):
        return slice(c * ra_ref.shape[1], (c + 1) * ra_ref.shape[1])

    @pl.when(j == 4)
    def _():
        gates_ref[:, gate_cols(1)] = jax.nn.sigmoid(big_dot()).astype(MXU_DTYPE)
        gates_ref[:, gate_cols(0)] = jax.nn.sigmoid(rb_ref[...]).astype(MXU_DTYPE)

    for c in range(2, gates_ref.shape[1] // ra_ref.shape[1]):
        @pl.when(j == 3 + c)
        def _(c=c):
            gates_ref[:, gate_cols(c)] = jax.nn.sigmoid(big_dot()).astype(MXU_DTYPE)


def _proj(h, shift, scale, g, wbig, wsel, gains, rope, layer, *, seq, tm):
    n, d = h.shape
    nbig = wbig.shape[-1]
    d_gate = (nbig - N_BIG) // 2
    per_b = seq // tm
    tn = 1024
    row = pl.BlockSpec((1, 1, d), lambda i, j: (i // per_b, 0, 0))
    tile = lambda w: pl.BlockSpec((tm, w), lambda i, j: (i, 0))
    bf = lambda w: jax.ShapeDtypeStruct((n, w), MXU_DTYPE)
    return pl.pallas_call(
        _proj_kernel,
        grid=(n // tm, nbig // tn),
        in_specs=[tile(d), row, row,
                  pl.BlockSpec((1, d), lambda i, j: (0, 0)),
                  pl.BlockSpec((None, d, tn), lambda i, j: (layer, 0, j)),
                  pl.BlockSpec((None, d, SEL_COLS), lambda i, j: (layer, 0, 0)),
                  pl.BlockSpec((4, LANES), lambda i, j: (0, 0)),
                  tile(6 * LANES)],
        out_specs=[tile(1024), tile(1024), tile(1024), tile(2 * d_gate),
                   tile(512), tile(LANES), tile(LANES)],
        out_shape=[bf(1024), bf(1024), bf(1024), bf(2 * d_gate), bf(512), bf(LANES),
                   jax.ShapeDtypeStruct((n, LANES), f32)],
        scratch_shapes=[pltpu.VMEM((tm, d), MXU_DTYPE),
                        pltpu.VMEM((tm, tn), f32), pltpu.VMEM((tm, tn), f32)],
        compiler_params=_params("arbitrary", "arbitrary"),
        name="mix_proj",
    )(h, shift, scale, g, wbig, wsel, gains, rope)


QB = 256
KC = 256
I16_MIN = -32768
PACK = 16


def _attn_kernel(qa_ref, qi_ref, qb_ref, wi_ref, kv_ref, ki_ref, sink_ref,
                 ya_ref, yb_ref,
                 vat_ref, vbt_ref, qat_ref, qit_ref, key_ref, hi_ref, lo_ref, m_ref, l_ref, *head_refs,
                 n_sel):
    acc_refs, za_refs, zb_refs = head_refs[:H_A], head_refs[H_A:2 * H_A], head_refs[2 * H_A:]
    i = pl.program_id(1)
    nstep = pl.num_programs(1)
    nchunk = i + 1
    halves = (slice(0, BLOCK), slice(BLOCK, QB))

    def rows(c, size=KC):
        return pl.ds(pl.multiple_of(c * size, size), size)

    def t128(x):
        return x.astype(f32).T.astype(MXU_DTYPE)

    @pl.when(i == 0)
    def _():
        def body(c, carry):
            b0, b1 = rows(2 * c, BLOCK), rows(2 * c + 1, BLOCK)
            vat_ref[c] = jnp.concatenate([t128(kv_ref[b0, 128:256]), t128(kv_ref[b1, 128:256])], axis=1)
            vbt_ref[2 * c] = t128(kv_ref[b0, 384:512])
            vbt_ref[2 * c + 1] = t128(kv_ref[b1, 384:512])
            return carry
        lax.fori_loop(0, nstep, body, 0)

    for k in range(8):
        sl = slice(k * LANES, (k + 1) * LANES)
        for e, hs in enumerate(halves):
            qat_ref[:, k * QB + e * BLOCK:k * QB + (e + 1) * BLOCK] = t128(qa_ref[hs, sl])
            qit_ref[sl, hs] = t128(qi_ref[hs, sl])
    w_t = jnp.concatenate([wi_ref[hs, :].T for hs in halves], axis=1)

    q_lane = lax.broadcasted_iota(i32, (BLOCK, QB), 1)
    k_row = lax.broadcasted_iota(i32, (BLOCK, QB), 0)
    q_pos = i * QB + q_lane

    def idx_body(c, carry):
        for e in range(KC // BLOCK):
            cc = (KC // BLOCK) * c + e
            r = rows(cc, BLOCK)
            kc = ki_ref[r, 0:D_I]
            acc = jnp.zeros((BLOCK, QB), f32)
            for h in range(H_I):
                d = _dot(kc, qit_ref[h * D_I:(h + 1) * D_I, :])
                acc = acc + w_t[h:h + 1, :] * jnp.maximum(d, 0.0)
            bits = lax.bitcast_convert_type(acc + 0.0, i32)
            key = jnp.where(bits >= 0, bits, bits ^ 0x7FFFFFFF)
            key = jnp.where(cc * BLOCK + k_row <= q_pos, key, INT_MIN)
            key_ref[r, :] = key
            hi_ref[r, :] = (key >> 16).astype(jnp.int16)
            lo_ref[r, :] = ((key & 0xFFFF) + I16_MIN).astype(jnp.int16)
        return carry
    lax.fori_loop(0, nchunk, idx_body, 0)

    need = jnp.minimum(n_sel, q_pos[0:1, :] + 1).astype(f32)
    one, zero = jnp.ones((), jnp.bfloat16), jnp.zeros((), jnp.bfloat16)

    def count_ge(ref, cand):
        def body(c, cnt):
            ind = jnp.where(ref[rows(c), :] >= cand, one, zero)
            parts = [ind[g * PACK:(g + 1) * PACK] for g in range(KC // PACK)]
            while len(parts) > 1:
                parts = [a + b for a, b in zip(parts[0::2], parts[1::2])]
            return cnt + parts[0]
        cnt = lax.fori_loop(0, nchunk, body, jnp.zeros((PACK, QB), jnp.bfloat16))
        return jnp.sum(cnt.astype(f32), axis=0, keepdims=True)

    def bisect(ref, want):
        def body(k, t_u):
            cand_u = t_u | lax.shift_left(jnp.int32(1), 15 - k)
            ok = count_ge(ref, (cand_u + I16_MIN).astype(jnp.int16)) >= want
            return jnp.where(ok, cand_u, t_u)
        return lax.fori_loop(0, 16, body, jnp.zeros((1, QB), i32))

    hi_u = bisect(hi_ref, need)
    hi_t = (hi_u + I16_MIN).astype(jnp.int16)
    above = jnp.where(hi_u == 0xFFFF, 0.0,
                      count_ge(hi_ref, (jnp.minimum(hi_u + 1, 0xFFFF) + I16_MIN).astype(jnp.int16)))

    def tie_body(c, carry):
        lo_ref[rows(c), :] = jnp.where(hi_ref[rows(c), :] == hi_t, lo_ref[rows(c), :], jnp.int16(I16_MIN))
        return carry
    lax.fori_loop(0, nchunk, tie_body, 0)
    lo_u = bisect(lo_ref, need - above)
    thr = (hi_u + I16_MIN) * 65536 + lo_u

    picked = above + count_ge(lo_ref, (lo_u + I16_MIN).astype(jnp.int16))

    @pl.when(jnp.max(picked - need) > 0.0)
    def _():
        def gt_body(c, cnt):
            return cnt + jnp.sum(jnp.where(key_ref[rows(c), :] > thr, 1.0, 0.0), axis=0, keepdims=True)
        allowed = need - lax.fori_loop(0, nchunk, gt_body, jnp.zeros((1, QB), f32))
        tri = (lax.broadcasted_iota(i32, (KC, KC), 0) >= lax.broadcasted_iota(i32, (KC, KC), 1))
        tri = jnp.where(tri, 1.0, 0.0).astype(jnp.bfloat16)

        def fix_body(c, seen):
            k = key_ref[rows(c), :]
            tie = jnp.where(k == thr, 1.0, 0.0)
            rank = _dot(tri, tie.astype(jnp.bfloat16)) + seen
            key_ref[rows(c), :] = jnp.where((tie > 0.0) & (rank > allowed), thr - 1, k)
            return seen + jnp.sum(tie, axis=0, keepdims=True)
        lax.fori_loop(0, nchunk, fix_body, jnp.zeros((1, QB), f32))

    m_ref[...] = jnp.full(m_ref.shape, M_INIT, f32)
    l_ref[...] = jnp.zeros(l_ref.shape, f32)
    for acc_ref in acc_refs:
        acc_ref[...] = jnp.zeros(acc_ref.shape, f32)
    c_a = HD_A ** -0.5 * LOG2E

    def logits_of(c, h):
        return _dot(kv_ref[rows(c), 0:128], qat_ref[:, h * QB:(h + 1) * QB])

    def zero_after(x):
        u = lax.bitcast_convert_type(x, jnp.uint32)
        u = lax.shift_right_logical(lax.shift_right_logical(u, jnp.uint32(16)), jnp.uint32(16))
        return lax.bitcast_convert_type(u, f32)

    def chunk_step(c, z_in, z_out):
        nxt = jnp.minimum(c + 1, nchunk - 1)
        vt = vat_ref[c]
        bias = jnp.where(key_ref[rows(c), :] >= thr, 0.0, NEG)
        m_all, l_all = m_ref[...], l_ref[...]
        m_out, l_out = [], []
        gate = None
        for h in range(H_A):
            hs = slice(h * QB, (h + 1) * QB)
            z = z_in[h][...] + bias
            m_old = m_all[:, hs]
            m_new = jnp.maximum(m_old, jnp.max(z, axis=0, keepdims=True))
            if gate is not None:
                m_new = m_new + gate
            z_next = logits_of(nxt, h)
            z_out[h][...] = z_next
            gate = zero_after(z_next[0:1, :])
            alpha = jnp.exp2((m_old - m_new) * c_a)
            p = jnp.exp2((z - m_new) * c_a)
            l_out.append(alpha * l_all[:, hs] + jnp.sum(p, axis=0, keepdims=True))
            acc_refs[h][...] = alpha * acc_refs[h][...] + _dot(vt, p.astype(MXU_DTYPE))
            m_out.append(m_new)
        m_ref[...] = jnp.concatenate(m_out, axis=1)
        l_ref[...] = jnp.concatenate(l_out, axis=1)

    for h in range(H_A):
        za_refs[h][...] = logits_of(0, h)

    def att_body(k2, carry):
        chunk_step(2 * k2, za_refs, zb_refs)

        @pl.when(2 * k2 + 1 < nchunk)
        def _():
            chunk_step(2 * k2 + 1, zb_refs, za_refs)
        return carry
    lax.fori_loop(0, (nchunk + 1) // 2, att_body, 0)

    inv_l = 1.0 / l_ref[...]
    for k in range(H_A):
        ya_t = acc_refs[k][...] * inv_l[:, k * QB:(k + 1) * QB]
        for e, hs in enumerate(halves):
            ya_ref[hs, k * LANES:(k + 1) * LANES] = ya_t[:, e * BLOCK:(e + 1) * BLOCK].T.astype(MXU_DTYPE)

    b_lane = lax.broadcasted_iota(i32, (BLOCK, LANES), 1)
    b_row = lax.broadcasted_iota(i32, (BLOCK, LANES), 0)
    sink = sink_ref[...]
    zero_pad = jnp.zeros((HD_B, LANES), MXU_DTYPE)
    per_g = H_B // KVH_B
    for e, hs in enumerate(halves):
        blk = 2 * i + e
        ip = jnp.maximum(blk - 1, 0)
        kband = jnp.concatenate([kv_ref[rows(ip, BLOCK), 256:384], kv_ref[rows(blk, BLOCK), 256:384]], axis=0)
        ext = []
        for k in range(8):
            t = t128(qb_ref[hs, k * LANES:(k + 1) * LANES])
            for o in range(2):
                piece = t[o * HD_B:(o + 1) * HD_B, :]
                grp = (2 * k + o) // per_g
                ext.append(jnp.concatenate([piece, zero_pad] if grp == 0 else [zero_pad, piece], axis=0))
        s = _dot(kband, jnp.concatenate(ext, axis=1))
        prev_open = jnp.where(blk > 0, 0.0, NEG)
        bias = jnp.concatenate([jnp.where(b_row > b_lane, prev_open, NEG),
                                jnp.where(b_row <= b_lane, 0.0, NEG)], axis=0)
        bias2 = jnp.concatenate([bias, bias], axis=1)
        vbands = [jnp.concatenate([vbt_ref[ip][g * HD_B:(g + 1) * HD_B, :],
                                   vbt_ref[blk][g * HD_B:(g + 1) * HD_B, :]], axis=1) for g in range(KVH_B)]
        for k in range(8):
            cols = slice(2 * k * LANES, (2 * k + 2) * LANES)
            z = s[:, cols] + bias2
            m = jnp.maximum(jnp.max(z, axis=0, keepdims=True), sink[:, cols])
            ex = jnp.exp(z - m)
            inv = 1.0 / (jnp.sum(ex, axis=0, keepdims=True) + jnp.exp(sink[:, cols] - m))
            og = _dot(vbands[2 * k // per_g], ex.astype(MXU_DTYPE)) * inv
            yb_ref[hs, k * LANES:(k + 1) * LANES] = (
                jnp.concatenate([og[:, 0:LANES], og[:, LANES:2 * LANES]], axis=0).T.astype(MXU_DTYPE))


def _attn(qa, qi, qb, wi, kv, ki, sink_row, *, batch, seq):
    n = qa.shape[0]
    nstep = seq // QB
    n_sel = min(K_MAX, seq // 4)
    blk = lambda w: pl.BlockSpec((QB, w), lambda b, i: (b * nstep + i, 0))
    per_b = lambda w: pl.BlockSpec((seq, w), lambda b, i: (b, 0))
    return pl.pallas_call(
        functools.partial(_attn_kernel, n_sel=n_sel),
        grid=(batch, nstep),
        in_specs=[blk(1024), blk(1024), blk(1024), blk(LANES), per_b(512), per_b(LANES),
                  pl.BlockSpec((1, H_B * LANES), lambda b, i: (0, 0))],
        out_specs=[blk(1024), blk(1024)],
        out_shape=[jax.ShapeDtypeStruct((n, 1024), MXU_DTYPE)] * 2,
        scratch_shapes=[pltpu.VMEM((nstep, HD_A, KC), MXU_DTYPE),
                        pltpu.VMEM((seq // BLOCK, LANES, BLOCK), MXU_DTYPE),
                        pltpu.VMEM((HD_A, H_A * QB), MXU_DTYPE),
                        pltpu.VMEM((H_I * D_I, QB), MXU_DTYPE),
                        pltpu.VMEM((seq, QB), i32),
                        pltpu.VMEM((seq, QB), jnp.int16),
                        pltpu.VMEM((seq, QB), jnp.int16),
                        pltpu.VMEM((1, H_A * QB), f32),
                        pltpu.VMEM((1, H_A * QB), f32)]
                       + [pltpu.VMEM((HD_A, QB), f32)] * H_A
                       + [pltpu.VMEM((KC, QB), f32)] * (2 * H_A),

        compiler_params=_params("arbitrary", "arbitrary"),
        name="mixers",
    )(qa, qi, qb, wi, kv, ki, sink_row)


def _out_kernel(h_ref, ya_ref, yb_ref, ga_ref, gb_ref, gt_ref, woa_ref, wob_ref, wout_ref, o_ref):
    @pl.when(pl.program_id(1) == 0)
    def _():
        o_ref[...] = h_ref[...]

    merged = (ga_ref[...].astype(f32) * _dot(ya_ref[...], woa_ref[...])
              + gb_ref[...].astype(f32) * _dot(yb_ref[...], wob_ref[...]))
    o_ref[...] += gt_ref[0] * _dot(merged.astype(MXU_DTYPE), wout_ref[...])


def _out_proj(h, ya, yb, gates, gate, woa, wob, wout, layer, *, seq, tm, tn):
    n, d = h.shape
    k = ya.shape[1]
    per_b = seq // tm
    nj = d // tn
    return pl.pallas_call(
        _out_kernel,
        grid=(n // tm, nj),
        in_specs=[pl.BlockSpec((tm, d), lambda i, j: (i, 0)),
                  pl.BlockSpec((tm, k), lambda i, j: (i, 0)),
                  pl.BlockSpec((tm, k), lambda i, j: (i, 0)),
                  pl.BlockSpec((tm, tn), lambda i, j: (i, j)),
                  pl.BlockSpec((tm, tn), lambda i, j: (i, nj + j)),
                  pl.BlockSpec((1, 1, d), lambda i, j: (i // per_b, 0, 0)),
                  pl.BlockSpec((None, k, tn), lambda i, j: (layer, 0, j)),
                  pl.BlockSpec((None, k, tn), lambda i, j: (layer, 0, j)),
                  pl.BlockSpec((None, tn, d), lambda i, j: (layer, j, 0))],
        out_specs=pl.BlockSpec((tm, d), lambda i, j: (i, 0)),
        out_shape=jax.ShapeDtypeStruct((n, d), f32),
        compiler_params=_params("arbitrary", "arbitrary"),
        name="out_proj",
    )(h, ya, yb, gates, gates, gate, woa, wob, wout)


def _tiles(seq, d_ff):
    tm = min(seq, 512)
    tm_ffn = min(seq, 1024)
    tf = 512 if d_ff % 512 == 0 else 256
    return tm, tm_ffn, tf


def _split_w_in(w_in, d):
    sizes = (H_A * HD_A, HD_A, HD_A, H_I * D_I, D_I, H_I, H_B * HD_B, KVH_B * HD_B, KVH_B * HD_B, d, d)
    offs = [0]
    for s in sizes:
        offs.append(offs[-1] + s)
    seg = lambda k: w_in[:, :, offs[k]:offs[k + 1]]
    pad = lambda a, w: jnp.pad(a, ((0, 0), (0, 0), (0, w - a.shape[-1])))
    big = jnp.concatenate([seg(0), seg(3), seg(6), seg(9), seg(10)], axis=-1)
    sel = jnp.concatenate([seg(1), seg(2), seg(7), seg(8), pad(seg(4), LANES), pad(seg(5), LANES)], axis=-1)
    return big.astype(MXU_DTYPE), sel.astype(MXU_DTYPE)


def kernel(x, c, positions, ada_w, ada_b, norm_ffn1_g, ffn1_w_gate, ffn1_w_up, ffn1_w_down, norm_mix_g, w_in, qn_a_g, kn_a_g, qn_b_g, kn_b_g, sinks, w_o_a, w_o_b, w_out, norm_ffn2_g, ffn2_w_gate, ffn2_w_up, ffn2_w_down):
    batch, seq, d = x.shape
    depth = ada_w.shape[0]
    d_ff = ffn1_w_gate.shape[-1]
    n = batch * seq
    assert seq % QB == 0 and d % 1024 == 0
    tm, tm_ffn, tf = _tiles(seq, d_ff)
    tn = min(d, 512)

    mod = _ada_mod(c, ada_w, ada_b).reshape(depth, batch, N_MOD, 1, d)
    rope = _rope_tables(positions)
    wbig, wsel = _split_w_in(w_in, d)
    cast = lambda w: w.astype(MXU_DTYPE)
    w1 = (cast(ffn1_w_gate), cast(ffn1_w_up), cast(ffn1_w_down))
    w2 = (cast(ffn2_w_gate), cast(ffn2_w_up), cast(ffn2_w_down))
    woa, wob, wout = cast(w_o_a), cast(w_o_b), cast(w_out)
    gains = jnp.stack([qn_a_g, kn_a_g, jnp.tile(qn_b_g, (1, LANES // HD_B)),
                       jnp.tile(kn_b_g, (1, LANES // HD_B))], axis=1)
    sink_rows = jnp.repeat(sinks, LANES, axis=1).reshape(depth, 1, H_B * LANES)

    h = x.reshape(n, d)
    for l in range(depth):
        m = [mod[l, :, k] for k in range(N_MOD)]
        h = _ffn(h, m[0], m[1], m[2], norm_ffn1_g[l][None], *w1, l, seq=seq, tm=tm_ffn, rows=tm, tf=tf, tn=tn)
        qa, qi, qb, gates, kv, ki, wi = _proj(h, m[3], m[4], norm_mix_g[l][None], wbig, wsel,
                                              gains[l], rope, l, seq=seq, tm=tm)
        ya, yb = _attn(qa, qi, qb, wi, kv, ki, sink_rows[l], batch=batch, seq=seq)
        h = _out_proj(h, ya, yb, gates, m[5], woa, wob, wout, l, seq=seq, tm=tm, tn=2 * tn)
        h = _ffn(h, m[6], m[7], m[8], norm_ffn2_g[l][None], *w2, l, seq=seq, tm=tm_ffn, rows=tm, tf=tf, tn=tn)
    return h.reshape(batch, seq, d)
```

```python
import functools

import jax
import jax.numpy as jnp
from jax import lax
from jax.experimental import pallas as pl
from jax.experimental.pallas import tpu as pltpu

H_A, HD_A = 8, 128
H_I, D_I, D_I_ROPE = 16, 64, 32
K_MAX = 256
H_B, HD_B, KVH_B = 16, 64, 2
BLOCK = 128
ROPE_THETA = 10000.0
EPS = 1e-6
NEG = -1e30
N_MOD = 9

LANES = 128
V7X_VMEM_LIMIT_BYTES = 60 * 1024 * 1024
MXU_DTYPE = jnp.bfloat16

INT_MIN = -2147483648
M_INIT = -1e29
LOG2E = 1.4426950408889634
SEL_COLS = 768
N_BIG = 3 * 1024

f32 = jnp.float32
i32 = jnp.int32


def _dot(a, b):
    return jnp.dot(a, b, preferred_element_type=f32)


def _params(*sem):
    return pltpu.CompilerParams(dimension_semantics=sem, vmem_limit_bytes=V7X_VMEM_LIMIT_BYTES)


def _ada_kernel(c_ref, w_ref, b_ref, o_ref):
    c = c_ref[...]
    act = (c * jax.nn.sigmoid(c)).astype(MXU_DTYPE)
    o_ref[0] = _dot(act, w_ref[0].astype(MXU_DTYPE)) + b_ref[0]


def _ada_mod(c, ada_w, ada_b):
    L, D, N = ada_w.shape
    B = c.shape[0]
    tn = min(N, 1024)
    return pl.pallas_call(
        _ada_kernel,
        grid=(L, N // tn),
        in_specs=[pl.BlockSpec((B, D), lambda l, j: (0, 0)),
                  pl.BlockSpec((1, D, tn), lambda l, j: (l, 0, j)),
                  pl.BlockSpec((1, 1, tn), lambda l, j: (l, 0, j))],
        out_specs=pl.BlockSpec((1, B, tn), lambda l, j: (l, 0, j)),
        out_shape=jax.ShapeDtypeStruct((L, B, N), f32),
        compiler_params=_params("arbitrary", "arbitrary"),
        name="ada_mod",
    )(c, ada_w, ada_b.reshape(L, 1, N))


def _rope_kernel(pos_ref, freq_ref, sign_ref, o_ref):
    pos = pos_ref[...].astype(f32)
    for k in range(3):
        ang = pos * freq_ref[k:k + 1, :]
        o_ref[:, (2 * k) * LANES:(2 * k + 1) * LANES] = jnp.cos(ang)
        o_ref[:, (2 * k + 1) * LANES:(2 * k + 2) * LANES] = jnp.sin(ang) * sign_ref[k:k + 1, :]


def _rope_rows():
    lane = jnp.arange(LANES)

    def inv_freq(dim):
        return ROPE_THETA ** (-jnp.arange(0, dim, 2, dtype=f32) / dim)

    fa = inv_freq(HD_A)[lane % (HD_A // 2)]
    sa = jnp.where(lane % HD_A < HD_A // 2, -1.0, 1.0)
    fb = inv_freq(HD_B)[lane % (HD_B // 2)]
    sb = jnp.where(lane % HD_B < HD_B // 2, -1.0, 1.0)
    in_rope = lane % D_I < D_I_ROPE
    fi = jnp.where(in_rope, inv_freq(D_I_ROPE)[lane % (D_I_ROPE // 2)], 0.0)
    si = jnp.where(lane % D_I < D_I_ROPE // 2, -1.0, 1.0)
    return jnp.stack([fa, fb, fi]).astype(f32), jnp.stack([sa, sb, si]).astype(f32)


def _rope_tables(positions):
    n = positions.size
    tm = min(n, 1024)
    freq, sign = _rope_rows()
    return pl.pallas_call(
        _rope_kernel,
        grid=(n // tm,),
        in_specs=[pl.BlockSpec((tm, 1), lambda i: (i, 0)),
                  pl.BlockSpec((3, LANES), lambda i: (0, 0)),
                  pl.BlockSpec((3, LANES), lambda i: (0, 0))],
        out_specs=pl.BlockSpec((tm, 6 * LANES), lambda i: (i, 0)),
        out_shape=jax.ShapeDtypeStruct((n, 6 * LANES), f32),
        compiler_params=_params("arbitrary"),
        name="rope_tables",
    )(positions.reshape(n, 1), freq, sign)


NORM_SLAB = 16


def _norm_modulate_to(u_ref, h_ref, g, shift, scale, copy_ref=None):
    geff = g * (1.0 + scale)
    for s in range(h_ref.shape[0] // NORM_SLAB):
        rs = slice(s * NORM_SLAB, (s + 1) * NORM_SLAB)
        x = h_ref[rs, :]
        ms = jnp.mean(x * x, axis=-1, keepdims=True)
        u_ref[rs, :] = (x * lax.rsqrt(ms + EPS) * geff + shift).astype(u_ref.dtype)
        if copy_ref is not None:
            copy_ref[rs, :] = x


def _head_norm(x, gain, lane, head_dim):
    sq = x * x
    if head_dim == LANES:
        ms = jnp.mean(sq, axis=-1, keepdims=True)
    else:
        lo = lane < head_dim
        s_lo = jnp.sum(jnp.where(lo, sq, 0.0), axis=-1, keepdims=True)
        s_hi = jnp.sum(jnp.where(lo, 0.0, sq), axis=-1, keepdims=True)
        ms = jnp.where(lo, s_lo, s_hi) * (1.0 / head_dim)
    return x * lax.rsqrt(ms + EPS) * gain


def _rotate(y, cos, sin, lane, half, period):
    if 2 * half == LANES:
        partner = pltpu.roll(y, half, 1)
    else:
        fwd = pltpu.roll(y, LANES - half, 1)
        bwd = pltpu.roll(y, half, 1)
        partner = jnp.where(lane % period < half, fwd, bwd)
    return y * cos + partner * sin


def _ffn_kernel(h_ref, sh_ref, sc_ref, gt_ref, g_ref, wg_ref, wu_ref, wd_ref, o_ref, u_ref, *, rows, tn):
    j = pl.program_id(1)

    @pl.when(j == 0)
    def _():
        _norm_modulate_to(u_ref, h_ref, g_ref[...], sh_ref[0], sc_ref[0], copy_ref=o_ref)

    half_gate = 0.5 * gt_ref[0]
    for r in range(u_ref.shape[0] // rows):
        rs = slice(r * rows, (r + 1) * rows)
        u = u_ref[rs, :]
        a = _dot(u, wg_ref[...])
        b = _dot(u, wu_ref[...])
        act = (a * jax.nn.sigmoid(a) * b).astype(MXU_DTYPE)
        for c in range(o_ref.shape[1] // tn):
            sl = slice(c * tn, (c + 1) * tn)
            o_ref[rs, sl] += half_gate[:, sl] * _dot(act, wd_ref[:, sl])


def _ffn(h, shift, scale, gate, g, wg, wu, wd, layer, *, seq, tm, rows, tf, tn):
    n, d = h.shape
    f = wg.shape[-1]
    per_b = seq // tm
    row = pl.BlockSpec((1, 1, d), lambda i, j: (i // per_b, 0, 0))
    return pl.pallas_call(
        functools.partial(_ffn_kernel, rows=rows, tn=tn),
        grid=(n // tm, f // tf),
        in_specs=[pl.BlockSpec((tm, d), lambda i, j: (i, 0)), row, row, row,
                  pl.BlockSpec((1, d), lambda i, j: (0, 0)),
                  pl.BlockSpec((None, d, tf), lambda i, j: (layer, 0, j)),
                  pl.BlockSpec((None, d, tf), lambda i, j: (layer, 0, j)),
                  pl.BlockSpec((None, tf, d), lambda i, j: (layer, j, 0))],
        out_specs=pl.BlockSpec((tm, d), lambda i, j: (i, 0)),
        out_shape=jax.ShapeDtypeStruct((n, d), f32),
        scratch_shapes=[pltpu.VMEM((tm, d), MXU_DTYPE)],
        compiler_params=_params("arbitrary", "arbitrary"),
        name="ffn",
    )(h, shift, scale, gate, g, wg, wu, wd)


def _proj_kernel(h_ref, sh_ref, sc_ref, g_ref, wbig_ref, wsel_ref, gains_ref, rope_ref,
                 qa_ref, qi_ref, qb_ref, ga_ref, gb_ref, kv_ref, ki_ref, wi_ref, u_ref, ra_ref, rb_ref):
    j = pl.program_id(1)
    tm = h_ref.shape[0]
    lane = lax.broadcasted_iota(i32, (tm, LANES), 1)

    def table(k):
        return rope_ref[:, k * LANES:(k + 1) * LANES]

    def gain(k):
        return gains_ref[k:k + 1, :]

    def rope_a(y):
        return _rotate(y, table(0), table(1), lane, HD_A // 2, HD_A)

    def rope_b(y):
        return _rotate(y, table(2), table(3), lane, HD_B // 2, HD_B)

    def rope_i(y):
        return _rotate(y, table(4), table(5), lane, D_I_ROPE // 2, D_I)

    def big_dot():
        return _dot(u_ref[...], wbig_ref[...])

    @pl.when(j == 0)
    def _():
        _norm_modulate_to(u_ref, h_ref, g_ref[...], sh_ref[0], sc_ref[0])
        sel = _dot(u_ref[...], wsel_ref[...])
        ra_ref[...] = big_dot()
        ka = rope_a(_head_norm(sel[:, 0:128], gain(1), lane, HD_A))
        kb = rope_b(_head_norm(sel[:, 256:384], gain(3), lane, HD_B))
        kv_ref[:, 0:128] = ka.astype(MXU_DTYPE)
        kv_ref[:, 128:256] = sel[:, 128:256].astype(MXU_DTYPE)
        kv_ref[:, 256:384] = kb.astype(MXU_DTYPE)
        kv_ref[:, 384:512] = sel[:, 384:512].astype(MXU_DTYPE)
        ki_ref[...] = rope_i(sel[:, 512:640]).astype(MXU_DTYPE)
        wi_ref[...] = sel[:, 640:768] * (H_I ** -0.5 * D_I ** -0.5)

    @pl.when(j == 1)
    def _():
        rb_ref[...] = big_dot()
        for k in range(8):
            y = _head_norm(ra_ref[:, k * LANES:(k + 1) * LANES], gain(0), lane, HD_A)
            qa_ref[:, k * LANES:(k + 1) * LANES] = rope_a(y).astype(MXU_DTYPE)

    @pl.when(j == 2)
    def _():
        ra_ref[...] = big_dot()
        for k in range(8):
            qi_ref[:, k * LANES:(k + 1) * LANES] = rope_i(rb_ref[:, k * LANES:(k + 1) * LANES]).astype(MXU_DTYPE)

    @pl.when(j == 3)
    def _():
        rb_ref[...] = big_dot()
        for k in range(8):
            y = _head_norm(ra_ref[:, k * LANES:(k + 1) * LANES], gain(2), lane, HD_B)
            qb_ref[:, k * LANES:(k + 1) * LANES] = (rope_b(y) * (HD_B ** -0.5)).astype(MXU_DTYPE)

    tn = ra_ref.shape[1]
    per_gate = ga_ref.shape[1] // tn

    def put_gate(c, res):
        ref = ga_ref if c < per_gate else gb_ref
        c = c % per_gate
        ref[:, c * tn:(c + 1) * tn] = jax.nn.sigmoid(res).astype(MXU_DTYPE)

    @pl.when(j == 4)
    def _():
        put_gate(1, big_dot())
        put_gate(0, rb_ref[...])

    for c in range(2, 2 * per_gate):
        @pl.when(j == 3 + c)
        def _(c=c):
            put_gate(c, big_dot())


def _proj(h, shift, scale, g, wbig, wsel, gains, rope, layer, *, seq, tm):
    n, d = h.shape
    nbig = wbig.shape[-1]
    d_gate = (nbig - N_BIG) // 2
    per_b = seq // tm
    tn = 1024
    nt = n // tm
    row = pl.BlockSpec((1, 1, d), lambda i, j: (i // per_b, 0, 0))
    tile = lambda w: pl.BlockSpec((tm, w), lambda i, j: (i, 0))

    def out_tile(w, done):
        return pl.BlockSpec((tm, w), lambda i, j: (jnp.minimum(i + (j > done).astype(jnp.int32), nt - 1), 0))

    bf = lambda w: jax.ShapeDtypeStruct((n, w), MXU_DTYPE)
    last = nbig // tn - 1
    return pl.pallas_call(
        _proj_kernel,
        grid=(nt, nbig // tn),
        in_specs=[tile(d), row, row,
                  pl.BlockSpec((1, d), lambda i, j: (0, 0)),
                  pl.BlockSpec((None, d, tn), lambda i, j: (layer, 0, j)),
                  pl.BlockSpec((None, d, SEL_COLS), lambda i, j: (layer, 0, 0)),
                  pl.BlockSpec((4, LANES), lambda i, j: (0, 0)),
                  tile(6 * LANES)],
        out_specs=[out_tile(1024, 1), out_tile(1024, 2), out_tile(1024, 3),
                   out_tile(d_gate, max(4, 2 + d_gate // tn)), out_tile(d_gate, last),
                   out_tile(512, 0), out_tile(LANES, 0), out_tile(LANES, 0)],
        out_shape=[bf(1024), bf(1024), bf(1024), bf(d_gate), bf(d_gate), bf(512), bf(LANES),
                   jax.ShapeDtypeStruct((n, LANES), f32)],
        scratch_shapes=[pltpu.VMEM((tm, d), MXU_DTYPE),
                        pltpu.VMEM((tm, tn), f32), pltpu.VMEM((tm, tn), f32)],
        compiler_params=_params("arbitrary", "arbitrary"),
        name="mix_proj",
    )(h, shift, scale, g, wbig, wsel, gains, rope)


QB = 256
KC = 256
I16_MIN = -32768
PACK = 16


def _attn_kernel(qa_ref, qi_ref, qb_ref, wi_ref, kv_ref, ki_ref, sink_ref,
                 ya_ref, yb_ref,
                 vat_ref, vbt_ref, qat_ref, qit_ref, key_ref, hi_ref, lo_ref, m_ref, l_ref, *head_refs,
                 n_sel):
    acc_refs, za_refs, zb_refs = head_refs[:H_A], head_refs[H_A:2 * H_A], head_refs[2 * H_A:]
    i = pl.program_id(1)
    nstep = pl.num_programs(1)
    nchunk = i + 1
    halves = (slice(0, BLOCK), slice(BLOCK, QB))

    def rows(c, size=KC):
        return pl.ds(pl.multiple_of(c * size, size), size)

    def t128(x):
        return x.astype(f32).T.astype(MXU_DTYPE)

    @pl.when(i == 0)
    def _():
        def body(c, carry):
            b0, b1 = rows(2 * c, BLOCK), rows(2 * c + 1, BLOCK)
            vat_ref[c] = jnp.concatenate([t128(kv_ref[b0, 128:256]), t128(kv_ref[b1, 128:256])], axis=1)
            vbt_ref[2 * c] = t128(kv_ref[b0, 384:512])
            vbt_ref[2 * c + 1] = t128(kv_ref[b1, 384:512])
            return carry
        lax.fori_loop(0, nstep, body, 0)

    for k in range(8):
        sl = slice(k * LANES, (k + 1) * LANES)
        for e, hs in enumerate(halves):
            qat_ref[:, k * QB + e * BLOCK:k * QB + (e + 1) * BLOCK] = t128(qa_ref[hs, sl])
            qit_ref[sl, hs] = t128(qi_ref[hs, sl])
    w_t = jnp.concatenate([wi_ref[hs, :].T for hs in halves], axis=1)

    q_lane = lax.broadcasted_iota(i32, (BLOCK, QB), 1)
    k_row = lax.broadcasted_iota(i32, (BLOCK, QB), 0)
    q_pos = i * QB + q_lane

    def idx_body(c, carry):
        for e in range(KC // BLOCK):
            cc = (KC // BLOCK) * c + e
            r = rows(cc, BLOCK)
            kc = ki_ref[r, 0:D_I]
            acc = jnp.zeros((BLOCK, QB), f32)
            for h in range(H_I):
                d = _dot(kc, qit_ref[h * D_I:(h + 1) * D_I, :])
                acc = acc + w_t[h:h + 1, :] * jnp.maximum(d, 0.0)
            bits = lax.bitcast_convert_type(acc + 0.0, i32)
            key = jnp.where(bits >= 0, bits, bits ^ 0x7FFFFFFF)
            key = jnp.where(cc * BLOCK + k_row <= q_pos, key, INT_MIN)
            key_ref[r, :] = key
            hi_ref[r, :] = (key >> 16).astype(jnp.int16)
            lo_ref[r, :] = ((key & 0xFFFF) + I16_MIN).astype(jnp.int16)
        return carry
    lax.fori_loop(0, nchunk, idx_body, 0)

    need = jnp.minimum(n_sel, q_pos[0:1, :] + 1).astype(f32)
    one, zero = jnp.ones((), jnp.bfloat16), jnp.zeros((), jnp.bfloat16)

    def count_ge(ref, cand):
        def body(c, cnt):
            ind = jnp.where(ref[rows(c), :] >= cand, one, zero)
            parts = [ind[g * PACK:(g + 1) * PACK] for g in range(KC // PACK)]
            while len(parts) > 1:
                parts = [a + b for a, b in zip(parts[0::2], parts[1::2])]
            return cnt + parts[0]
        cnt = lax.fori_loop(0, nchunk, body, jnp.zeros((PACK, QB), jnp.bfloat16))
        return jnp.sum(cnt.astype(f32), axis=0, keepdims=True)

    def bisect(ref, want):
        def body(k, t_u):
            cand_u = t_u | lax.shift_left(jnp.int32(1), 15 - k)
            ok = count_ge(ref, (cand_u + I16_MIN).astype(jnp.int16)) >= want
            return jnp.where(ok, cand_u, t_u)
        return lax.fori_loop(0, 16, body, jnp.zeros((1, QB), i32))

    hi_u = bisect(hi_ref, need)
    hi_t = (hi_u + I16_MIN).astype(jnp.int16)
    above = jnp.where(hi_u == 0xFFFF, 0.0,
                      count_ge(hi_ref, (jnp.minimum(hi_u + 1, 0xFFFF) + I16_MIN).astype(jnp.int16)))

    def tie_body(c, carry):
        lo_ref[rows(c), :] = jnp.where(hi_ref[rows(c), :] == hi_t, lo_ref[rows(c), :], jnp.int16(I16_MIN))
        return carry
    lax.fori_loop(0, nchunk, tie_body, 0)
    lo_u = bisect(lo_ref, need - above)
    thr = (hi_u + I16_MIN) * 65536 + lo_u

    picked = above + count_ge(lo_ref, (lo_u + I16_MIN).astype(jnp.int16))

    @pl.when(jnp.max(picked - need) > 0.0)
    def _():
        def gt_body(c, cnt):
            return cnt + jnp.sum(jnp.where(key_ref[rows(c), :] > thr, 1.0, 0.0), axis=0, keepdims=True)
        allowed = need - lax.fori_loop(0, nchunk, gt_body, jnp.zeros((1, QB), f32))
        tri = (lax.broadcasted_iota(i32, (KC, KC), 0) >= lax.broadcasted_iota(i32, (KC, KC), 1))
        tri = jnp.where(tri, 1.0, 0.0).astype(jnp.bfloat16)

        def fix_body(c, seen):
            k = key_ref[rows(c), :]
            tie = jnp.where(k == thr, 1.0, 0.0)
            rank = _dot(tri, tie.astype(jnp.bfloat16)) + seen
            key_ref[rows(c), :] = jnp.where((tie > 0.0) & (rank > allowed), thr - 1, k)
            return seen + jnp.sum(tie, axis=0, keepdims=True)
        lax.fori_loop(0, nchunk, fix_body, jnp.zeros((1, QB), f32))

    m_ref[...] = jnp.full(m_ref.shape, M_INIT, f32)
    l_ref[...] = jnp.zeros(l_ref.shape, f32)
    for acc_ref in acc_refs:
        acc_ref[...] = jnp.zeros(acc_ref.shape, f32)
    c_a = HD_A ** -0.5 * LOG2E

    def logits_of(c, h):
        return _dot(kv_ref[rows(c), 0:128], qat_ref[:, h * QB:(h + 1) * QB])

    def zero_after(x):
        u = lax.bitcast_convert_type(x, jnp.uint32)
        u = lax.shift_right_logical(lax.shift_right_logical(u, jnp.uint32(16)), jnp.uint32(16))
        return lax.bitcast_convert_type(u, f32)

    def chunk_step(c, z_in, z_out):
        nxt = jnp.minimum(c + 1, nchunk - 1)
        vt = vat_ref[c]
        bias = jnp.where(key_ref[rows(c), :] >= thr, 0.0, NEG)
        m_all, l_all = m_ref[...], l_ref[...]
        m_out, l_out = [], []
        gate = None
        for h in range(H_A):
            hs = slice(h * QB, (h + 1) * QB)
            z = z_in[h][...] + bias
            m_old = m_all[:, hs]
            m_new = jnp.maximum(m_old, jnp.max(z, axis=0, keepdims=True))
            if gate is not None:
                m_new = m_new + gate
            z_next = logits_of(nxt, h)
            z_out[h][...] = z_next
            gate = zero_after(z_next[0:1, :])
            alpha = jnp.exp2((m_old - m_new) * c_a)
            p = jnp.exp2((z - m_new) * c_a)
            l_out.append(alpha * l_all[:, hs] + jnp.sum(p, axis=0, keepdims=True))
            acc_refs[h][...] = alpha * acc_refs[h][...] + _dot(vt, p.astype(MXU_DTYPE))
            m_out.append(m_new)
        m_ref[...] = jnp.concatenate(m_out, axis=1)
        l_ref[...] = jnp.concatenate(l_out, axis=1)

    for h in range(H_A):
        za_refs[h][...] = logits_of(0, h)

    def att_body(k2, carry):
        chunk_step(2 * k2, za_refs, zb_refs)

        @pl.when(2 * k2 + 1 < nchunk)
        def _():
            chunk_step(2 * k2 + 1, zb_refs, za_refs)
        return carry
    lax.fori_loop(0, (nchunk + 1) // 2, att_body, 0)

    inv_l = 1.0 / l_ref[...]
    for k in range(H_A):
        ya_t = acc_refs[k][...] * inv_l[:, k * QB:(k + 1) * QB]
        for e, hs in enumerate(halves):
            ya_ref[hs, k * LANES:(k + 1) * LANES] = ya_t[:, e * BLOCK:(e + 1) * BLOCK].T.astype(MXU_DTYPE)

    b_lane = lax.broadcasted_iota(i32, (BLOCK, LANES), 1)
    b_row = lax.broadcasted_iota(i32, (BLOCK, LANES), 0)
    sink = sink_ref[...]
    zero_pad = jnp.zeros((HD_B, LANES), MXU_DTYPE)
    per_g = H_B // KVH_B
    for e, hs in enumerate(halves):
        blk = 2 * i + e
        ip = jnp.maximum(blk - 1, 0)
        kband = jnp.concatenate([kv_ref[rows(ip, BLOCK), 256:384], kv_ref[rows(blk, BLOCK), 256:384]], axis=0)
        ext = []
        for k in range(8):
            t = t128(qb_ref[hs, k * LANES:(k + 1) * LANES])
            for o in range(2):
                piece = t[o * HD_B:(o + 1) * HD_B, :]
                grp = (2 * k + o) // per_g
                ext.append(jnp.concatenate([piece, zero_pad] if grp == 0 else [zero_pad, piece], axis=0))
        s = _dot(kband, jnp.concatenate(ext, axis=1))
        prev_open = jnp.where(blk > 0, 0.0, NEG)
        bias = jnp.concatenate([jnp.where(b_row > b_lane, prev_open, NEG),
                                jnp.where(b_row <= b_lane, 0.0, NEG)], axis=0)
        bias2 = jnp.concatenate([bias, bias], axis=1)
        vbands = [jnp.concatenate([vbt_ref[ip][g * HD_B:(g + 1) * HD_B, :],
                                   vbt_ref[blk][g * HD_B:(g + 1) * HD_B, :]], axis=1) for g in range(KVH_B)]
        for k in range(8):
            cols = slice(2 * k * LANES, (2 * k + 2) * LANES)
            z = s[:, cols] + bias2
            m = jnp.maximum(jnp.max(z, axis=0, keepdims=True), sink[:, cols])
            ex = jnp.exp(z - m)
            inv = 1.0 / (jnp.sum(ex, axis=0, keepdims=True) + jnp.exp(sink[:, cols] - m))
            og = _dot(vbands[2 * k // per_g], ex.astype(MXU_DTYPE)) * inv
            yb_ref[hs, k * LANES:(k + 1) * LANES] = (
                jnp.concatenate([og[:, 0:LANES], og[:, LANES:2 * LANES]], axis=0).T.astype(MXU_DTYPE))


def _attn(qa, qi, qb, wi, kv, ki, sink_row, *, batch, seq):
    n = qa.shape[0]
    nstep = seq // QB
    n_sel = min(K_MAX, seq // 4)
    blk = lambda w: pl.BlockSpec((QB, w), lambda b, i: (b * nstep + i, 0))
    per_b = lambda w: pl.BlockSpec((seq, w), lambda b, i: (b, 0))
    return pl.pallas_call(
        functools.partial(_attn_kernel, n_sel=n_sel),
        grid=(batch, nstep),
        in_specs=[blk(1024), blk(1024), blk(1024), blk(LANES), per_b(512), per_b(LANES),
                  pl.BlockSpec((1, H_B * LANES), lambda b, i: (0, 0))],
        out_specs=[blk(1024), blk(1024)],
        out_shape=[jax.ShapeDtypeStruct((n, 1024), MXU_DTYPE)] * 2,
        scratch_shapes=[pltpu.VMEM((nstep, HD_A, KC), MXU_DTYPE),
                        pltpu.VMEM((seq // BLOCK, LANES, BLOCK), MXU_DTYPE),
                        pltpu.VMEM((HD_A, H_A * QB), MXU_DTYPE),
                        pltpu.VMEM((H_I * D_I, QB), MXU_DTYPE),
                        pltpu.VMEM((seq, QB), i32),
                        pltpu.VMEM((seq, QB), jnp.int16),
                        pltpu.VMEM((seq, QB), jnp.int16),
                        pltpu.VMEM((1, H_A * QB), f32),
                        pltpu.VMEM((1, H_A * QB), f32)]
                       + [pltpu.VMEM((HD_A, QB), f32)] * H_A
                       + [pltpu.VMEM((KC, QB), f32)] * (2 * H_A),

        compiler_params=_params("arbitrary", "arbitrary"),
        name="mixers",
    )(qa, qi, qb, wi, kv, ki, sink_row)


def _out_kernel(h_ref, ya_ref, yb_ref, ga_ref, gb_ref, gt_ref, woa_ref, wob_ref, wout_ref, o_ref, *, tn, tc):
    ya, yb, gate = ya_ref[...], yb_ref[...], gt_ref[0]
    d = o_ref.shape[1]
    for c in range(d // tn):
        sl = slice(c * tn, (c + 1) * tn)
        merged = (ga_ref[:, sl].astype(f32) * _dot(ya, woa_ref[:, sl])
                  + gb_ref[:, sl].astype(f32) * _dot(yb, wob_ref[:, sl])).astype(MXU_DTYPE)
        for e in range(d // tc):
            oc = slice(e * tc, (e + 1) * tc)
            part = gate[:, oc] * _dot(merged, wout_ref[sl, oc])
            if c == 0:
                o_ref[:, oc] = h_ref[:, oc] + part
            else:
                o_ref[:, oc] += part


def _out_proj(h, ya, yb, ga, gb, gate, woa, wob, wout, layer, *, seq, tm, tn, tc):
    n, d = h.shape
    k = ya.shape[1]
    per_b = seq // tm
    resident = lambda r, c: pl.BlockSpec((None, r, c), lambda i: (layer, 0, 0), pipeline_mode=pl.Buffered(1))
    return pl.pallas_call(
        functools.partial(_out_kernel, tn=tn, tc=tc),
        grid=(n // tm,),
        in_specs=[pl.BlockSpec((tm, d), lambda i: (i, 0)),
                  pl.BlockSpec((tm, k), lambda i: (i, 0)),
                  pl.BlockSpec((tm, k), lambda i: (i, 0)),
                  pl.BlockSpec((tm, d), lambda i: (i, 0)),
                  pl.BlockSpec((tm, d), lambda i: (i, 0)),
                  pl.BlockSpec((1, 1, d), lambda i: (i // per_b, 0, 0)),
                  resident(k, d), resident(k, d), resident(d, d)],
        out_specs=pl.BlockSpec((tm, d), lambda i: (i, 0)),
        out_shape=jax.ShapeDtypeStruct((n, d), f32),
        compiler_params=_params("arbitrary"),
        name="out_proj",
    )(h, ya, yb, ga, gb, gate, woa, wob, wout)


def _tiles(seq, d_ff):
    tm = min(seq, 512)
    tm_ffn = min(seq, 1024)
    tf = 512 if d_ff % 512 == 0 else 256
    return tm, tm_ffn, tf


def _split_w_in(w_in, d):
    sizes = (H_A * HD_A, HD_A, HD_A, H_I * D_I, D_I, H_I, H_B * HD_B, KVH_B * HD_B, KVH_B * HD_B, d, d)
    offs = [0]
    for s in sizes:
        offs.append(offs[-1] + s)
    seg = lambda k: w_in[:, :, offs[k]:offs[k + 1]]
    pad = lambda a, w: jnp.pad(a, ((0, 0), (0, 0), (0, w - a.shape[-1])))
    big = jnp.concatenate([seg(0), seg(3), seg(6), seg(9), seg(10)], axis=-1)
    sel = jnp.concatenate([seg(1), seg(2), seg(7), seg(8), pad(seg(4), LANES), pad(seg(5), LANES)], axis=-1)
    return big.astype(MXU_DTYPE), sel.astype(MXU_DTYPE)


def kernel(x, c, positions, ada_w, ada_b, norm_ffn1_g, ffn1_w_gate, ffn1_w_up, ffn1_w_down, norm_mix_g, w_in, qn_a_g, kn_a_g, qn_b_g, kn_b_g, sinks, w_o_a, w_o_b, w_out, norm_ffn2_g, ffn2_w_gate, ffn2_w_up, ffn2_w_down):
    batch, seq, d = x.shape
    depth = ada_w.shape[0]
    d_ff = ffn1_w_gate.shape[-1]
    n = batch * seq
    assert seq % QB == 0 and d % 1024 == 0
    tm, tm_ffn, tf = _tiles(seq, d_ff)
    tn = min(d, 512)

    mod = _ada_mod(c, ada_w, ada_b).reshape(depth, batch, N_MOD, 1, d)
    rope = _rope_tables(positions)
    wbig, wsel = _split_w_in(w_in, d)
    cast = lambda w: w.astype(MXU_DTYPE)
    w1 = (cast(ffn1_w_gate), cast(ffn1_w_up), cast(ffn1_w_down))
    w2 = (cast(ffn2_w_gate), cast(ffn2_w_up), cast(ffn2_w_down))
    woa, wob, wout = cast(w_o_a), cast(w_o_b), cast(w_out)
    gains = jnp.stack([qn_a_g, kn_a_g, jnp.tile(qn_b_g, (1, LANES // HD_B)),
                       jnp.tile(kn_b_g, (1, LANES // HD_B))], axis=1)
    sink_rows = jnp.repeat(sinks, LANES, axis=1).reshape(depth, 1, H_B * LANES)

    h = x.reshape(n, d)
    for l in range(depth):
        m = [mod[l, :, k] for k in range(N_MOD)]
        h = _ffn(h, m[0], m[1], m[2], norm_ffn1_g[l][None], *w1, l, seq=seq, tm=tm_ffn, rows=tm, tf=tf, tn=tn)
        qa, qi, qb, ga, gb, kv, ki, wi = _proj(h, m[3], m[4], norm_mix_g[l][None], wbig, wsel,
                                               gains[l], rope, l, seq=seq, tm=tm)
        ya, yb = _attn(qa, qi, qb, wi, kv, ki, sink_rows[l], batch=batch, seq=seq)
        h = _out_proj(h, ya, yb, ga, gb, m[5], woa, wob, wout, l, seq=seq, tm=tm, tn=2 * tn, tc=tn)
        h = _ffn(h, m[6], m[7], m[8], norm_ffn2_g[l][None], *w2, l, seq=seq, tm=tm_ffn, rows=tm, tf=tf, tn=tn)
    return h.reshape(batch, seq, d)
```

```python
import functools

import jax
import jax.numpy as jnp
from jax import lax
from jax.experimental import pallas as pl
from jax.experimental.pallas import tpu as pltpu

H_A, HD_A = 8, 128
H_I, D_I, D_I_ROPE = 16, 64, 32
K_MAX = 256
H_B, HD_B, KVH_B = 16, 64, 2
BLOCK = 128
ROPE_THETA = 10000.0
EPS = 1e-6
NEG = -1e30
N_MOD = 9

LANES = 128
V7X_VMEM_LIMIT_BYTES = 60 * 1024 * 1024
MXU_DTYPE = jnp.bfloat16

INT_MIN = -2147483648
M_INIT = -1e29
LOG2E = 1.4426950408889634
SEL_COLS = 768
N_BIG = 3 * 1024

f32 = jnp.float32
i32 = jnp.int32


def _dot(a, b):
    return jnp.dot(a, b, preferred_element_type=f32)


def _params(*sem):
    return pltpu.CompilerParams(dimension_semantics=sem, vmem_limit_bytes=V7X_VMEM_LIMIT_BYTES)


def _ada_kernel(c_ref, w_ref, b_ref, o_ref):
    c = c_ref[...]
    act = (c * jax.nn.sigmoid(c)).astype(MXU_DTYPE)
    o_ref[0] = _dot(act, w_ref[0].astype(MXU_DTYPE)) + b_ref[0]


def _ada_mod(c, ada_w, ada_b):
    L, D, N = ada_w.shape
    B = c.shape[0]
    tn = min(N, 1024)
    return pl.pallas_call(
        _ada_kernel,
        grid=(L, N // tn),
        in_specs=[pl.BlockSpec((B, D), lambda l, j: (0, 0)),
                  pl.BlockSpec((1, D, tn), lambda l, j: (l, 0, j)),
                  pl.BlockSpec((1, 1, tn), lambda l, j: (l, 0, j))],
        out_specs=pl.BlockSpec((1, B, tn), lambda l, j: (l, 0, j)),
        out_shape=jax.ShapeDtypeStruct((L, B, N), f32),
        compiler_params=_params("arbitrary", "arbitrary"),
        name="ada_mod",
    )(c, ada_w, ada_b.reshape(L, 1, N))


def _rope_kernel(pos_ref, freq_ref, sign_ref, o_ref):
    ang = pos_ref[...].astype(f32) * freq_ref[...]
    lane = lax.broadcasted_iota(i32, ang.shape, 1)

    def rep(v, n):
        return jnp.where(lane < n, v, pltpu.roll(v, n, 1))

    for t, (trig, fill) in enumerate(((jnp.cos(ang), 1.0), (jnp.sin(ang), 0.0))):
        a = rep(trig, HD_A // 2)
        b = rep(rep(pltpu.roll(trig, LANES - HD_A // 2, 1), HD_B // 2), HD_B)
        i_ = rep(pltpu.roll(trig, LANES - HD_A // 2 - HD_B // 2, 1), D_I_ROPE // 2)
        i_ = rep(jnp.where(lane < D_I_ROPE, i_, fill), D_I)
        for k, tab in enumerate((a, b, i_)):
            if t:
                tab = tab * sign_ref[k:k + 1, :]
            o_ref[:, (2 * k + t) * LANES:(2 * k + t + 1) * LANES] = tab


def _rope_rows():
    lane = jnp.arange(LANES)

    def inv_freq(dim):
        return ROPE_THETA ** (-jnp.arange(0, dim, 2, dtype=f32) / dim)

    freq = jnp.concatenate([inv_freq(HD_A), inv_freq(HD_B), inv_freq(D_I_ROPE),
                            jnp.zeros((LANES - (HD_A + HD_B + D_I_ROPE) // 2,), f32)])[None]
    sa = jnp.where(lane % HD_A < HD_A // 2, -1.0, 1.0)
    sb = jnp.where(lane % HD_B < HD_B // 2, -1.0, 1.0)
    si = jnp.where(lane % D_I < D_I_ROPE // 2, -1.0, 1.0)
    return freq.astype(f32), jnp.stack([sa, sb, si]).astype(f32)


def _rope_tables(positions):
    n = positions.size
    tm = min(n, 1024)
    freq, sign = _rope_rows()
    return pl.pallas_call(
        _rope_kernel,
        grid=(n // tm,),
        in_specs=[pl.BlockSpec((tm, 1), lambda i: (i, 0)),
                  pl.BlockSpec((1, LANES), lambda i: (0, 0)),
                  pl.BlockSpec((3, LANES), lambda i: (0, 0))],
        out_specs=pl.BlockSpec((tm, 6 * LANES), lambda i: (i, 0)),
        out_shape=jax.ShapeDtypeStruct((n, 6 * LANES), f32),
        compiler_params=_params("arbitrary"),
        name="rope_tables",
    )(positions.reshape(n, 1), freq, sign)


NORM_SLAB = 16


def _norm_modulate_to(u_ref, h_ref, g, shift, scale, copy_ref=None):
    geff = g * (1.0 + scale)
    for s in range(h_ref.shape[0] // NORM_SLAB):
        rs = slice(s * NORM_SLAB, (s + 1) * NORM_SLAB)
        x = h_ref[rs, :]
        ms = jnp.mean(x * x, axis=-1, keepdims=True)
        u_ref[rs, :] = (x * lax.rsqrt(ms + EPS) * geff + shift).astype(u_ref.dtype)
        if copy_ref is not None:
            copy_ref[rs, :] = x


def _head_norm(x, gain, lane, head_dim):
    sq = x * x
    if head_dim == LANES:
        ms = jnp.mean(sq, axis=-1, keepdims=True)
    else:
        lo = lane < head_dim
        s_lo = jnp.sum(jnp.where(lo, sq, 0.0), axis=-1, keepdims=True)
        s_hi = jnp.sum(jnp.where(lo, 0.0, sq), axis=-1, keepdims=True)
        ms = jnp.where(lo, s_lo, s_hi) * (1.0 / head_dim)
    return x * lax.rsqrt(ms + EPS) * gain


def _rotate(y, cos, sin, lane, half, period):
    if 2 * half == LANES:
        partner = pltpu.roll(y, half, 1)
    else:
        fwd = pltpu.roll(y, LANES - half, 1)
        bwd = pltpu.roll(y, half, 1)
        partner = jnp.where(lane % period < half, fwd, bwd)
    return y * cos + partner * sin


def _ffn_kernel(h_ref, sh_ref, sc_ref, gt_ref, g_ref, wg_ref, wu_ref, wd_ref, o_ref, u_ref, *, rows, tn):
    j = pl.program_id(1)

    @pl.when(j == 0)
    def _():
        _norm_modulate_to(u_ref, h_ref, g_ref[...], sh_ref[0], sc_ref[0], copy_ref=o_ref)

    half_gate = 0.5 * gt_ref[0]
    for r in range(u_ref.shape[0] // rows):
        rs = slice(r * rows, (r + 1) * rows)
        u = u_ref[rs, :]
        a = _dot(u, wg_ref[...])
        b = _dot(u, wu_ref[...])
        act = (a * jax.nn.sigmoid(a) * b).astype(MXU_DTYPE)
        for c in range(o_ref.shape[1] // tn):
            sl = slice(c * tn, (c + 1) * tn)
            o_ref[rs, sl] += half_gate[:, sl] * _dot(act, wd_ref[:, sl])


def _ffn(h, shift, scale, gate, g, wg, wu, wd, layer, *, seq, tm, rows, tf, tn):
    n, d = h.shape
    f = wg.shape[-1]
    per_b = seq // tm
    row = pl.BlockSpec((1, 1, d), lambda i, j: (i // per_b, 0, 0))
    return pl.pallas_call(
        functools.partial(_ffn_kernel, rows=rows, tn=tn),
        grid=(n // tm, f // tf),
        in_specs=[pl.BlockSpec((tm, d), lambda i, j: (i, 0)), row, row, row,
                  pl.BlockSpec((1, d), lambda i, j: (0, 0)),
                  pl.BlockSpec((None, d, tf), lambda i, j: (layer, 0, j)),
                  pl.BlockSpec((None, d, tf), lambda i, j: (layer, 0, j)),
                  pl.BlockSpec((None, tf, d), lambda i, j: (layer, j, 0))],
        out_specs=pl.BlockSpec((tm, d), lambda i, j: (i, 0)),
        out_shape=jax.ShapeDtypeStruct((n, d), f32),
        scratch_shapes=[pltpu.VMEM((tm, d), MXU_DTYPE)],
        compiler_params=_params("arbitrary", "arbitrary"),
        name="ffn",
    )(h, shift, scale, gate, g, wg, wu, wd)


def _proj_kernel(h_ref, sh_ref, sc_ref, g_ref, wq_ref, wgate_ref, wsel_ref, gains_ref, rope_ref,
                 qa_ref, qi_ref, qb_ref, ga_ref, gb_ref, kv_ref, ki_ref, wi_ref, u_ref, ra_ref, rb_ref):
    j = pl.program_id(1)
    tm = h_ref.shape[0]
    tn = ra_ref.shape[1]
    lane = lax.broadcasted_iota(i32, (tm, LANES), 1)

    def table(k):
        return rope_ref[:, k * LANES:(k + 1) * LANES]

    def gain(k):
        return gains_ref[k:k + 1, :]

    def rope_a(y):
        return _rotate(y, table(0), table(1), lane, HD_A // 2, HD_A)

    def rope_b(y):
        return _rotate(y, table(2), table(3), lane, HD_B // 2, HD_B)

    def rope_i(y):
        return _rotate(y, table(4), table(5), lane, D_I_ROPE // 2, D_I)

    def q_dot(c):
        return _dot(u_ref[...], wq_ref[:, c * tn:(c + 1) * tn])

    def gate_dot():
        return _dot(u_ref[...], wgate_ref[...])

    @pl.when(j == 0)
    def _():
        _norm_modulate_to(u_ref, h_ref, g_ref[...], sh_ref[0], sc_ref[0])
        sel = _dot(u_ref[...], wsel_ref[...])
        ra_ref[...] = q_dot(0)
        ka = rope_a(_head_norm(sel[:, 0:128], gain(1), lane, HD_A))
        kb = rope_b(_head_norm(sel[:, 256:384], gain(3), lane, HD_B))
        kv_ref[:, 0:128] = ka.astype(MXU_DTYPE)
        kv_ref[:, 128:256] = sel[:, 128:256].astype(MXU_DTYPE)
        kv_ref[:, 256:384] = kb.astype(MXU_DTYPE)
        kv_ref[:, 384:512] = sel[:, 384:512].astype(MXU_DTYPE)
        ki_ref[...] = rope_i(sel[:, 512:640]).astype(MXU_DTYPE)
        wi_ref[...] = sel[:, 640:768] * (H_I ** -0.5 * D_I ** -0.5)

    @pl.when(j == 1)
    def _():
        rb_ref[...] = q_dot(1)
        for k in range(8):
            y = _head_norm(ra_ref[:, k * LANES:(k + 1) * LANES], gain(0), lane, HD_A)
            qa_ref[:, k * LANES:(k + 1) * LANES] = rope_a(y).astype(MXU_DTYPE)

    @pl.when(j == 2)
    def _():
        ra_ref[...] = q_dot(2)
        for k in range(8):
            qi_ref[:, k * LANES:(k + 1) * LANES] = rope_i(rb_ref[:, k * LANES:(k + 1) * LANES]).astype(MXU_DTYPE)

    @pl.when(j == 3)
    def _():
        rb_ref[...] = gate_dot()
        for k in range(8):
            y = _head_norm(ra_ref[:, k * LANES:(k + 1) * LANES], gain(2), lane, HD_B)
            qb_ref[:, k * LANES:(k + 1) * LANES] = (rope_b(y) * (HD_B ** -0.5)).astype(MXU_DTYPE)

    per_gate = ga_ref.shape[1] // tn

    def put_gate(c, res):
        ref = ga_ref if c < per_gate else gb_ref
        c = c % per_gate
        ref[:, c * tn:(c + 1) * tn] = jax.nn.sigmoid(res).astype(MXU_DTYPE)

    @pl.when(j == 4)
    def _():
        put_gate(1, gate_dot())
        put_gate(0, rb_ref[...])

    for c in range(2, 2 * per_gate):
        @pl.when(j == 3 + c)
        def _(c=c):
            put_gate(c, gate_dot())


def _proj(h, shift, scale, g, wq, wgate, wsel, gains, rope, layer, *, seq, tm):
    n, d = h.shape
    d_gate = wgate.shape[-1] // 2
    per_b = seq // tm
    tn = 1024
    n_q = N_BIG // tn
    row = pl.BlockSpec((1, 1, d), lambda i, j: (i // per_b, 0, 0))
    tile = lambda w: pl.BlockSpec((tm, w), lambda i, j: (i, 0))
    resident = lambda w: pl.BlockSpec((None, d, w), lambda i, j: (layer, 0, 0), pipeline_mode=pl.Buffered(1))
    bf = lambda w: jax.ShapeDtypeStruct((n, w), MXU_DTYPE)
    return pl.pallas_call(
        _proj_kernel,
        grid=(n // tm, n_q + 2 * d_gate // tn),
        in_specs=[tile(d), row, row,
                  pl.BlockSpec((1, d), lambda i, j: (0, 0)),
                  resident(N_BIG),
                  pl.BlockSpec((None, d, tn), lambda i, j: (layer, 0, jnp.maximum(j - n_q, 0))),
                  resident(SEL_COLS),
                  pl.BlockSpec((4, LANES), lambda i, j: (0, 0)),
                  tile(6 * LANES)],
        out_specs=[tile(1024), tile(1024), tile(1024), tile(d_gate), tile(d_gate),
                   tile(512), tile(LANES), tile(LANES)],
        out_shape=[bf(1024), bf(1024), bf(1024), bf(d_gate), bf(d_gate), bf(512), bf(LANES),
                   jax.ShapeDtypeStruct((n, LANES), f32)],
        scratch_shapes=[pltpu.VMEM((tm, d), MXU_DTYPE),
                        pltpu.VMEM((tm, tn), f32), pltpu.VMEM((tm, tn), f32)],
        compiler_params=_params("arbitrary", "arbitrary"),
        name="mix_proj",
    )(h, shift, scale, g, wq, wgate, wsel, gains, rope)


QB = 256
KC = 256
I16_MIN = -32768
PACK = 16


def _attn_kernel(qa_ref, qi_ref, qb_ref, wi_ref, kv_ref, ki_ref, sink_ref,
                 ya_ref, yb_ref,
                 vat_ref, vbt_ref, qat_ref, qit_ref, key_ref, hi_ref, lo_ref, m_ref, l_ref, *head_refs,
                 n_sel):
    acc_refs, za_refs, zb_refs = head_refs[:H_A], head_refs[H_A:2 * H_A], head_refs[2 * H_A:]
    i = pl.program_id(1)
    nstep = pl.num_programs(1)
    nchunk = i + 1
    halves = (slice(0, BLOCK), slice(BLOCK, QB))

    def rows(c, size=KC):
        return pl.ds(pl.multiple_of(c * size, size), size)

    def t128(x):
        return x.astype(f32).T.astype(MXU_DTYPE)

    @pl.when(i == 0)
    def _():
        def body(c, carry):
            b0, b1 = rows(2 * c, BLOCK), rows(2 * c + 1, BLOCK)
            vat_ref[c] = jnp.concatenate([t128(kv_ref[b0, 128:256]), t128(kv_ref[b1, 128:256])], axis=1)
            vbt_ref[2 * c] = t128(kv_ref[b0, 384:512])
            vbt_ref[2 * c + 1] = t128(kv_ref[b1, 384:512])
            return carry
        lax.fori_loop(0, nstep, body, 0)

    for k in range(8):
        sl = slice(k * LANES, (k + 1) * LANES)
        for e, hs in enumerate(halves):
            qat_ref[:, k * QB + e * BLOCK:k * QB + (e + 1) * BLOCK] = t128(qa_ref[hs, sl])
            qit_ref[sl, hs] = t128(qi_ref[hs, sl])
    w_t = jnp.concatenate([wi_ref[hs, :].T for hs in halves], axis=1)

    q_lane = lax.broadcasted_iota(i32, (BLOCK, QB), 1)
    k_row = lax.broadcasted_iota(i32, (BLOCK, QB), 0)
    q_pos = i * QB + q_lane

    def idx_body(c, carry):
        for e in range(KC // BLOCK):
            cc = (KC // BLOCK) * c + e
            r = rows(cc, BLOCK)
            kc = ki_ref[r, 0:D_I]
            acc = jnp.zeros((BLOCK, QB), f32)
            for h in range(H_I):
                d = _dot(kc, qit_ref[h * D_I:(h + 1) * D_I, :])
                acc = acc + w_t[h:h + 1, :] * jnp.maximum(d, 0.0)
            bits = lax.bitcast_convert_type(acc + 0.0, i32)
            key = jnp.where(bits >= 0, bits, bits ^ 0x7FFFFFFF)
            key = jnp.where(cc * BLOCK + k_row <= q_pos, key, INT_MIN)
            key_ref[r, :] = key
            hi_ref[r, :] = (key >> 16).astype(jnp.int16)
            lo_ref[r, :] = ((key & 0xFFFF) + I16_MIN).astype(jnp.int16)
        return carry
    lax.fori_loop(0, nchunk, idx_body, 0)

    need = jnp.minimum(n_sel, q_pos[0:1, :] + 1).astype(f32)
    one, zero = jnp.ones((), jnp.bfloat16), jnp.zeros((), jnp.bfloat16)

    def count_ge(ref, cand):
        def body(c, cnt):
            ind = jnp.where(ref[rows(c), :] >= cand, one, zero)
            parts = [ind[g * PACK:(g + 1) * PACK] for g in range(KC // PACK)]
            while len(parts) > 1:
                parts = [a + b for a, b in zip(parts[0::2], parts[1::2])]
            return cnt + parts[0]
        cnt = lax.fori_loop(0, nchunk, body, jnp.zeros((PACK, QB), jnp.bfloat16))
        return jnp.sum(cnt.astype(f32), axis=0, keepdims=True)

    def bisect(ref, want):
        def body(k, t_u):
            cand_u = t_u | lax.shift_left(jnp.int32(1), 15 - k)
            ok = count_ge(ref, (cand_u + I16_MIN).astype(jnp.int16)) >= want
            return jnp.where(ok, cand_u, t_u)
        return lax.fori_loop(0, 16, body, jnp.zeros((1, QB), i32))

    hi_u = bisect(hi_ref, need)
    hi_t = (hi_u + I16_MIN).astype(jnp.int16)
    above = jnp.where(hi_u == 0xFFFF, 0.0,
                      count_ge(hi_ref, (jnp.minimum(hi_u + 1, 0xFFFF) + I16_MIN).astype(jnp.int16)))

    def tie_body(c, carry):
        lo_ref[rows(c), :] = jnp.where(hi_ref[rows(c), :] == hi_t, lo_ref[rows(c), :], jnp.int16(I16_MIN))
        return carry
    lax.fori_loop(0, nchunk, tie_body, 0)
    lo_u = bisect(lo_ref, need - above)
    thr = (hi_u + I16_MIN) * 65536 + lo_u

    picked = above + count_ge(lo_ref, (lo_u + I16_MIN).astype(jnp.int16))

    @pl.when(jnp.max(picked - need) > 0.0)
    def _():
        def gt_body(c, cnt):
            return cnt + jnp.sum(jnp.where(key_ref[rows(c), :] > thr, 1.0, 0.0), axis=0, keepdims=True)
        allowed = need - lax.fori_loop(0, nchunk, gt_body, jnp.zeros((1, QB), f32))
        tri = (lax.broadcasted_iota(i32, (KC, KC), 0) >= lax.broadcasted_iota(i32, (KC, KC), 1))
        tri = jnp.where(tri, 1.0, 0.0).astype(jnp.bfloat16)

        def fix_body(c, seen):
            k = key_ref[rows(c), :]
            tie = jnp.where(k == thr, 1.0, 0.0)
            rank = _dot(tri, tie.astype(jnp.bfloat16)) + seen
            key_ref[rows(c), :] = jnp.where((tie > 0.0) & (rank > allowed), thr - 1, k)
            return seen + jnp.sum(tie, axis=0, keepdims=True)
        lax.fori_loop(0, nchunk, fix_body, jnp.zeros((1, QB), f32))

    m_ref[...] = jnp.full(m_ref.shape, M_INIT, f32)
    l_ref[...] = jnp.zeros(l_ref.shape, f32)
    for acc_ref in acc_refs:
        acc_ref[...] = jnp.zeros(acc_ref.shape, f32)
    c_a = HD_A ** -0.5 * LOG2E

    def logits_of(c, h):
        return _dot(kv_ref[rows(c), 0:128], qat_ref[:, h * QB:(h + 1) * QB])

    def zero_after(x):
        u = lax.bitcast_convert_type(x, jnp.uint32)
        u = lax.shift_right_logical(lax.shift_right_logical(u, jnp.uint32(16)), jnp.uint32(16))
        return lax.bitcast_convert_type(u, f32)

    def chunk_step(c, z_in, z_out):
        nxt = jnp.minimum(c + 1, nchunk - 1)
        vt = vat_ref[c]
        bias = jnp.where(key_ref[rows(c), :] >= thr, 0.0, NEG)
        m_all, l_all = m_ref[...], l_ref[...]
        m_out, l_out = [], []
        gate = None
        for h in range(H_A):
            hs = slice(h * QB, (h + 1) * QB)
            z = z_in[h][...] + bias
            m_old = m_all[:, hs]
            m_new = jnp.maximum(m_old, jnp.max(z, axis=0, keepdims=True))
            if gate is not None:
                m_new = m_new + gate
            z_next = logits_of(nxt, h)
            z_out[h][...] = z_next
            gate = zero_after(z_next[0:1, :])
            alpha = jnp.exp2((m_old - m_new) * c_a)
            p = jnp.exp2((z - m_new) * c_a)
            l_out.append(alpha * l_all[:, hs] + jnp.sum(p, axis=0, keepdims=True))
            acc_refs[h][...] = alpha * acc_refs[h][...] + _dot(vt, p.astype(MXU_DTYPE))
            m_out.append(m_new)
        m_ref[...] = jnp.concatenate(m_out, axis=1)
        l_ref[...] = jnp.concatenate(l_out, axis=1)

    for h in range(H_A):
        za_refs[h][...] = logits_of(0, h)

    def att_body(k2, carry):
        chunk_step(2 * k2, za_refs, zb_refs)

        @pl.when(2 * k2 + 1 < nchunk)
        def _():
            chunk_step(2 * k2 + 1, zb_refs, za_refs)
        return carry
    lax.fori_loop(0, (nchunk + 1) // 2, att_body, 0)

    inv_l = 1.0 / l_ref[...]
    for k in range(H_A):
        ya_t = acc_refs[k][...] * inv_l[:, k * QB:(k + 1) * QB]
        for e, hs in enumerate(halves):
            ya_ref[hs, k * LANES:(k + 1) * LANES] = ya_t[:, e * BLOCK:(e + 1) * BLOCK].T.astype(MXU_DTYPE)

    b_lane = lax.broadcasted_iota(i32, (BLOCK, LANES), 1)
    b_row = lax.broadcasted_iota(i32, (BLOCK, LANES), 0)
    sink = sink_ref[...]
    zero_pad = jnp.zeros((HD_B, LANES), MXU_DTYPE)
    per_g = H_B // KVH_B
    for e, hs in enumerate(halves):
        blk = 2 * i + e
        ip = jnp.maximum(blk - 1, 0)
        kband = jnp.concatenate([kv_ref[rows(ip, BLOCK), 256:384], kv_ref[rows(blk, BLOCK), 256:384]], axis=0)
        ext = []
        for k in range(8):
            t = t128(qb_ref[hs, k * LANES:(k + 1) * LANES])
            for o in range(2):
                piece = t[o * HD_B:(o + 1) * HD_B, :]
                grp = (2 * k + o) // per_g
                ext.append(jnp.concatenate([piece, zero_pad] if grp == 0 else [zero_pad, piece], axis=0))
        s = _dot(kband, jnp.concatenate(ext, axis=1))
        prev_open = jnp.where(blk > 0, 0.0, NEG)
        bias = jnp.concatenate([jnp.where(b_row > b_lane, prev_open, NEG),
                                jnp.where(b_row <= b_lane, 0.0, NEG)], axis=0)
        bias2 = jnp.concatenate([bias, bias], axis=1)
        vbands = [jnp.concatenate([vbt_ref[ip][g * HD_B:(g + 1) * HD_B, :],
                                   vbt_ref[blk][g * HD_B:(g + 1) * HD_B, :]], axis=1) for g in range(KVH_B)]
        for k in range(8):
            cols = slice(2 * k * LANES, (2 * k + 2) * LANES)
            z = s[:, cols] + bias2
            m = jnp.maximum(jnp.max(z, axis=0, keepdims=True), sink[:, cols])
            ex = jnp.exp(z - m)
            inv = 1.0 / (jnp.sum(ex, axis=0, keepdims=True) + jnp.exp(sink[:, cols] - m))
            og = _dot(vbands[2 * k // per_g], ex.astype(MXU_DTYPE)) * inv
            yb_ref[hs, k * LANES:(k + 1) * LANES] = (
                jnp.concatenate([og[:, 0:LANES], og[:, LANES:2 * LANES]], axis=0).T.astype(MXU_DTYPE))


def _attn(qa, qi, qb, wi, kv, ki, sink_row, *, batch, seq):
    n = qa.shape[0]
    nstep = seq // QB
    n_sel = min(K_MAX, seq // 4)
    blk = lambda w: pl.BlockSpec((QB, w), lambda b, i: (b * nstep + i, 0))
    per_b = lambda w: pl.BlockSpec((seq, w), lambda b, i: (b, 0))
    return pl.pallas_call(
        functools.partial(_attn_kernel, n_sel=n_sel),
        grid=(batch, nstep),
        in_specs=[blk(1024), blk(1024), blk(1024), blk(LANES), per_b(512), per_b(LANES),
                  pl.BlockSpec((1, H_B * LANES), lambda b, i: (0, 0))],
        out_specs=[blk(1024), blk(1024)],
        out_shape=[jax.ShapeDtypeStruct((n, 1024), MXU_DTYPE)] * 2,
        scratch_shapes=[pltpu.VMEM((nstep, HD_A, KC), MXU_DTYPE),
                        pltpu.VMEM((seq // BLOCK, LANES, BLOCK), MXU_DTYPE),
                        pltpu.VMEM((HD_A, H_A * QB), MXU_DTYPE),
                        pltpu.VMEM((H_I * D_I, QB), MXU_DTYPE),
                        pltpu.VMEM((seq, QB), i32),
                        pltpu.VMEM((seq, QB), jnp.int16),
                        pltpu.VMEM((seq, QB), jnp.int16),
                        pltpu.VMEM((1, H_A * QB), f32),
                        pltpu.VMEM((1, H_A * QB), f32)]
                       + [pltpu.VMEM((HD_A, QB), f32)] * H_A
                       + [pltpu.VMEM((KC, QB), f32)] * (2 * H_A),

        compiler_params=_params("arbitrary", "arbitrary"),
        name="mixers",
    )(qa, qi, qb, wi, kv, ki, sink_row)


def _out_kernel(h_ref, ya_ref, yb_ref, ga_ref, gb_ref, gt_ref, woa_ref, wob_ref, wout_ref, o_ref, *, tn, tc):
    ya, yb, gate = ya_ref[...], yb_ref[...], gt_ref[0]
    d = o_ref.shape[1]
    for c in range(d // tn):
        sl = slice(c * tn, (c + 1) * tn)
        merged = (ga_ref[:, sl].astype(f32) * _dot(ya, woa_ref[:, sl])
                  + gb_ref[:, sl].astype(f32) * _dot(yb, wob_ref[:, sl])).astype(MXU_DTYPE)
        for e in range(d // tc):
            oc = slice(e * tc, (e + 1) * tc)
            part = gate[:, oc] * _dot(merged, wout_ref[sl, oc])
            if c == 0:
                o_ref[:, oc] = h_ref[:, oc] + part
            else:
                o_ref[:, oc] += part


def _out_proj(h, ya, yb, ga, gb, gate, woa, wob, wout, layer, *, seq, tm, tn, tc):
    n, d = h.shape
    k = ya.shape[1]
    per_b = seq // tm
    resident = lambda r, c: pl.BlockSpec((None, r, c), lambda i: (layer, 0, 0), pipeline_mode=pl.Buffered(1))
    return pl.pallas_call(
        functools.partial(_out_kernel, tn=tn, tc=tc),
        grid=(n // tm,),
        in_specs=[pl.BlockSpec((tm, d), lambda i: (i, 0)),
                  pl.BlockSpec((tm, k), lambda i: (i, 0)),
                  pl.BlockSpec((tm, k), lambda i: (i, 0)),
                  pl.BlockSpec((tm, d), lambda i: (i, 0)),
                  pl.BlockSpec((tm, d), lambda i: (i, 0)),
                  pl.BlockSpec((1, 1, d), lambda i: (i // per_b, 0, 0)),
                  resident(k, d), resident(k, d), resident(d, d)],
        out_specs=pl.BlockSpec((tm, d), lambda i: (i, 0)),
        out_shape=jax.ShapeDtypeStruct((n, d), f32),
        compiler_params=_params("arbitrary"),
        name="out_proj",
    )(h, ya, yb, ga, gb, gate, woa, wob, wout)


def _tiles(seq, d_ff):
    tm = min(seq, 512)
    tm_ffn = min(seq, 1024)
    tf = 512 if d_ff % 512 == 0 else 256
    return tm, tm_ffn, tf


def _split_w_in(w_in, d):
    sizes = (H_A * HD_A, HD_A, HD_A, H_I * D_I, D_I, H_I, H_B * HD_B, KVH_B * HD_B, KVH_B * HD_B, d, d)
    offs = [0]
    for s in sizes:
        offs.append(offs[-1] + s)
    seg = lambda k: w_in[:, :, offs[k]:offs[k + 1]]
    pad = lambda a, w: jnp.pad(a, ((0, 0), (0, 0), (0, w - a.shape[-1])))
    q = jnp.concatenate([seg(0), seg(3), seg(6)], axis=-1)
    gates = jnp.concatenate([seg(9), seg(10)], axis=-1)
    sel = jnp.concatenate([seg(1), seg(2), seg(7), seg(8), pad(seg(4), LANES), pad(seg(5), LANES)], axis=-1)
    return q.astype(MXU_DTYPE), gates.astype(MXU_DTYPE), sel.astype(MXU_DTYPE)


def kernel(x, c, positions, ada_w, ada_b, norm_ffn1_g, ffn1_w_gate, ffn1_w_up, ffn1_w_down, norm_mix_g, w_in, qn_a_g, kn_a_g, qn_b_g, kn_b_g, sinks, w_o_a, w_o_b, w_out, norm_ffn2_g, ffn2_w_gate, ffn2_w_up, ffn2_w_down):
    batch, seq, d = x.shape
    depth = ada_w.shape[0]
    d_ff = ffn1_w_gate.shape[-1]
    n = batch * seq
    assert seq % QB == 0 and d % 1024 == 0
    tm, tm_ffn, tf = _tiles(seq, d_ff)
    tn = min(d, 512)

    mod = _ada_mod(c, ada_w, ada_b).reshape(depth, batch, N_MOD, 1, d)
    rope = _rope_tables(positions)
    wq, wgate, wsel = _split_w_in(w_in, d)
    cast = lambda w: w.astype(MXU_DTYPE)
    w1 = (cast(ffn1_w_gate), cast(ffn1_w_up), cast(ffn1_w_down))
    w2 = (cast(ffn2_w_gate), cast(ffn2_w_up), cast(ffn2_w_down))
    woa, wob, wout = cast(w_o_a), cast(w_o_b), cast(w_out)
    gains = jnp.stack([qn_a_g, kn_a_g, jnp.tile(qn_b_g, (1, LANES // HD_B)),
                       jnp.tile(kn_b_g, (1, LANES // HD_B))], axis=1)
    sink_rows = jnp.repeat(sinks, LANES, axis=1).reshape(depth, 1, H_B * LANES)

    h = x.reshape(n, d)
    for l in range(depth):
        m = [mod[l, :, k] for k in range(N_MOD)]
        h = _ffn(h, m[0], m[1], m[2], norm_ffn1_g[l][None], *w1, l, seq=seq, tm=tm_ffn, rows=tm, tf=tf, tn=tn)
        qa, qi, qb, ga, gb, kv, ki, wi = _proj(h, m[3], m[4], norm_mix_g[l][None], wq, wgate, wsel,
                                               gains[l], rope, l, seq=seq, tm=tm)
        ya, yb = _attn(qa, qi, qb, wi, kv, ki, sink_rows[l], batch=batch, seq=seq)
        h = _out_proj(h, ya, yb, ga, gb, m[5], woa, wob, wout, l, seq=seq, tm=tm, tn=2 * tn, tc=tn)
        h = _ffn(h, m[6], m[7], m[8], norm_ffn2_g[l][None], *w2, l, seq=seq, tm=tm_ffn, rows=tm, tf=tf, tn=tn)
    return h.reshape(batch, seq, d)
```

```python
import functools

import jax
import jax.numpy as jnp
from jax import lax
from jax.experimental import pallas as pl
from jax.experimental.pallas import tpu as pltpu

H_A, HD_A = 8, 128
H_I, D_I, D_I_ROPE = 16, 64, 32
K_MAX = 256
H_B, HD_B, KVH_B = 16, 64, 2
BLOCK = 128
ROPE_THETA = 10000.0
EPS = 1e-6
NEG = -1e30
N_MOD = 9

LANES = 128
V7X_VMEM_LIMIT_BYTES = 60 * 1024 * 1024
MXU_DTYPE = jnp.bfloat16

INT_MIN = -2147483648
M_INIT = -1e29
LOG2E = 1.4426950408889634
SEL_COLS = 768
N_BIG = 3 * 1024

f32 = jnp.float32
i32 = jnp.int32


def _sigmoid(x):
    return 0.5 * jnp.tanh(0.5 * x) + 0.5


def _dot(a, b):
    return jnp.dot(a, b, preferred_element_type=f32)


def _params(*sem):
    return pltpu.CompilerParams(dimension_semantics=sem, vmem_limit_bytes=V7X_VMEM_LIMIT_BYTES)


def _ada_kernel(c_ref, w_ref, b_ref, o_ref):
    c = c_ref[...]
    act = (c * jax.nn.sigmoid(c)).astype(MXU_DTYPE)
    o_ref[0] = _dot(act, w_ref[0].astype(MXU_DTYPE)) + b_ref[0]


def _ada_mod(c, ada_w, ada_b):
    L, D, N = ada_w.shape
    B = c.shape[0]
    tn = min(N, 1024)
    return pl.pallas_call(
        _ada_kernel,
        grid=(L, N // tn),
        in_specs=[pl.BlockSpec((B, D), lambda l, j: (0, 0)),
                  pl.BlockSpec((1, D, tn), lambda l, j: (l, 0, j)),
                  pl.BlockSpec((1, 1, tn), lambda l, j: (l, 0, j))],
        out_specs=pl.BlockSpec((1, B, tn), lambda l, j: (l, 0, j)),
        out_shape=jax.ShapeDtypeStruct((L, B, N), f32),
        compiler_params=_params("arbitrary", "arbitrary"),
        name="ada_mod",
    )(c, ada_w, ada_b.reshape(L, 1, N))


def _rope_kernel(pos_ref, freq_ref, sign_ref, o_ref):
    ang = pos_ref[...].astype(f32) * freq_ref[...]
    lane = lax.broadcasted_iota(i32, ang.shape, 1)

    def rep(v, n):
        return jnp.where(lane < n, v, pltpu.roll(v, n, 1))

    for t, (trig, fill) in enumerate(((jnp.cos(ang), 1.0), (jnp.sin(ang), 0.0))):
        a = rep(trig, HD_A // 2)
        b = rep(rep(pltpu.roll(trig, LANES - HD_A // 2, 1), HD_B // 2), HD_B)
        i_ = rep(pltpu.roll(trig, LANES - HD_A // 2 - HD_B // 2, 1), D_I_ROPE // 2)
        i_ = rep(jnp.where(lane < D_I_ROPE, i_, fill), D_I)
        for k, tab in enumerate((a, b, i_)):
            if t:
                tab = tab * sign_ref[k:k + 1, :]
            o_ref[:, (2 * k + t) * LANES:(2 * k + t + 1) * LANES] = tab


def _rope_rows():
    lane = jnp.arange(LANES)

    def inv_freq(dim):
        return ROPE_THETA ** (-jnp.arange(0, dim, 2, dtype=f32) / dim)

    freq = jnp.concatenate([inv_freq(HD_A), inv_freq(HD_B), inv_freq(D_I_ROPE),
                            jnp.zeros((LANES - (HD_A + HD_B + D_I_ROPE) // 2,), f32)])[None]
    sa = jnp.where(lane % HD_A < HD_A // 2, -1.0, 1.0)
    sb = jnp.where(lane % HD_B < HD_B // 2, -1.0, 1.0)
    si = jnp.where(lane % D_I < D_I_ROPE // 2, -1.0, 1.0)
    return freq.astype(f32), jnp.stack([sa, sb, si]).astype(f32)


def _rope_tables(positions):
    n = positions.size
    tm = min(n, 1024)
    freq, sign = _rope_rows()
    return pl.pallas_call(
        _rope_kernel,
        grid=(n // tm,),
        in_specs=[pl.BlockSpec((tm, 1), lambda i: (i, 0)),
                  pl.BlockSpec((1, LANES), lambda i: (0, 0)),
                  pl.BlockSpec((3, LANES), lambda i: (0, 0))],
        out_specs=pl.BlockSpec((tm, 6 * LANES), lambda i: (i, 0)),
        out_shape=jax.ShapeDtypeStruct((n, 6 * LANES), f32),
        compiler_params=_params("arbitrary"),
        name="rope_tables",
    )(positions.reshape(n, 1), freq, sign)


NORM_SLAB = 16


def _norm_modulate_to(u_ref, h_ref, g, shift, scale, copy_ref=None):
    geff = g * (1.0 + scale)
    for s in range(h_ref.shape[0] // NORM_SLAB):
        rs = slice(s * NORM_SLAB, (s + 1) * NORM_SLAB)
        x = h_ref[rs, :]
        ms = jnp.mean(x * x, axis=-1, keepdims=True)
        u_ref[rs, :] = (x * lax.rsqrt(ms + EPS) * geff + shift).astype(u_ref.dtype)
        if copy_ref is not None:
            copy_ref[rs, :] = x


def _head_norm(x, gain, lane, head_dim):
    sq = x * x
    if head_dim == LANES:
        ms = jnp.mean(sq, axis=-1, keepdims=True)
    else:
        lo = lane < head_dim
        s_lo = jnp.sum(jnp.where(lo, sq, 0.0), axis=-1, keepdims=True)
        s_hi = jnp.sum(jnp.where(lo, 0.0, sq), axis=-1, keepdims=True)
        ms = jnp.where(lo, s_lo, s_hi) * (1.0 / head_dim)
    return x * lax.rsqrt(ms + EPS) * gain


def _rotate(y, cos, sin, lane, half, period):
    if 2 * half == LANES:
        partner = pltpu.roll(y, half, 1)
    else:
        fwd = pltpu.roll(y, LANES - half, 1)
        bwd = pltpu.roll(y, half, 1)
        partner = jnp.where(lane % period < half, fwd, bwd)
    return y * cos + partner * sin


def _ffn_kernel(h_ref, sh_ref, sc_ref, gt_ref, g_ref, wg_ref, wu_ref, wd_ref, o_ref, u_ref, *, rows, tn):
    j = pl.program_id(1)

    @pl.when(j == 0)
    def _():
        _norm_modulate_to(u_ref, h_ref, g_ref[...], sh_ref[0], sc_ref[0], copy_ref=o_ref)

    half_gate = 0.5 * gt_ref[0]
    for r in range(u_ref.shape[0] // rows):
        rs = slice(r * rows, (r + 1) * rows)
        u = u_ref[rs, :]
        a = _dot(u, wg_ref[...])
        b = _dot(u, wu_ref[...])
        act = (a * _sigmoid(a) * b).astype(MXU_DTYPE)
        for c in range(o_ref.shape[1] // tn):
            sl = slice(c * tn, (c + 1) * tn)
            o_ref[rs, sl] += half_gate[:, sl] * _dot(act, wd_ref[:, sl])


def _ffn(h, shift, scale, gate, g, wg, wu, wd, layer, *, seq, tm, rows, tf, tn):
    n, d = h.shape
    f = wg.shape[-1]
    per_b = seq // tm
    row = pl.BlockSpec((1, 1, d), lambda i, j: (i // per_b, 0, 0))
    return pl.pallas_call(
        functools.partial(_ffn_kernel, rows=rows, tn=tn),
        grid=(n // tm, f // tf),
        in_specs=[pl.BlockSpec((tm, d), lambda i, j: (i, 0)), row, row, row,
                  pl.BlockSpec((1, d), lambda i, j: (0, 0)),
                  pl.BlockSpec((None, d, tf), lambda i, j: (layer, 0, j)),
                  pl.BlockSpec((None, d, tf), lambda i, j: (layer, 0, j)),
                  pl.BlockSpec((None, tf, d), lambda i, j: (layer, j, 0))],
        out_specs=pl.BlockSpec((tm, d), lambda i, j: (i, 0)),
        out_shape=jax.ShapeDtypeStruct((n, d), f32),
        scratch_shapes=[pltpu.VMEM((tm, d), MXU_DTYPE)],
        compiler_params=_params("arbitrary", "arbitrary"),
        name="ffn",
    )(h, shift, scale, gate, g, wg, wu, wd)


def _proj_kernel(h_ref, sh_ref, sc_ref, g_ref, wq_ref, wgate_ref, wsel_ref, gains_ref, rope_ref,
                 qa_ref, qi_ref, qb_ref, ga_ref, gb_ref, kv_ref, ki_ref, wi_ref, u_ref, ra_ref, rb_ref):
    j = pl.program_id(1)
    tm = h_ref.shape[0]
    tn = ra_ref.shape[1]
    lane = lax.broadcasted_iota(i32, (tm, LANES), 1)

    def table(k):
        return rope_ref[:, k * LANES:(k + 1) * LANES]

    def gain(k):
        return gains_ref[k:k + 1, :]

    def rope_a(y):
        return _rotate(y, table(0), table(1), lane, HD_A // 2, HD_A)

    def rope_b(y):
        return _rotate(y, table(2), table(3), lane, HD_B // 2, HD_B)

    def rope_i(y):
        return _rotate(y, table(4), table(5), lane, D_I_ROPE // 2, D_I)

    def q_dot(c):
        return _dot(u_ref[...], wq_ref[:, c * tn:(c + 1) * tn])

    def gate_dot():
        return _dot(u_ref[...], wgate_ref[...])

    @pl.when(j == 0)
    def _():
        _norm_modulate_to(u_ref, h_ref, g_ref[...], sh_ref[0], sc_ref[0])
        sel = _dot(u_ref[...], wsel_ref[...])
        ra_ref[...] = q_dot(0)
        ka = rope_a(_head_norm(sel[:, 0:128], gain(1), lane, HD_A))
        kb = rope_b(_head_norm(sel[:, 256:384], gain(3), lane, HD_B))
        kv_ref[:, 0:128] = ka.astype(MXU_DTYPE)
        kv_ref[:, 128:256] = sel[:, 128:256].astype(MXU_DTYPE)
        kv_ref[:, 256:384] = kb.astype(MXU_DTYPE)
        kv_ref[:, 384:512] = sel[:, 384:512].astype(MXU_DTYPE)
        ki_ref[...] = rope_i(sel[:, 512:640]).astype(MXU_DTYPE)
        wi_ref[...] = sel[:, 640:768] * (H_I ** -0.5 * D_I ** -0.5)

    @pl.when(j == 1)
    def _():
        rb_ref[...] = q_dot(1)
        for k in range(8):
            y = _head_norm(ra_ref[:, k * LANES:(k + 1) * LANES], gain(0), lane, HD_A)
            qa_ref[:, k * LANES:(k + 1) * LANES] = rope_a(y).astype(MXU_DTYPE)

    @pl.when(j == 2)
    def _():
        ra_ref[...] = q_dot(2)
        for k in range(8):
            qi_ref[:, k * LANES:(k + 1) * LANES] = rope_i(rb_ref[:, k * LANES:(k + 1) * LANES]).astype(MXU_DTYPE)

    @pl.when(j == 3)
    def _():
        rb_ref[...] = gate_dot()
        for k in range(8):
            y = _head_norm(ra_ref[:, k * LANES:(k + 1) * LANES], gain(2), lane, HD_B)
            qb_ref[:, k * LANES:(k + 1) * LANES] = (rope_b(y) * (HD_B ** -0.5)).astype(MXU_DTYPE)

    per_gate = ga_ref.shape[1] // tn

    def put_gate(c, res):
        ref = ga_ref if c < per_gate else gb_ref
        c = c % per_gate
        ref[:, c * tn:(c + 1) * tn] = _sigmoid(res).astype(MXU_DTYPE)

    @pl.when(j == 4)
    def _():
        put_gate(1, gate_dot())
        put_gate(0, rb_ref[...])

    for c in range(2, 2 * per_gate):
        @pl.when(j == 3 + c)
        def _(c=c):
            put_gate(c, gate_dot())


def _proj(h, shift, scale, g, wq, wgate, wsel, gains, rope, layer, *, seq, tm):
    n, d = h.shape
    d_gate = wgate.shape[-1] // 2
    per_b = seq // tm
    tn = 1024
    n_q = N_BIG // tn
    row = pl.BlockSpec((1, 1, d), lambda i, j: (i // per_b, 0, 0))
    tile = lambda w: pl.BlockSpec((tm, w), lambda i, j: (i, 0))
    resident = lambda w: pl.BlockSpec((None, d, w), lambda i, j: (layer, 0, 0), pipeline_mode=pl.Buffered(1))
    bf = lambda w: jax.ShapeDtypeStruct((n, w), MXU_DTYPE)
    return pl.pallas_call(
        _proj_kernel,
        grid=(n // tm, n_q + 2 * d_gate // tn),
        in_specs=[tile(d), row, row,
                  pl.BlockSpec((1, d), lambda i, j: (0, 0)),
                  resident(N_BIG),
                  pl.BlockSpec((None, d, tn), lambda i, j: (layer, 0, jnp.maximum(j - n_q, 0))),
                  resident(SEL_COLS),
                  pl.BlockSpec((4, LANES), lambda i, j: (0, 0)),
                  tile(6 * LANES)],
        out_specs=[tile(1024), tile(1024), tile(1024), tile(d_gate), tile(d_gate),
                   tile(512), tile(LANES), tile(LANES)],
        out_shape=[bf(1024), bf(1024), bf(1024), bf(d_gate), bf(d_gate), bf(512), bf(LANES),
                   jax.ShapeDtypeStruct((n, LANES), f32)],
        scratch_shapes=[pltpu.VMEM((tm, d), MXU_DTYPE),
                        pltpu.VMEM((tm, tn), f32), pltpu.VMEM((tm, tn), f32)],
        compiler_params=_params("arbitrary", "arbitrary"),
        name="mix_proj",
    )(h, shift, scale, g, wq, wgate, wsel, gains, rope)


QB = 256
KC = 256
I16_MIN = -32768
PACK = 16


def _attn_kernel(qa_ref, qi_ref, qb_ref, wi_ref, kv_ref, ki_ref, sink_ref,
                 ya_ref, yb_ref,
                 vat_ref, vbt_ref, qat_ref, qit_ref, key_ref, hi_ref, lo_ref, m_ref, l_ref, *head_refs,
                 n_sel):
    acc_refs, za_refs, zb_refs = head_refs[:H_A], head_refs[H_A:2 * H_A], head_refs[2 * H_A:]
    i = pl.program_id(1)
    nstep = pl.num_programs(1)
    nchunk = i + 1
    halves = (slice(0, BLOCK), slice(BLOCK, QB))

    def rows(c, size=KC):
        return pl.ds(pl.multiple_of(c * size, size), size)

    def t128(x):
        return x.astype(f32).T.astype(MXU_DTYPE)

    @pl.when(i == 0)
    def _():
        def body(c, carry):
            b0, b1 = rows(2 * c, BLOCK), rows(2 * c + 1, BLOCK)
            vat_ref[c] = jnp.concatenate([t128(kv_ref[b0, 128:256]), t128(kv_ref[b1, 128:256])], axis=1)
            vbt_ref[2 * c] = t128(kv_ref[b0, 384:512])
            vbt_ref[2 * c + 1] = t128(kv_ref[b1, 384:512])
            return carry
        lax.fori_loop(0, nstep, body, 0)

    for k in range(8):
        sl = slice(k * LANES, (k + 1) * LANES)
        for e, hs in enumerate(halves):
            qat_ref[:, k * QB + e * BLOCK:k * QB + (e + 1) * BLOCK] = t128(qa_ref[hs, sl])
            qit_ref[sl, hs] = t128(qi_ref[hs, sl])
    w_t = jnp.concatenate([wi_ref[hs, :].T for hs in halves], axis=1)

    q_lane = lax.broadcasted_iota(i32, (BLOCK, QB), 1)
    k_row = lax.broadcasted_iota(i32, (BLOCK, QB), 0)
    q_pos = i * QB + q_lane

    def idx_body(c, carry):
        for e in range(KC // BLOCK):
            cc = (KC // BLOCK) * c + e
            r = rows(cc, BLOCK)
            kc = ki_ref[r, 0:D_I]
            acc = jnp.zeros((BLOCK, QB), f32)
            for h in range(H_I):
                d = _dot(kc, qit_ref[h * D_I:(h + 1) * D_I, :])
                acc = acc + w_t[h:h + 1, :] * jnp.maximum(d, 0.0)
            bits = lax.bitcast_convert_type(acc + 0.0, i32)
            key = jnp.where(bits >= 0, bits, bits ^ 0x7FFFFFFF)
            key = jnp.where(cc * BLOCK + k_row <= q_pos, key, INT_MIN)
            key_ref[r, :] = key
            hi_ref[r, :] = (key >> 16).astype(jnp.int16)
            lo_ref[r, :] = ((key & 0xFFFF) + I16_MIN).astype(jnp.int16)
        return carry
    lax.fori_loop(0, nchunk, idx_body, 0)

    need = jnp.minimum(n_sel, q_pos[0:1, :] + 1).astype(f32)
    one, zero = jnp.ones((), jnp.bfloat16), jnp.zeros((), jnp.bfloat16)

    n_total = key_ref.shape[0] // KC
    n_half = max(n_total // 2, 1)
    span = jnp.where(nchunk <= n_half, n_half, n_total)

    def pad_body(c, carry):
        hi_ref[rows(c), :] = jnp.full((KC, QB), I16_MIN, jnp.int16)
        lo_ref[rows(c), :] = jnp.full((KC, QB), I16_MIN, jnp.int16)
        return carry
    lax.fori_loop(nchunk, span, pad_body, 0)

    def search(n_chunks):
        def count_ge(ref, cand):
            cnt = None
            for c in range(n_chunks):
                ind = jnp.where(ref[c * KC:(c + 1) * KC, :] >= cand, one, zero)
                parts = [ind[g * PACK:(g + 1) * PACK] for g in range(KC // PACK)]
                while len(parts) > 1:
                    parts = [a + b for a, b in zip(parts[0::2], parts[1::2])]
                cnt = parts[0] if cnt is None else cnt + parts[0]
            return jnp.sum(cnt.astype(f32), axis=0, keepdims=True)

        def bisect(ref, want):
            def body(k, t_u):
                cand_u = t_u | lax.shift_left(jnp.int32(1), 15 - k)
                ok = count_ge(ref, (cand_u + I16_MIN).astype(jnp.int16)) >= want
                return jnp.where(ok, cand_u, t_u)
            return lax.fori_loop(0, 16, body, jnp.zeros((1, QB), i32))

        hi_u = bisect(hi_ref, need)
        hi_t = (hi_u + I16_MIN).astype(jnp.int16)
        above = jnp.where(hi_u == 0xFFFF, 0.0,
                          count_ge(hi_ref, (jnp.minimum(hi_u + 1, 0xFFFF) + I16_MIN).astype(jnp.int16)))
        for c in range(n_chunks):
            cs = slice(c * KC, (c + 1) * KC)
            lo_ref[cs, :] = jnp.where(hi_ref[cs, :] == hi_t, lo_ref[cs, :], jnp.int16(I16_MIN))
        lo_u = bisect(lo_ref, need - above)
        return ((hi_u + I16_MIN) * 65536 + lo_u,
                above + count_ge(lo_ref, (lo_u + I16_MIN).astype(jnp.int16)))

    if n_half == n_total:
        thr, picked = search(n_total)
    else:
        thr, picked = lax.cond(nchunk <= n_half, lambda: search(n_half), lambda: search(n_total))


    @pl.when(jnp.max(picked - need) > 0.0)
    def _():
        def gt_body(c, cnt):
            return cnt + jnp.sum(jnp.where(key_ref[rows(c), :] > thr, 1.0, 0.0), axis=0, keepdims=True)
        allowed = need - lax.fori_loop(0, nchunk, gt_body, jnp.zeros((1, QB), f32))
        tri = (lax.broadcasted_iota(i32, (KC, KC), 0) >= lax.broadcasted_iota(i32, (KC, KC), 1))
        tri = jnp.where(tri, 1.0, 0.0).astype(jnp.bfloat16)

        def fix_body(c, seen):
            k = key_ref[rows(c), :]
            tie = jnp.where(k == thr, 1.0, 0.0)
            rank = _dot(tri, tie.astype(jnp.bfloat16)) + seen
            key_ref[rows(c), :] = jnp.where((tie > 0.0) & (rank > allowed), thr - 1, k)
            return seen + jnp.sum(tie, axis=0, keepdims=True)
        lax.fori_loop(0, nchunk, fix_body, jnp.zeros((1, QB), f32))

    m_ref[...] = jnp.full(m_ref.shape, M_INIT, f32)
    l_ref[...] = jnp.zeros(l_ref.shape, f32)
    for acc_ref in acc_refs:
        acc_ref[...] = jnp.zeros(acc_ref.shape, f32)
    c_a = HD_A ** -0.5 * LOG2E

    def logits_of(c, h):
        return _dot(kv_ref[rows(c), 0:128], qat_ref[:, h * QB:(h + 1) * QB])

    def chunk_step(c, z_in, z_out):
        nxt = jnp.minimum(c + 1, nchunk - 1)
        vt = vat_ref[c]
        bias = jnp.where(key_ref[rows(c), :] >= thr, 0.0, NEG)
        m_all, l_all = m_ref[...], l_ref[...]
        m_out, l_out = [], []
        for h in range(H_A):
            hs = slice(h * QB, (h + 1) * QB)
            z = z_in[h][...] + bias
            m_old = m_all[:, hs]
            m_new = jnp.maximum(m_old, jnp.max(z, axis=0, keepdims=True))
            z_out[h][...] = logits_of(nxt, h)
            alpha = jnp.exp2((m_old - m_new) * c_a)
            p = jnp.exp2((z - m_new) * c_a)
            l_out.append(alpha * l_all[:, hs] + jnp.sum(p, axis=0, keepdims=True))
            acc_refs[h][...] = alpha * acc_refs[h][...] + _dot(vt, p.astype(MXU_DTYPE))
            m_out.append(m_new)
        m_ref[...] = jnp.concatenate(m_out, axis=1)
        l_ref[...] = jnp.concatenate(l_out, axis=1)

    for h in range(H_A):
        za_refs[h][...] = logits_of(0, h)

    def att_body(k2, carry):
        chunk_step(2 * k2, za_refs, zb_refs)

        @pl.when(2 * k2 + 1 < nchunk)
        def _():
            chunk_step(2 * k2 + 1, zb_refs, za_refs)
        return carry
    lax.fori_loop(0, (nchunk + 1) // 2, att_body, 0)

    inv_l = 1.0 / l_ref[...]
    for k in range(H_A):
        ya_t = acc_refs[k][...] * inv_l[:, k * QB:(k + 1) * QB]
        for e, hs in enumerate(halves):
            ya_ref[hs, k * LANES:(k + 1) * LANES] = ya_t[:, e * BLOCK:(e + 1) * BLOCK].T.astype(MXU_DTYPE)

    b_lane = lax.broadcasted_iota(i32, (BLOCK, LANES), 1)
    b_row = lax.broadcasted_iota(i32, (BLOCK, LANES), 0)
    sink = sink_ref[...]
    zero_pad = jnp.zeros((HD_B, LANES), MXU_DTYPE)
    per_g = H_B // KVH_B
    for e, hs in enumerate(halves):
        blk = 2 * i + e
        ip = jnp.maximum(blk - 1, 0)
        kband = jnp.concatenate([kv_ref[rows(ip, BLOCK), 256:384], kv_ref[rows(blk, BLOCK), 256:384]], axis=0)
        ext = []
        for k in range(8):
            t = t128(qb_ref[hs, k * LANES:(k + 1) * LANES])
            for o in range(2):
                piece = t[o * HD_B:(o + 1) * HD_B, :]
                grp = (2 * k + o) // per_g
                ext.append(jnp.concatenate([piece, zero_pad] if grp == 0 else [zero_pad, piece], axis=0))
        s = _dot(kband, jnp.concatenate(ext, axis=1))
        prev_open = jnp.where(blk > 0, 0.0, NEG)
        bias = jnp.concatenate([jnp.where(b_row > b_lane, prev_open, NEG),
                                jnp.where(b_row <= b_lane, 0.0, NEG)], axis=0)
        bias2 = jnp.concatenate([bias, bias], axis=1)
        vbands = [jnp.concatenate([vbt_ref[ip][g * HD_B:(g + 1) * HD_B, :],
                                   vbt_ref[blk][g * HD_B:(g + 1) * HD_B, :]], axis=1) for g in range(KVH_B)]
        for k in range(8):
            cols = slice(2 * k * LANES, (2 * k + 2) * LANES)
            z = s[:, cols] + bias2
            m = jnp.maximum(jnp.max(z, axis=0, keepdims=True), sink[:, cols])
            ex = jnp.exp(z - m)
            inv = 1.0 / (jnp.sum(ex, axis=0, keepdims=True) + jnp.exp(sink[:, cols] - m))
            og = _dot(vbands[2 * k // per_g], ex.astype(MXU_DTYPE)) * inv
            yb_ref[hs, k * LANES:(k + 1) * LANES] = (
                jnp.concatenate([og[:, 0:LANES], og[:, LANES:2 * LANES]], axis=0).T.astype(MXU_DTYPE))


def _attn(qa, qi, qb, wi, kv, ki, sink_row, *, batch, seq):
    n = qa.shape[0]
    nstep = seq // QB
    n_sel = min(K_MAX, seq // 4)
    blk = lambda w: pl.BlockSpec((QB, w), lambda b, i: (b * nstep + i, 0))
    per_b = lambda w: pl.BlockSpec((seq, w), lambda b, i: (b, 0))
    return pl.pallas_call(
        functools.partial(_attn_kernel, n_sel=n_sel),
        grid=(batch, nstep),
        in_specs=[blk(1024), blk(1024), blk(1024), blk(LANES), per_b(512), per_b(LANES),
                  pl.BlockSpec((1, H_B * LANES), lambda b, i: (0, 0))],
        out_specs=[blk(1024), blk(1024)],
        out_shape=[jax.ShapeDtypeStruct((n, 1024), MXU_DTYPE)] * 2,
        scratch_shapes=[pltpu.VMEM((nstep, HD_A, KC), MXU_DTYPE),
                        pltpu.VMEM((seq // BLOCK, LANES, BLOCK), MXU_DTYPE),
                        pltpu.VMEM((HD_A, H_A * QB), MXU_DTYPE),
                        pltpu.VMEM((H_I * D_I, QB), MXU_DTYPE),
                        pltpu.VMEM((seq, QB), i32),
                        pltpu.VMEM((seq, QB), jnp.int16),
                        pltpu.VMEM((seq, QB), jnp.int16),
                        pltpu.VMEM((1, H_A * QB), f32),
                        pltpu.VMEM((1, H_A * QB), f32)]
                       + [pltpu.VMEM((HD_A, QB), f32)] * H_A
                       + [pltpu.VMEM((KC, QB), f32)] * (2 * H_A),

        compiler_params=_params("arbitrary", "arbitrary"),
        name="mixers",
    )(qa, qi, qb, wi, kv, ki, sink_row)


def _out_kernel(h_ref, ya_ref, yb_ref, ga_ref, gb_ref, gt_ref, woa_ref, wob_ref, wout_ref, o_ref, *, tn, tc):
    ya, yb, gate = ya_ref[...], yb_ref[...], gt_ref[0]
    d = o_ref.shape[1]
    for c in range(d // tn):
        sl = slice(c * tn, (c + 1) * tn)
        merged = (ga_ref[:, sl].astype(f32) * _dot(ya, woa_ref[:, sl])
                  + gb_ref[:, sl].astype(f32) * _dot(yb, wob_ref[:, sl])).astype(MXU_DTYPE)
        for e in range(d // tc):
            oc = slice(e * tc, (e + 1) * tc)
            part = gate[:, oc] * _dot(merged, wout_ref[sl, oc])
            if c == 0:
                o_ref[:, oc] = h_ref[:, oc] + part
            else:
                o_ref[:, oc] += part


def _out_proj(h, ya, yb, ga, gb, gate, woa, wob, wout, layer, *, seq, tm, tn, tc):
    n, d = h.shape
    k = ya.shape[1]
    per_b = seq // tm
    resident = lambda r, c: pl.BlockSpec((None, r, c), lambda i: (layer, 0, 0), pipeline_mode=pl.Buffered(1))
    return pl.pallas_call(
        functools.partial(_out_kernel, tn=tn, tc=tc),
        grid=(n // tm,),
        in_specs=[pl.BlockSpec((tm, d), lambda i: (i, 0)),
                  pl.BlockSpec((tm, k), lambda i: (i, 0)),
                  pl.BlockSpec((tm, k), lambda i: (i, 0)),
                  pl.BlockSpec((tm, d), lambda i: (i, 0)),
                  pl.BlockSpec((tm, d), lambda i: (i, 0)),
                  pl.BlockSpec((1, 1, d), lambda i: (i // per_b, 0, 0)),
                  resident(k, d), resident(k, d), resident(d, d)],
        out_specs=pl.BlockSpec((tm, d), lambda i: (i, 0)),
        out_shape=jax.ShapeDtypeStruct((n, d), f32),
        compiler_params=_params("arbitrary"),
        name="out_proj",
    )(h, ya, yb, ga, gb, gate, woa, wob, wout)


def _tiles(seq, d_ff):
    tm = min(seq, 512)
    tm_ffn = min(seq, 1024)
    tf = 512 if d_ff % 512 == 0 else 256
    return tm, tm_ffn, tf


def _split_w_in(w_in, d):
    sizes = (H_A * HD_A, HD_A, HD_A, H_I * D_I, D_I, H_I, H_B * HD_B, KVH_B * HD_B, KVH_B * HD_B, d, d)
    offs = [0]
    for s in sizes:
        offs.append(offs[-1] + s)
    seg = lambda k: w_in[:, :, offs[k]:offs[k + 1]]
    pad = lambda a, w: jnp.pad(a, ((0, 0), (0, 0), (0, w - a.shape[-1])))
    q = jnp.concatenate([seg(0), seg(3), seg(6)], axis=-1)
    gates = jnp.concatenate([seg(9), seg(10)], axis=-1)
    sel = jnp.concatenate([seg(1), seg(2), seg(7), seg(8), pad(seg(4), LANES), pad(seg(5), LANES)], axis=-1)
    return q.astype(MXU_DTYPE), gates.astype(MXU_DTYPE), sel.astype(MXU_DTYPE)


def kernel(x, c, positions, ada_w, ada_b, norm_ffn1_g, ffn1_w_gate, ffn1_w_up, ffn1_w_down, norm_mix_g, w_in, qn_a_g, kn_a_g, qn_b_g, kn_b_g, sinks, w_o_a, w_o_b, w_out, norm_ffn2_g, ffn2_w_gate, ffn2_w_up, ffn2_w_down):
    batch, seq, d = x.shape
    depth = ada_w.shape[0]
    d_ff = ffn1_w_gate.shape[-1]
    n = batch * seq
    assert seq % QB == 0 and d % 1024 == 0
    tm, tm_ffn, tf = _tiles(seq, d_ff)
    tn = min(d, 512)

    mod = _ada_mod(c, ada_w, ada_b).reshape(depth, batch, N_MOD, 1, d)
    rope = _rope_tables(positions)
    wq, wgate, wsel = _split_w_in(w_in, d)
    cast = lambda w: w.astype(MXU_DTYPE)
    w1 = (cast(ffn1_w_gate), cast(ffn1_w_up), cast(ffn1_w_down))
    w2 = (cast(ffn2_w_gate), cast(ffn2_w_up), cast(ffn2_w_down))
    woa, wob, wout = cast(w_o_a), cast(w_o_b), cast(w_out)
    gains = jnp.stack([qn_a_g, kn_a_g, jnp.tile(qn_b_g, (1, LANES // HD_B)),
                       jnp.tile(kn_b_g, (1, LANES // HD_B))], axis=1)
    sink_rows = jnp.repeat(sinks, LANES, axis=1).reshape(depth, 1, H_B * LANES)

    h = x.reshape(n, d)
    for l in range(depth):
        m = [mod[l, :, k] for k in range(N_MOD)]
        h = _ffn(h, m[0], m[1], m[2], norm_ffn1_g[l][None], *w1, l, seq=seq, tm=tm_ffn, rows=tm, tf=tf, tn=tn)
        qa, qi, qb, ga, gb, kv, ki, wi = _proj(h, m[3], m[4], norm_mix_g[l][None], wq, wgate, wsel,
                                               gains[l], rope, l, seq=seq, tm=tm)
        ya, yb = _attn(qa, qi, qb, wi, kv, ki, sink_rows[l], batch=batch, seq=seq)
        h = _out_proj(h, ya, yb, ga, gb, m[5], woa, wob, wout, l, seq=seq, tm=tm, tn=2 * tn, tc=tn)
        h = _ffn(h, m[6], m[7], m[8], norm_ffn2_g[l][None], *w2, l, seq=seq, tm=tm_ffn, rows=tm, tf=tf, tn=tn)
    return h.reshape(batch, seq, d)
```

```python
import functools

import jax
import jax.numpy as jnp
from jax import lax
from jax.experimental import pallas as pl
from jax.experimental.pallas import tpu as pltpu

H_A, HD_A = 8, 128
H_I, D_I, D_I_ROPE = 16, 64, 32
K_MAX = 256
H_B, HD_B, KVH_B = 16, 64, 2
BLOCK = 128
ROPE_THETA = 10000.0
EPS = 1e-6
NEG = -1e30
N_MOD = 9

LANES = 128
V7X_VMEM_LIMIT_BYTES = 60 * 1024 * 1024
MXU_DTYPE = jnp.bfloat16

INT_MIN = -2147483648
M_INIT = -1e29
LOG2E = 1.4426950408889634
SEL_COLS = 768
N_BIG = 3 * 1024

f32 = jnp.float32
i32 = jnp.int32


def _sigmoid(x):
    return 0.5 * jnp.tanh(0.5 * x) + 0.5


def _dot(a, b):
    return jnp.dot(a, b, preferred_element_type=f32)


def _params(*sem):
    return pltpu.CompilerParams(dimension_semantics=sem, vmem_limit_bytes=V7X_VMEM_LIMIT_BYTES)


def _ada_kernel(c_ref, w_ref, b_ref, o_ref):
    c = c_ref[...]
    act = (c * jax.nn.sigmoid(c)).astype(MXU_DTYPE)
    o_ref[0] = _dot(act, w_ref[0].astype(MXU_DTYPE)) + b_ref[0]


def _ada_mod(c, ada_w, ada_b):
    L, D, N = ada_w.shape
    B = c.shape[0]
    tn = min(N, 1024)
    return pl.pallas_call(
        _ada_kernel,
        grid=(L, N // tn),
        in_specs=[pl.BlockSpec((B, D), lambda l, j: (0, 0)),
                  pl.BlockSpec((1, D, tn), lambda l, j: (l, 0, j)),
                  pl.BlockSpec((1, 1, tn), lambda l, j: (l, 0, j))],
        out_specs=pl.BlockSpec((1, B, tn), lambda l, j: (l, 0, j)),
        out_shape=jax.ShapeDtypeStruct((L, B, N), f32),
        compiler_params=_params("arbitrary", "arbitrary"),
        name="ada_mod",
    )(c, ada_w, ada_b.reshape(L, 1, N))


def _rope_kernel(pos_ref, freq_ref, sign_ref, o_ref):
    ang = pos_ref[...].astype(f32) * freq_ref[...]
    lane = lax.broadcasted_iota(i32, ang.shape, 1)

    def rep(v, n):
        return jnp.where(lane < n, v, pltpu.roll(v, n, 1))

    for t, (trig, fill) in enumerate(((jnp.cos(ang), 1.0), (jnp.sin(ang), 0.0))):
        a = rep(trig, HD_A // 2)
        b = rep(rep(pltpu.roll(trig, LANES - HD_A // 2, 1), HD_B // 2), HD_B)
        i_ = rep(pltpu.roll(trig, LANES - HD_A // 2 - HD_B // 2, 1), D_I_ROPE // 2)
        i_ = rep(jnp.where(lane < D_I_ROPE, i_, fill), D_I)
        for k, tab in enumerate((a, b, i_)):
            if t:
                tab = tab * sign_ref[k:k + 1, :]
            o_ref[:, (2 * k + t) * LANES:(2 * k + t + 1) * LANES] = tab


def _rope_rows():
    lane = jnp.arange(LANES)

    def inv_freq(dim):
        return ROPE_THETA ** (-jnp.arange(0, dim, 2, dtype=f32) / dim)

    freq = jnp.concatenate([inv_freq(HD_A), inv_freq(HD_B), inv_freq(D_I_ROPE),
                            jnp.zeros((LANES - (HD_A + HD_B + D_I_ROPE) // 2,), f32)])[None]
    sa = jnp.where(lane % HD_A < HD_A // 2, -1.0, 1.0)
    sb = jnp.where(lane % HD_B < HD_B // 2, -1.0, 1.0)
    si = jnp.where(lane % D_I < D_I_ROPE // 2, -1.0, 1.0)
    return freq.astype(f32), jnp.stack([sa, sb, si]).astype(f32)


def _rope_tables(positions):
    n = positions.size
    tm = min(n, 1024)
    freq, sign = _rope_rows()
    return pl.pallas_call(
        _rope_kernel,
        grid=(n // tm,),
        in_specs=[pl.BlockSpec((tm, 1), lambda i: (i, 0)),
                  pl.BlockSpec((1, LANES), lambda i: (0, 0)),
                  pl.BlockSpec((3, LANES), lambda i: (0, 0))],
        out_specs=pl.BlockSpec((tm, 6 * LANES), lambda i: (i, 0)),
        out_shape=jax.ShapeDtypeStruct((n, 6 * LANES), f32),
        compiler_params=_params("arbitrary"),
        name="rope_tables",
    )(positions.reshape(n, 1), freq, sign)


NORM_SLAB = 16


def _norm_modulate_to(u_ref, h_ref, g, shift, scale, copy_ref=None):
    geff = g * (1.0 + scale)
    for s in range(h_ref.shape[0] // NORM_SLAB):
        rs = slice(s * NORM_SLAB, (s + 1) * NORM_SLAB)
        x = h_ref[rs, :]
        ms = jnp.mean(x * x, axis=-1, keepdims=True)
        u_ref[rs, :] = (x * lax.rsqrt(ms + EPS) * geff + shift).astype(u_ref.dtype)
        if copy_ref is not None:
            copy_ref[rs, :] = x


def _head_norm(x, gain, lane, head_dim):
    sq = x * x
    if head_dim == LANES:
        ms = jnp.mean(sq, axis=-1, keepdims=True)
    else:
        lo = lane < head_dim
        s_lo = jnp.sum(jnp.where(lo, sq, 0.0), axis=-1, keepdims=True)
        s_hi = jnp.sum(jnp.where(lo, 0.0, sq), axis=-1, keepdims=True)
        ms = jnp.where(lo, s_lo, s_hi) * (1.0 / head_dim)
    return x * lax.rsqrt(ms + EPS) * gain


def _rotate(y, cos, sin, lane, half, period):
    if 2 * half == LANES:
        partner = pltpu.roll(y, half, 1)
    else:
        fwd = pltpu.roll(y, LANES - half, 1)
        bwd = pltpu.roll(y, half, 1)
        partner = jnp.where(lane % period < half, fwd, bwd)
    return y * cos + partner * sin


def _ffn_kernel(h_ref, sh_ref, sc_ref, gt_ref, g_ref, wg_ref, wu_ref, wd_ref, o_ref, u_ref, *, rows, tn):
    j = pl.program_id(1)

    @pl.when(j == 0)
    def _():
        _norm_modulate_to(u_ref, h_ref, g_ref[...], sh_ref[0], sc_ref[0], copy_ref=o_ref)

    half_gate = 0.5 * gt_ref[0]
    for r in range(u_ref.shape[0] // rows):
        rs = slice(r * rows, (r + 1) * rows)
        u = u_ref[rs, :]
        a = _dot(u, wg_ref[...])
        b = _dot(u, wu_ref[...])
        act = (a * _sigmoid(a) * b).astype(MXU_DTYPE)
        for c in range(o_ref.shape[1] // tn):
            sl = slice(c * tn, (c + 1) * tn)
            o_ref[rs, sl] += half_gate[:, sl] * _dot(act, wd_ref[:, sl])


def _ffn(h, shift, scale, gate, g, wg, wu, wd, layer, *, seq, tm, rows, tf, tn):
    n, d = h.shape
    f = wg.shape[-1]
    per_b = seq // tm
    row = pl.BlockSpec((1, 1, d), lambda i, j: (i // per_b, 0, 0))
    return pl.pallas_call(
        functools.partial(_ffn_kernel, rows=rows, tn=tn),
        grid=(n // tm, f // tf),
        in_specs=[pl.BlockSpec((tm, d), lambda i, j: (i, 0)), row, row, row,
                  pl.BlockSpec((1, d), lambda i, j: (0, 0)),
                  pl.BlockSpec((None, d, tf), lambda i, j: (layer, 0, j)),
                  pl.BlockSpec((None, d, tf), lambda i, j: (layer, 0, j)),
                  pl.BlockSpec((None, tf, d), lambda i, j: (layer, j, 0))],
        out_specs=pl.BlockSpec((tm, d), lambda i, j: (i, 0)),
        out_shape=jax.ShapeDtypeStruct((n, d), f32),
        scratch_shapes=[pltpu.VMEM((tm, d), MXU_DTYPE)],
        compiler_params=_params("arbitrary", "arbitrary"),
        name="ffn",
    )(h, shift, scale, gate, g, wg, wu, wd)


def _proj_kernel(h_ref, sh_ref, sc_ref, g_ref, wq_ref, wgate_lo_ref, wgate_hi_ref, wsel_ref, gains_ref, rope_ref,
                 qa_ref, qi_ref, qb_ref, ga_ref, gb_ref, kv_ref, ki_ref, wi_ref, u_ref, ra_ref, rb_ref):
    j = pl.program_id(1)
    tm = h_ref.shape[0]
    tn = ra_ref.shape[1]
    lane = lax.broadcasted_iota(i32, (tm, LANES), 1)

    def table(k):
        return rope_ref[:, k * LANES:(k + 1) * LANES]

    def gain(k):
        return gains_ref[k:k + 1, :]

    def rope_a(y):
        return _rotate(y, table(0), table(1), lane, HD_A // 2, HD_A)

    def rope_b(y):
        return _rotate(y, table(2), table(3), lane, HD_B // 2, HD_B)

    def rope_i(y):
        return _rotate(y, table(4), table(5), lane, D_I_ROPE // 2, D_I)

    def q_dot(c):
        return _dot(u_ref[...], wq_ref[:, c * tn:(c + 1) * tn])

    def gate_dot():
        u = u_ref[...]
        return jnp.concatenate([_dot(u, wgate_lo_ref[...]), _dot(u, wgate_hi_ref[...])], axis=1)

    @pl.when(j == 0)
    def _():
        _norm_modulate_to(u_ref, h_ref, g_ref[...], sh_ref[0], sc_ref[0])
        sel = _dot(u_ref[...], wsel_ref[...])
        ra_ref[...] = q_dot(0)
        ka = rope_a(_head_norm(sel[:, 0:128], gain(1), lane, HD_A))
        kb = rope_b(_head_norm(sel[:, 256:384], gain(3), lane, HD_B))
        kv_ref[:, 0:128] = ka.astype(MXU_DTYPE)
        kv_ref[:, 128:256] = sel[:, 128:256].astype(MXU_DTYPE)
        kv_ref[:, 256:384] = kb.astype(MXU_DTYPE)
        kv_ref[:, 384:512] = sel[:, 384:512].astype(MXU_DTYPE)
        ki_ref[...] = rope_i(sel[:, 512:640]).astype(MXU_DTYPE)
        wi_ref[...] = sel[:, 640:768] * (H_I ** -0.5 * D_I ** -0.5)

    @pl.when(j == 1)
    def _():
        rb_ref[...] = q_dot(1)
        for k in range(8):
            y = _head_norm(ra_ref[:, k * LANES:(k + 1) * LANES], gain(0), lane, HD_A)
            qa_ref[:, k * LANES:(k + 1) * LANES] = rope_a(y).astype(MXU_DTYPE)

    @pl.when(j == 2)
    def _():
        ra_ref[...] = q_dot(2)
        for k in range(8):
            qi_ref[:, k * LANES:(k + 1) * LANES] = rope_i(rb_ref[:, k * LANES:(k + 1) * LANES]).astype(MXU_DTYPE)

    @pl.when(j == 3)
    def _():
        rb_ref[...] = gate_dot()
        for k in range(8):
            y = _head_norm(ra_ref[:, k * LANES:(k + 1) * LANES], gain(2), lane, HD_B)
            qb_ref[:, k * LANES:(k + 1) * LANES] = (rope_b(y) * (HD_B ** -0.5)).astype(MXU_DTYPE)

    per_gate = ga_ref.shape[1] // tn

    def put_gate(c, res):
        ref = ga_ref if c < per_gate else gb_ref
        c = c % per_gate
        ref[:, c * tn:(c + 1) * tn] = _sigmoid(res).astype(MXU_DTYPE)

    @pl.when(j == 4)
    def _():
        put_gate(1, gate_dot())
        put_gate(0, rb_ref[...])

    for c in range(2, 2 * per_gate):
        @pl.when(j == 3 + c)
        def _(c=c):
            put_gate(c, gate_dot())


def _proj(h, shift, scale, g, wq, wgate, wsel, gains, rope, layer, *, seq, tm):
    n, d = h.shape
    d_gate = wgate.shape[-1] // 2
    per_b = seq // tm
    tn = 1024
    n_q = N_BIG // tn
    row = pl.BlockSpec((1, 1, d), lambda i, j: (i // per_b, 0, 0))
    tile = lambda w: pl.BlockSpec((tm, w), lambda i, j: (i, 0))
    resident = lambda w: pl.BlockSpec((None, d, w), lambda i, j: (layer, 0, 0), pipeline_mode=pl.Buffered(1))
    bf = lambda w: jax.ShapeDtypeStruct((n, w), MXU_DTYPE)
    return pl.pallas_call(
        _proj_kernel,
        grid=(n // tm, n_q + 2 * d_gate // tn),
        in_specs=[tile(d), row, row,
                  pl.BlockSpec((1, d), lambda i, j: (0, 0)),
                  resident(N_BIG),
                  pl.BlockSpec((None, d, tn // 2), lambda i, j: (layer, 0, 2 * jnp.maximum(j - n_q, 0))),
                  pl.BlockSpec((None, d, tn // 2), lambda i, j: (layer, 0, 2 * jnp.maximum(j - n_q, 0) + 1)),
                  resident(SEL_COLS),
                  pl.BlockSpec((4, LANES), lambda i, j: (0, 0)),
                  tile(6 * LANES)],
        out_specs=[tile(1024), tile(1024), tile(1024), tile(d_gate), tile(d_gate),
                   tile(512), tile(LANES), tile(LANES)],
        out_shape=[bf(1024), bf(1024), bf(1024), bf(d_gate), bf(d_gate), bf(512), bf(LANES),
                   jax.ShapeDtypeStruct((n, LANES), f32)],
        scratch_shapes=[pltpu.VMEM((tm, d), MXU_DTYPE),
                        pltpu.VMEM((tm, tn), f32), pltpu.VMEM((tm, tn), f32)],
        compiler_params=_params("arbitrary", "arbitrary"),
        name="mix_proj",
    )(h, shift, scale, g, wq, wgate, wgate, wsel, gains, rope)


QB = 256
KC = 256
I16_MIN = -32768
PACK = 16


def _attn_kernel(qa_ref, qi_ref, qb_ref, wi_ref, kv_ref, ki_ref, sink_ref,
                 ya_ref, yb_ref,
                 vat_ref, vbt_ref, qat_ref, qit_ref, key_ref, hi_ref, lo_ref, m_ref, l_ref, *head_refs,
                 n_sel):
    acc_refs, za_refs, zb_refs = head_refs[:H_A], head_refs[H_A:2 * H_A], head_refs[2 * H_A:]
    i = pl.program_id(1)
    nstep = pl.num_programs(1)
    nchunk = i + 1
    halves = (slice(0, BLOCK), slice(BLOCK, QB))

    def rows(c, size=KC):
        return pl.ds(pl.multiple_of(c * size, size), size)

    def t128(x):
        return x.astype(f32).T.astype(MXU_DTYPE)

    @pl.when(i == 0)
    def _():
        def body(c, carry):
            b0, b1 = rows(2 * c, BLOCK), rows(2 * c + 1, BLOCK)
            vat_ref[c] = jnp.concatenate([t128(kv_ref[b0, 128:256]), t128(kv_ref[b1, 128:256])], axis=1)
            vbt_ref[2 * c] = t128(kv_ref[b0, 384:512])
            vbt_ref[2 * c + 1] = t128(kv_ref[b1, 384:512])
            return carry
        lax.fori_loop(0, nstep, body, 0)

    for k in range(8):
        sl = slice(k * LANES, (k + 1) * LANES)
        for e, hs in enumerate(halves):
            qat_ref[:, k * QB + e * BLOCK:k * QB + (e + 1) * BLOCK] = t128(qa_ref[hs, sl])
            qit_ref[sl, hs] = t128(qi_ref[hs, sl])
    w_t = jnp.concatenate([wi_ref[hs, :].T for hs in halves], axis=1)

    q_lane = lax.broadcasted_iota(i32, (BLOCK, QB), 1)
    k_row = lax.broadcasted_iota(i32, (BLOCK, QB), 0)
    q_pos = i * QB + q_lane

    def idx_body(c, carry):
        for e in range(KC // BLOCK):
            cc = (KC // BLOCK) * c + e
            r = rows(cc, BLOCK)
            kc = ki_ref[r, 0:D_I]
            acc = jnp.zeros((BLOCK, QB), f32)
            for h in range(H_I):
                d = _dot(kc, qit_ref[h * D_I:(h + 1) * D_I, :])
                acc = acc + w_t[h:h + 1, :] * jnp.maximum(d, 0.0)
            bits = lax.bitcast_convert_type(acc + 0.0, i32)
            key = jnp.where(bits >= 0, bits, bits ^ 0x7FFFFFFF)
            key = jnp.where(cc * BLOCK + k_row <= q_pos, key, INT_MIN)
            key_ref[r, :] = key
            hi_ref[r, :] = (key >> 16).astype(jnp.int16)
            lo_ref[r, :] = ((key & 0xFFFF) + I16_MIN).astype(jnp.int16)
        return carry
    lax.fori_loop(0, nchunk, idx_body, 0)

    need = jnp.minimum(n_sel, q_pos[0:1, :] + 1).astype(f32)
    one, zero = jnp.ones((), jnp.bfloat16), jnp.zeros((), jnp.bfloat16)

    n_total = key_ref.shape[0] // KC
    n_half = max(n_total // 2, 1)
    span = jnp.where(nchunk <= n_half, n_half, n_total)

    def pad_body(c, carry):
        hi_ref[rows(c), :] = jnp.full((KC, QB), I16_MIN, jnp.int16)
        lo_ref[rows(c), :] = jnp.full((KC, QB), I16_MIN, jnp.int16)
        return carry
    lax.fori_loop(nchunk, span, pad_body, 0)

    def search(n_chunks):
        def count_ge(ref, cand):
            cnt = None
            for c in range(n_chunks):
                ind = jnp.where(ref[c * KC:(c + 1) * KC, :] >= cand, one, zero)
                parts = [ind[g * PACK:(g + 1) * PACK] for g in range(KC // PACK)]
                while len(parts) > 1:
                    parts = [a + b for a, b in zip(parts[0::2], parts[1::2])]
                cnt = parts[0] if cnt is None else cnt + parts[0]
            return jnp.sum(cnt.astype(f32), axis=0, keepdims=True)

        def bisect(ref, want):
            def body(k, t_u):
                cand_u = t_u | lax.shift_left(jnp.int32(1), 15 - k)
                ok = count_ge(ref, (cand_u + I16_MIN).astype(jnp.int16)) >= want
                return jnp.where(ok, cand_u, t_u)
            return lax.fori_loop(0, 16, body, jnp.zeros((1, QB), i32))

        hi_u = bisect(hi_ref, need)
        hi_t = (hi_u + I16_MIN).astype(jnp.int16)
        above = jnp.where(hi_u == 0xFFFF, 0.0,
                          count_ge(hi_ref, (jnp.minimum(hi_u + 1, 0xFFFF) + I16_MIN).astype(jnp.int16)))
        for c in range(n_chunks):
            cs = slice(c * KC, (c + 1) * KC)
            lo_ref[cs, :] = jnp.where(hi_ref[cs, :] == hi_t, lo_ref[cs, :], jnp.int16(I16_MIN))
        lo_u = bisect(lo_ref, need - above)
        return ((hi_u + I16_MIN) * 65536 + lo_u,
                above + count_ge(lo_ref, (lo_u + I16_MIN).astype(jnp.int16)))

    if n_half == n_total:
        thr, picked = search(n_total)
    else:
        thr, picked = lax.cond(nchunk <= n_half, lambda: search(n_half), lambda: search(n_total))


    @pl.when(jnp.max(picked - need) > 0.0)
    def _():
        def gt_body(c, cnt):
            return cnt + jnp.sum(jnp.where(key_ref[rows(c), :] > thr, 1.0, 0.0), axis=0, keepdims=True)
        allowed = need - lax.fori_loop(0, nchunk, gt_body, jnp.zeros((1, QB), f32))
        tri = (lax.broadcasted_iota(i32, (KC, KC), 0) >= lax.broadcasted_iota(i32, (KC, KC), 1))
        tri = jnp.where(tri, 1.0, 0.0).astype(jnp.bfloat16)

        def fix_body(c, seen):
            k = key_ref[rows(c), :]
            tie = jnp.where(k == thr, 1.0, 0.0)
            rank = _dot(tri, tie.astype(jnp.bfloat16)) + seen
            key_ref[rows(c), :] = jnp.where((tie > 0.0) & (rank > allowed), thr - 1, k)
            return seen + jnp.sum(tie, axis=0, keepdims=True)
        lax.fori_loop(0, nchunk, fix_body, jnp.zeros((1, QB), f32))

    m_ref[...] = jnp.full(m_ref.shape, M_INIT, f32)
    l_ref[...] = jnp.zeros(l_ref.shape, f32)
    for acc_ref in acc_refs:
        acc_ref[...] = jnp.zeros(acc_ref.shape, f32)
    c_a = HD_A ** -0.5 * LOG2E

    def logits_of(c, h):
        return _dot(kv_ref[rows(c), 0:128], qat_ref[:, h * QB:(h + 1) * QB])

    def chunk_step(c, z_in, z_out):
        nxt = jnp.minimum(c + 1, nchunk - 1)
        vt = vat_ref[c]
        bias = jnp.where(key_ref[rows(c), :] >= thr, 0.0, NEG)
        m_all, l_all = m_ref[...], l_ref[...]
        m_out, l_out = [], []
        for h in range(H_A):
            hs = slice(h * QB, (h + 1) * QB)
            z = z_in[h][...] + bias
            m_old = m_all[:, hs]
            m_new = jnp.maximum(m_old, jnp.max(z, axis=0, keepdims=True))
            z_out[h][...] = logits_of(nxt, h)
            alpha = jnp.exp2((m_old - m_new) * c_a)
            p = jnp.exp2((z - m_new) * c_a)
            l_out.append(alpha * l_all[:, hs] + jnp.sum(p, axis=0, keepdims=True))
            acc_refs[h][...] = alpha * acc_refs[h][...] + _dot(vt, p.astype(MXU_DTYPE))
            m_out.append(m_new)
        m_ref[...] = jnp.concatenate(m_out, axis=1)
        l_ref[...] = jnp.concatenate(l_out, axis=1)

    for h in range(H_A):
        za_refs[h][...] = logits_of(0, h)

    def att_body(k2, carry):
        chunk_step(2 * k2, za_refs, zb_refs)

        @pl.when(2 * k2 + 1 < nchunk)
        def _():
            chunk_step(2 * k2 + 1, zb_refs, za_refs)
        return carry
    lax.fori_loop(0, (nchunk + 1) // 2, att_body, 0)

    inv_l = 1.0 / l_ref[...]
    for k in range(H_A):
        ya_t = acc_refs[k][...] * inv_l[:, k * QB:(k + 1) * QB]
        for e, hs in enumerate(halves):
            ya_ref[hs, k * LANES:(k + 1) * LANES] = ya_t[:, e * BLOCK:(e + 1) * BLOCK].T.astype(MXU_DTYPE)

    b_lane = lax.broadcasted_iota(i32, (BLOCK, LANES), 1)
    b_row = lax.broadcasted_iota(i32, (BLOCK, LANES), 0)
    sink = sink_ref[...]
    zero_pad = jnp.zeros((HD_B, LANES), MXU_DTYPE)
    per_g = H_B // KVH_B
    for e, hs in enumerate(halves):
        blk = 2 * i + e
        ip = jnp.maximum(blk - 1, 0)
        kband = jnp.concatenate([kv_ref[rows(ip, BLOCK), 256:384], kv_ref[rows(blk, BLOCK), 256:384]], axis=0)
        ext = []
        for k in range(8):
            t = t128(qb_ref[hs, k * LANES:(k + 1) * LANES])
            for o in range(2):
                piece = t[o * HD_B:(o + 1) * HD_B, :]
                grp = (2 * k + o) // per_g
                ext.append(jnp.concatenate([piece, zero_pad] if grp == 0 else [zero_pad, piece], axis=0))
        s = _dot(kband, jnp.concatenate(ext, axis=1))
        prev_open = jnp.where(blk > 0, 0.0, NEG)
        bias = jnp.concatenate([jnp.where(b_row > b_lane, prev_open, NEG),
                                jnp.where(b_row <= b_lane, 0.0, NEG)], axis=0)
        bias2 = jnp.concatenate([bias, bias], axis=1)
        vbands = [jnp.concatenate([vbt_ref[ip][g * HD_B:(g + 1) * HD_B, :],
                                   vbt_ref[blk][g * HD_B:(g + 1) * HD_B, :]], axis=1) for g in range(KVH_B)]
        for k in range(8):
            cols = slice(2 * k * LANES, (2 * k + 2) * LANES)
            z = s[:, cols] + bias2
            m = jnp.maximum(jnp.max(z, axis=0, keepdims=True), sink[:, cols])
            ex = jnp.exp(z - m)
            inv = 1.0 / (jnp.sum(ex, axis=0, keepdims=True) + jnp.exp(sink[:, cols] - m))
            og = _dot(vbands[2 * k // per_g], ex.astype(MXU_DTYPE)) * inv
            yb_ref[hs, k * LANES:(k + 1) * LANES] = (
                jnp.concatenate([og[:, 0:LANES], og[:, LANES:2 * LANES]], axis=0).T.astype(MXU_DTYPE))


def _attn(qa, qi, qb, wi, kv, ki, sink_row, *, batch, seq):
    n = qa.shape[0]
    nstep = seq // QB
    n_sel = min(K_MAX, seq // 4)
    blk = lambda w: pl.BlockSpec((QB, w), lambda b, i: (b * nstep + i, 0))
    per_b = lambda w: pl.BlockSpec((seq, w), lambda b, i: (b, 0))
    return pl.pallas_call(
        functools.partial(_attn_kernel, n_sel=n_sel),
        grid=(batch, nstep),
        in_specs=[blk(1024), blk(1024), blk(1024), blk(LANES), per_b(512), per_b(LANES),
                  pl.BlockSpec((1, H_B * LANES), lambda b, i: (0, 0))],
        out_specs=[blk(1024), blk(1024)],
        out_shape=[jax.ShapeDtypeStruct((n, 1024), MXU_DTYPE)] * 2,
        scratch_shapes=[pltpu.VMEM((nstep, HD_A, KC), MXU_DTYPE),
                        pltpu.VMEM((seq // BLOCK, LANES, BLOCK), MXU_DTYPE),
                        pltpu.VMEM((HD_A, H_A * QB), MXU_DTYPE),
                        pltpu.VMEM((H_I * D_I, QB), MXU_DTYPE),
                        pltpu.VMEM((seq, QB), i32),
                        pltpu.VMEM((seq, QB), jnp.int16),
                        pltpu.VMEM((seq, QB), jnp.int16),
                        pltpu.VMEM((1, H_A * QB), f32),
                        pltpu.VMEM((1, H_A * QB), f32)]
                       + [pltpu.VMEM((HD_A, QB), f32)] * H_A
                       + [pltpu.VMEM((KC, QB), f32)] * (2 * H_A),

        compiler_params=_params("arbitrary", "arbitrary"),
        name="mixers",
    )(qa, qi, qb, wi, kv, ki, sink_row)


def _out_kernel(h_ref, ya_ref, yb_ref, ga_ref, gb_ref, gt_ref, woa_ref, wob_ref, wout_ref, o_ref, *, tn, tc):
    ya, yb, gate = ya_ref[...], yb_ref[...], gt_ref[0]
    d = o_ref.shape[1]
    for c in range(d // tn):
        sl = slice(c * tn, (c + 1) * tn)
        merged = (ga_ref[:, sl].astype(f32) * _dot(ya, woa_ref[:, sl])
                  + gb_ref[:, sl].astype(f32) * _dot(yb, wob_ref[:, sl])).astype(MXU_DTYPE)
        for e in range(d // tc):
            oc = slice(e * tc, (e + 1) * tc)
            part = gate[:, oc] * _dot(merged, wout_ref[sl, oc])
            if c == 0:
                o_ref[:, oc] = h_ref[:, oc] + part
            else:
                o_ref[:, oc] += part


def _out_proj(h, ya, yb, ga, gb, gate, woa, wob, wout, layer, *, seq, tm, tn, tc):
    n, d = h.shape
    k = ya.shape[1]
    per_b = seq // tm
    resident = lambda r, c: pl.BlockSpec((None, r, c), lambda i: (layer, 0, 0), pipeline_mode=pl.Buffered(1))
    return pl.pallas_call(
        functools.partial(_out_kernel, tn=tn, tc=tc),
        grid=(n // tm,),
        in_specs=[pl.BlockSpec((tm, d), lambda i: (i, 0)),
                  pl.BlockSpec((tm, k), lambda i: (i, 0)),
                  pl.BlockSpec((tm, k), lambda i: (i, 0)),
                  pl.BlockSpec((tm, d), lambda i: (i, 0)),
                  pl.BlockSpec((tm, d), lambda i: (i, 0)),
                  pl.BlockSpec((1, 1, d), lambda i: (i // per_b, 0, 0)),
                  resident(k, d), resident(k, d), resident(d, d)],
        out_specs=pl.BlockSpec((tm, d), lambda i: (i, 0)),
        out_shape=jax.ShapeDtypeStruct((n, d), f32),
        compiler_params=_params("arbitrary"),
        name="out_proj",
    )(h, ya, yb, ga, gb, gate, woa, wob, wout)


def _tiles(seq, d_ff):
    tm = min(seq, 512)
    tm_ffn = min(seq, 1024)
    tf = 512 if d_ff % 512 == 0 else 256
    return tm, tm_ffn, tf


def _split_w_in(w_in, d):
    sizes = (H_A * HD_A, HD_A, HD_A, H_I * D_I, D_I, H_I, H_B * HD_B, KVH_B * HD_B, KVH_B * HD_B, d, d)
    offs = [0]
    for s in sizes:
        offs.append(offs[-1] + s)
    seg = lambda k: w_in[:, :, offs[k]:offs[k + 1]]
    pad = lambda a, w: jnp.pad(a, ((0, 0), (0, 0), (0, w - a.shape[-1])))
    q = jnp.concatenate([seg(0), seg(3), seg(6)], axis=-1)
    gates = jnp.concatenate([seg(9), seg(10)], axis=-1)
    sel = jnp.concatenate([seg(1), seg(2), seg(7), seg(8), pad(seg(4), LANES), pad(seg(5), LANES)], axis=-1)
    return q.astype(MXU_DTYPE), gates.astype(MXU_DTYPE), sel.astype(MXU_DTYPE)


def kernel(x, c, positions, ada_w, ada_b, norm_ffn1_g, ffn1_w_gate, ffn1_w_up, ffn1_w_down, norm_mix_g, w_in, qn_a_g, kn_a_g, qn_b_g, kn_b_g, sinks, w_o_a, w_o_b, w_out, norm_ffn2_g, ffn2_w_gate, ffn2_w_up, ffn2_w_down):
    batch, seq, d = x.shape
    depth = ada_w.shape[0]
    d_ff = ffn1_w_gate.shape[-1]
    n = batch * seq
    assert seq % QB == 0 and d % 1024 == 0
    tm, tm_ffn, tf = _tiles(seq, d_ff)
    tn = min(d, 512)

    mod = _ada_mod(c, ada_w, ada_b).reshape(depth, batch, N_MOD, 1, d)
    rope = _rope_tables(positions)
    wq, wgate, wsel = _split_w_in(w_in, d)
    cast = lambda w: w.astype(MXU_DTYPE)
    w1 = (cast(ffn1_w_gate), cast(ffn1_w_up), cast(ffn1_w_down))
    w2 = (cast(ffn2_w_gate), cast(ffn2_w_up), cast(ffn2_w_down))
    woa, wob, wout = cast(w_o_a), cast(w_o_b), cast(w_out)
    gains = jnp.stack([qn_a_g, kn_a_g, jnp.tile(qn_b_g, (1, LANES // HD_B)),
                       jnp.tile(kn_b_g, (1, LANES // HD_B))], axis=1)
    sink_rows = jnp.repeat(sinks, LANES, axis=1).reshape(depth, 1, H_B * LANES)

    h = x.reshape(n, d)
    for l in range(depth):
        m = [mod[l, :, k] for k in range(N_MOD)]
        h = _ffn(h, m[0], m[1], m[2], norm_ffn1_g[l][None], *w1, l, seq=seq, tm=tm_ffn, rows=tm, tf=tf, tn=tn)
        qa, qi, qb, ga, gb, kv, ki, wi = _proj(h, m[3], m[4], norm_mix_g[l][None], wq, wgate, wsel,
                                               gains[l], rope, l, seq=seq, tm=tm)
        ya, yb = _attn(qa, qi, qb, wi, kv, ki, sink_rows[l], batch=batch, seq=seq)
        h = _out_proj(h, ya, yb, ga, gb, m[5], woa, wob, wout, l, seq=seq, tm=tm, tn=2 * tn, tc=tn)
        h = _ffn(h, m[6], m[7], m[8], norm_ffn2_g[l][None], *w2, l, seq=seq, tm=tm_ffn, rows=tm, tf=tf, tn=tn)
    return h.reshape(batch, seq, d)
```

```python
import functools

import jax
import jax.numpy as jnp
from jax import lax
from jax.experimental import pallas as pl
from jax.experimental.pallas import tpu as pltpu

H_A, HD_A = 8, 128
H_I, D_I, D_I_ROPE = 16, 64, 32
K_MAX = 256
H_B, HD_B, KVH_B = 16, 64, 2
BLOCK = 128
ROPE_THETA = 10000.0
EPS = 1e-6
NEG = -1e30
N_MOD = 9

LANES = 128
V7X_VMEM_LIMIT_BYTES = 60 * 1024 * 1024
MXU_DTYPE = jnp.bfloat16

INT_MIN = -2147483648
M_INIT = -1e29
LOG2E = 1.4426950408889634
SEL_COLS = 768
N_BIG = 3 * 1024

f32 = jnp.float32
i32 = jnp.int32


def _sigmoid(x):
    return 0.5 * jnp.tanh(0.5 * x) + 0.5


def _dot(a, b):
    return jnp.dot(a, b, preferred_element_type=f32)


def _params(*sem):
    return pltpu.CompilerParams(dimension_semantics=sem, vmem_limit_bytes=V7X_VMEM_LIMIT_BYTES)


def _ada_kernel(c_ref, w_ref, b_ref, o_ref):
    c = c_ref[...]
    act = (c * jax.nn.sigmoid(c)).astype(MXU_DTYPE)
    o_ref[0] = _dot(act, w_ref[0].astype(MXU_DTYPE)) + b_ref[0]


def _ada_mod(c, ada_w, ada_b):
    L, D, N = ada_w.shape
    B = c.shape[0]
    tn = min(N, 1024)
    return pl.pallas_call(
        _ada_kernel,
        grid=(L, N // tn),
        in_specs=[pl.BlockSpec((B, D), lambda l, j: (0, 0)),
                  pl.BlockSpec((1, D, tn), lambda l, j: (l, 0, j)),
                  pl.BlockSpec((1, 1, tn), lambda l, j: (l, 0, j))],
        out_specs=pl.BlockSpec((1, B, tn), lambda l, j: (l, 0, j)),
        out_shape=jax.ShapeDtypeStruct((L, B, N), f32),
        compiler_params=_params("arbitrary", "arbitrary"),
        name="ada_mod",
    )(c, ada_w, ada_b.reshape(L, 1, N))


def _rope_kernel(pos_ref, freq_ref, sign_ref, o_ref):
    ang = pos_ref[...].astype(f32) * freq_ref[...]
    lane = lax.broadcasted_iota(i32, ang.shape, 1)

    def rep(v, n):
        return jnp.where(lane < n, v, pltpu.roll(v, n, 1))

    for t, (trig, fill) in enumerate(((jnp.cos(ang), 1.0), (jnp.sin(ang), 0.0))):
        a = rep(trig, HD_A // 2)
        b = rep(rep(pltpu.roll(trig, LANES - HD_A // 2, 1), HD_B // 2), HD_B)
        i_ = rep(pltpu.roll(trig, LANES - HD_A // 2 - HD_B // 2, 1), D_I_ROPE // 2)
        i_ = rep(jnp.where(lane < D_I_ROPE, i_, fill), D_I)
        for k, tab in enumerate((a, b, i_)):
            if t:
                tab = tab * sign_ref[k:k + 1, :]
            o_ref[:, (2 * k + t) * LANES:(2 * k + t + 1) * LANES] = tab


def _rope_rows():
    lane = jnp.arange(LANES)

    def inv_freq(dim):
        return ROPE_THETA ** (-jnp.arange(0, dim, 2, dtype=f32) / dim)

    freq = jnp.concatenate([inv_freq(HD_A), inv_freq(HD_B), inv_freq(D_I_ROPE),
                            jnp.zeros((LANES - (HD_A + HD_B + D_I_ROPE) // 2,), f32)])[None]
    sa = jnp.where(lane % HD_A < HD_A // 2, -1.0, 1.0)
    sb = jnp.where(lane % HD_B < HD_B // 2, -1.0, 1.0)
    si = jnp.where(lane % D_I < D_I_ROPE // 2, -1.0, 1.0)
    return freq.astype(f32), jnp.stack([sa, sb, si]).astype(f32)


def _rope_tables(positions):
    n = positions.size
    tm = min(n, 1024)
    freq, sign = _rope_rows()
    return pl.pallas_call(
        _rope_kernel,
        grid=(n // tm,),
        in_specs=[pl.BlockSpec((tm, 1), lambda i: (i, 0)),
                  pl.BlockSpec((1, LANES), lambda i: (0, 0)),
                  pl.BlockSpec((3, LANES), lambda i: (0, 0))],
        out_specs=pl.BlockSpec((tm, 6 * LANES), lambda i: (i, 0)),
        out_shape=jax.ShapeDtypeStruct((n, 6 * LANES), f32),
        compiler_params=_params("arbitrary"),
        name="rope_tables",
    )(positions.reshape(n, 1), freq, sign)


NORM_SLAB = 16


def _norm_modulate_to(u_ref, h_ref, g, shift, scale, copy_ref=None):
    geff = g * (1.0 + scale)
    for s in range(h_ref.shape[0] // NORM_SLAB):
        rs = slice(s * NORM_SLAB, (s + 1) * NORM_SLAB)
        x = h_ref[rs, :]
        ms = jnp.mean(x * x, axis=-1, keepdims=True)
        u_ref[rs, :] = (x * lax.rsqrt(ms + EPS) * geff + shift).astype(u_ref.dtype)
        if copy_ref is not None:
            copy_ref[rs, :] = x


def _head_norm(x, gain, lane, head_dim):
    sq = x * x
    if head_dim == LANES:
        ms = jnp.mean(sq, axis=-1, keepdims=True)
    else:
        lo = lane < head_dim
        s_lo = jnp.sum(jnp.where(lo, sq, 0.0), axis=-1, keepdims=True)
        s_hi = jnp.sum(jnp.where(lo, 0.0, sq), axis=-1, keepdims=True)
        ms = jnp.where(lo, s_lo, s_hi) * (1.0 / head_dim)
    return x * lax.rsqrt(ms + EPS) * gain


def _rotate(y, cos, sin, lane, half, period):
    if 2 * half == LANES:
        partner = pltpu.roll(y, half, 1)
    else:
        fwd = pltpu.roll(y, LANES - half, 1)
        bwd = pltpu.roll(y, half, 1)
        partner = jnp.where(lane % period < half, fwd, bwd)
    return y * cos + partner * sin


def _ffn_kernel(h_ref, sh_ref, sc_ref, gt_ref, g_ref, wg_ref, wu_ref, wd_ref, o_ref, u_ref, *, rows, tn):
    j = pl.program_id(1)

    @pl.when(j == 0)
    def _():
        _norm_modulate_to(u_ref, h_ref, g_ref[...], sh_ref[0], sc_ref[0], copy_ref=o_ref)

    half_gate = 0.5 * gt_ref[0]
    for r in range(u_ref.shape[0] // rows):
        rs = slice(r * rows, (r + 1) * rows)
        u = u_ref[rs, :]
        a = _dot(u, wg_ref[...])
        b = _dot(u, wu_ref[...])
        act = (a * _sigmoid(a) * b).astype(MXU_DTYPE)
        for c in range(o_ref.shape[1] // tn):
            sl = slice(c * tn, (c + 1) * tn)
            o_ref[rs, sl] += half_gate[:, sl] * _dot(act, wd_ref[:, sl])


def _ffn(h, shift, scale, gate, g, wg, wu, wd, layer, *, seq, tm, rows, tf, tn):
    n, d = h.shape
    f = wg.shape[-1]
    per_b = seq // tm
    row = pl.BlockSpec((1, 1, d), lambda i, j: (i // per_b, 0, 0))
    return pl.pallas_call(
        functools.partial(_ffn_kernel, rows=rows, tn=tn),
        grid=(n // tm, f // tf),
        in_specs=[pl.BlockSpec((tm, d), lambda i, j: (jnp.minimum(i + jnp.where(j > 0, 1, 0), n // tm - 1), 0)),
                  row, row, row,
                  pl.BlockSpec((1, d), lambda i, j: (0, 0)),
                  pl.BlockSpec((None, d, tf), lambda i, j: (layer, 0, j)),
                  pl.BlockSpec((None, d, tf), lambda i, j: (layer, 0, j)),
                  pl.BlockSpec((None, tf, d), lambda i, j: (layer, j, 0))],
        out_specs=pl.BlockSpec((tm, d), lambda i, j: (i, 0)),
        out_shape=jax.ShapeDtypeStruct((n, d), f32),
        scratch_shapes=[pltpu.VMEM((tm, d), MXU_DTYPE)],
        compiler_params=_params("arbitrary", "arbitrary"),
        name="ffn",
    )(h, shift, scale, gate, g, wg, wu, wd)


def _proj_kernel(h_ref, sh_ref, sc_ref, g_ref, wq_ref, wgate_lo_ref, wgate_hi_ref, wsel_ref, gains_ref, rope_ref,
                 qa_ref, qi_ref, qb_ref, ga_ref, gb_ref, kv_ref, ki_ref, wi_ref, u_ref, ra_ref, rb_ref):
    j = pl.program_id(1)
    tm = h_ref.shape[0]
    tn = ra_ref.shape[1]
    lane = lax.broadcasted_iota(i32, (tm, LANES), 1)

    def table(k):
        return rope_ref[:, k * LANES:(k + 1) * LANES]

    def gain(k):
        return gains_ref[k:k + 1, :]

    def rope_a(y):
        return _rotate(y, table(0), table(1), lane, HD_A // 2, HD_A)

    def rope_b(y):
        return _rotate(y, table(2), table(3), lane, HD_B // 2, HD_B)

    def rope_i(y):
        return _rotate(y, table(4), table(5), lane, D_I_ROPE // 2, D_I)

    def q_dot(c):
        return _dot(u_ref[...], wq_ref[:, c * tn:(c + 1) * tn])

    def gate_dot():
        u = u_ref[...]
        return jnp.concatenate([_dot(u, wgate_lo_ref[...]), _dot(u, wgate_hi_ref[...])], axis=1)

    @pl.when(j == 0)
    def _():
        _norm_modulate_to(u_ref, h_ref, g_ref[...], sh_ref[0], sc_ref[0])
        sel = _dot(u_ref[...], wsel_ref[...])
        ra_ref[...] = q_dot(0)
        ka = rope_a(_head_norm(sel[:, 0:128], gain(1), lane, HD_A))
        kb = rope_b(_head_norm(sel[:, 256:384], gain(3), lane, HD_B))
        kv_ref[:, 0:128] = ka.astype(MXU_DTYPE)
        kv_ref[:, 128:256] = sel[:, 128:256].astype(MXU_DTYPE)
        kv_ref[:, 256:384] = kb.astype(MXU_DTYPE)
        kv_ref[:, 384:512] = sel[:, 384:512].astype(MXU_DTYPE)
        ki_ref[...] = rope_i(sel[:, 512:640]).astype(MXU_DTYPE)
        wi_ref[...] = sel[:, 640:768] * (H_I ** -0.5 * D_I ** -0.5)

    @pl.when(j == 1)
    def _():
        rb_ref[...] = q_dot(1)
        for k in range(8):
            y = _head_norm(ra_ref[:, k * LANES:(k + 1) * LANES], gain(0), lane, HD_A)
            qa_ref[:, k * LANES:(k + 1) * LANES] = rope_a(y).astype(MXU_DTYPE)

    @pl.when(j == 2)
    def _():
        ra_ref[...] = q_dot(2)
        for k in range(8):
            qi_ref[:, k * LANES:(k + 1) * LANES] = rope_i(rb_ref[:, k * LANES:(k + 1) * LANES]).astype(MXU_DTYPE)

    @pl.when(j == 3)
    def _():
        rb_ref[...] = gate_dot()
        for k in range(8):
            y = _head_norm(ra_ref[:, k * LANES:(k + 1) * LANES], gain(2), lane, HD_B)
            qb_ref[:, k * LANES:(k + 1) * LANES] = (rope_b(y) * (HD_B ** -0.5)).astype(MXU_DTYPE)

    per_gate = ga_ref.shape[1] // tn

    def put_gate(c, res):
        ref = ga_ref if c < per_gate else gb_ref
        c = c % per_gate
        ref[:, c * tn:(c + 1) * tn] = _sigmoid(res).astype(MXU_DTYPE)

    @pl.when(j == 4)
    def _():
        put_gate(1, gate_dot())
        put_gate(0, rb_ref[...])

    for c in range(2, 2 * per_gate):
        @pl.when(j == 3 + c)
        def _(c=c):
            put_gate(c, gate_dot())


def _proj(h, shift, scale, g, wq, wgate, wsel, gains, rope, layer, *, seq, tm):
    n, d = h.shape
    d_gate = wgate.shape[-1] // 2
    per_b = seq // tm
    tn = 1024
    n_q = N_BIG // tn
    n_gate = 2 * d_gate // tn
    nt = n // tm
    row = pl.BlockSpec((1, 1, d), lambda i, j: (i // per_b, 0, 0))
    tile = lambda w: pl.BlockSpec((tm, w), lambda i, j: (i, 0))

    def tile_until(w, last_use):
        return pl.BlockSpec((tm, w), lambda i, j: (jnp.minimum(i + jnp.where(j > last_use, 1, 0), nt - 1), 0))

    def gate_chunk(half):
        return pl.BlockSpec((None, d, tn // 2), lambda i, j: (
            layer, 0, 2 * jnp.where(j < n_q, n_gate - 1, j - n_q) + half))

    resident = lambda w: pl.BlockSpec((None, d, w), lambda i, j: (layer, 0, 0), pipeline_mode=pl.Buffered(1))
    bf = lambda w: jax.ShapeDtypeStruct((n, w), MXU_DTYPE)
    return pl.pallas_call(
        _proj_kernel,
        grid=(nt, n_q + n_gate),
        in_specs=[tile_until(d, 0), row, row,
                  pl.BlockSpec((1, d), lambda i, j: (0, 0)),
                  resident(N_BIG), gate_chunk(0), gate_chunk(1),
                  resident(SEL_COLS),
                  pl.BlockSpec((4, LANES), lambda i, j: (0, 0)),
                  tile_until(6 * LANES, n_q)],
        out_specs=[tile(1024), tile(1024), tile(1024), tile(d_gate), tile(d_gate),
                   tile(512), tile(LANES), tile(LANES)],
        out_shape=[bf(1024), bf(1024), bf(1024), bf(d_gate), bf(d_gate), bf(512), bf(LANES),
                   jax.ShapeDtypeStruct((n, LANES), f32)],
        scratch_shapes=[pltpu.VMEM((tm, d), MXU_DTYPE),
                        pltpu.VMEM((tm, tn), f32), pltpu.VMEM((tm, tn), f32)],
        compiler_params=_params("arbitrary", "arbitrary"),
        name="mix_proj",
    )(h, shift, scale, g, wq, wgate, wgate, wsel, gains, rope)


QB = 256
KC = 256
I16_MIN = -32768
PACK = 16


def _attn_kernel(qa_ref, qi_ref, qb_ref, wi_ref, kv_ref, ki_ref, sink_ref,
                 ya_ref, yb_ref,
                 vat_ref, vbt_ref, qat_ref, qit_ref, key_ref, hi_ref, lo_ref, m_ref, l_ref, *head_refs,
                 n_sel):
    acc_refs, za_refs, zb_refs = head_refs[:H_A], head_refs[H_A:2 * H_A], head_refs[2 * H_A:]
    i = pl.program_id(1)
    nstep = pl.num_programs(1)
    nchunk = i + 1
    halves = (slice(0, BLOCK), slice(BLOCK, QB))

    def rows(c, size=KC):
        return pl.ds(pl.multiple_of(c * size, size), size)

    def t128(x):
        return x.astype(f32).T.astype(MXU_DTYPE)

    @pl.when(i == 0)
    def _():
        def body(c, carry):
            b0, b1 = rows(2 * c, BLOCK), rows(2 * c + 1, BLOCK)
            vat_ref[c] = jnp.concatenate([t128(kv_ref[b0, 128:256]), t128(kv_ref[b1, 128:256])], axis=1)
            vbt_ref[2 * c] = t128(kv_ref[b0, 384:512])
            vbt_ref[2 * c + 1] = t128(kv_ref[b1, 384:512])
            return carry
        lax.fori_loop(0, nstep, body, 0)

    for k in range(8):
        sl = slice(k * LANES, (k + 1) * LANES)
        for e, hs in enumerate(halves):
            qat_ref[:, k * QB + e * BLOCK:k * QB + (e + 1) * BLOCK] = t128(qa_ref[hs, sl])
            qit_ref[sl, hs] = t128(qi_ref[hs, sl])
    w_t = jnp.concatenate([wi_ref[hs, :].T for hs in halves], axis=1)

    q_lane = lax.broadcasted_iota(i32, (BLOCK, QB), 1)
    k_row = lax.broadcasted_iota(i32, (BLOCK, QB), 0)
    q_pos = i * QB + q_lane

    def idx_body(c, carry):
        for e in range(KC // BLOCK):
            cc = (KC // BLOCK) * c + e
            r = rows(cc, BLOCK)
            kc = ki_ref[r, 0:D_I]
            acc = jnp.zeros((BLOCK, QB), f32)
            for h in range(H_I):
                d = _dot(kc, qit_ref[h * D_I:(h + 1) * D_I, :])
                acc = acc + w_t[h:h + 1, :] * jnp.maximum(d, 0.0)
            bits = lax.bitcast_convert_type(acc + 0.0, i32)
            key = jnp.where(bits >= 0, bits, bits ^ 0x7FFFFFFF)
            key = jnp.where(cc * BLOCK + k_row <= q_pos, key, INT_MIN)
            key_ref[r, :] = key
            hi_ref[r, :] = (key >> 16).astype(jnp.int16)
            lo_ref[r, :] = ((key & 0xFFFF) + I16_MIN).astype(jnp.int16)
        return carry
    lax.fori_loop(0, nchunk, idx_body, 0)

    need = jnp.minimum(n_sel, q_pos[0:1, :] + 1).astype(f32)
    one, zero = jnp.ones((), jnp.bfloat16), jnp.zeros((), jnp.bfloat16)

    n_total = key_ref.shape[0] // KC
    n_half = max(n_total // 2, 1)
    span = jnp.where(nchunk <= n_half, n_half, n_total)

    def pad_body(c, carry):
        hi_ref[rows(c), :] = jnp.full((KC, QB), I16_MIN, jnp.int16)
        lo_ref[rows(c), :] = jnp.full((KC, QB), I16_MIN, jnp.int16)
        return carry
    lax.fori_loop(nchunk, span, pad_body, 0)

    def search(n_chunks):
        def count_ge(ref, cand):
            cnt = None
            for c in range(n_chunks):
                ind = jnp.where(ref[c * KC:(c + 1) * KC, :] >= cand, one, zero)
                parts = [ind[g * PACK:(g + 1) * PACK] for g in range(KC // PACK)]
                while len(parts) > 1:
                    parts = [a + b for a, b in zip(parts[0::2], parts[1::2])]
                cnt = parts[0] if cnt is None else cnt + parts[0]
            return jnp.sum(cnt.astype(f32), axis=0, keepdims=True)

        def bisect(ref, want):
            def body(k, t_u):
                cand_u = t_u | lax.shift_left(jnp.int32(1), 15 - k)
                ok = count_ge(ref, (cand_u + I16_MIN).astype(jnp.int16)) >= want
                return jnp.where(ok, cand_u, t_u)
            return lax.fori_loop(0, 16, body, jnp.zeros((1, QB), i32))

        hi_u = bisect(hi_ref, need)
        hi_t = (hi_u + I16_MIN).astype(jnp.int16)
        above = jnp.where(hi_u == 0xFFFF, 0.0,
                          count_ge(hi_ref, (jnp.minimum(hi_u + 1, 0xFFFF) + I16_MIN).astype(jnp.int16)))
        for c in range(n_chunks):
            cs = slice(c * KC, (c + 1) * KC)
            lo_ref[cs, :] = jnp.where(hi_ref[cs, :] == hi_t, lo_ref[cs, :], jnp.int16(I16_MIN))
        lo_u = bisect(lo_ref, need - above)
        return ((hi_u + I16_MIN) * 65536 + lo_u,
                above + count_ge(lo_ref, (lo_u + I16_MIN).astype(jnp.int16)))

    if n_half == n_total:
        thr, picked = search(n_total)
    else:
        thr, picked = lax.cond(nchunk <= n_half, lambda: search(n_half), lambda: search(n_total))


    @pl.when(jnp.max(picked - need) > 0.0)
    def _():
        def gt_body(c, cnt):
            return cnt + jnp.sum(jnp.where(key_ref[rows(c), :] > thr, 1.0, 0.0), axis=0, keepdims=True)
        allowed = need - lax.fori_loop(0, nchunk, gt_body, jnp.zeros((1, QB), f32))
        tri = (lax.broadcasted_iota(i32, (KC, KC), 0) >= lax.broadcasted_iota(i32, (KC, KC), 1))
        tri = jnp.where(tri, 1.0, 0.0).astype(jnp.bfloat16)

        def fix_body(c, seen):
            k = key_ref[rows(c), :]
            tie = jnp.where(k == thr, 1.0, 0.0)
            rank = _dot(tri, tie.astype(jnp.bfloat16)) + seen
            key_ref[rows(c), :] = jnp.where((tie > 0.0) & (rank > allowed), thr - 1, k)
            return seen + jnp.sum(tie, axis=0, keepdims=True)
        lax.fori_loop(0, nchunk, fix_body, jnp.zeros((1, QB), f32))

    m_ref[...] = jnp.full(m_ref.shape, M_INIT, f32)
    l_ref[...] = jnp.zeros(l_ref.shape, f32)
    for acc_ref in acc_refs:
        acc_ref[...] = jnp.zeros(acc_ref.shape, f32)
    c_a = HD_A ** -0.5 * LOG2E

    def logits_of(c, h):
        return _dot(kv_ref[rows(c), 0:128], qat_ref[:, h * QB:(h + 1) * QB])

    def chunk_step(c, z_in, z_out):
        nxt = jnp.minimum(c + 1, nchunk - 1)
        vt = vat_ref[c]
        bias = jnp.where(key_ref[rows(c), :] >= thr, 0.0, NEG)
        m_all, l_all = m_ref[...], l_ref[...]
        m_out, l_out = [], []
        for h in range(H_A):
            hs = slice(h * QB, (h + 1) * QB)
            z = z_in[h][...] + bias
            m_old = m_all[:, hs]
            m_new = jnp.maximum(m_old, jnp.max(z, axis=0, keepdims=True))
            z_out[h][...] = logits_of(nxt, h)
            alpha = jnp.exp2((m_old - m_new) * c_a)
            p = jnp.exp2((z - m_new) * c_a)
            l_out.append(alpha * l_all[:, hs] + jnp.sum(p, axis=0, keepdims=True))
            acc_refs[h][...] = alpha * acc_refs[h][...] + _dot(vt, p.astype(MXU_DTYPE))
            m_out.append(m_new)
        m_ref[...] = jnp.concatenate(m_out, axis=1)
        l_ref[...] = jnp.concatenate(l_out, axis=1)

    for h in range(H_A):
        za_refs[h][...] = logits_of(0, h)

    def att_body(k2, carry):
        chunk_step(2 * k2, za_refs, zb_refs)

        @pl.when(2 * k2 + 1 < nchunk)
        def _():
            chunk_step(2 * k2 + 1, zb_refs, za_refs)
        return carry
    lax.fori_loop(0, (nchunk + 1) // 2, att_body, 0)

    inv_l = 1.0 / l_ref[...]
    for k in range(H_A):
        ya_t = acc_refs[k][...] * inv_l[:, k * QB:(k + 1) * QB]
        for e, hs in enumerate(halves):
            ya_ref[hs, k * LANES:(k + 1) * LANES] = ya_t[:, e * BLOCK:(e + 1) * BLOCK].T.astype(MXU_DTYPE)

    b_lane = lax.broadcasted_iota(i32, (BLOCK, LANES), 1)
    b_row = lax.broadcasted_iota(i32, (BLOCK, LANES), 0)
    sink = sink_ref[...]
    zero_pad = jnp.zeros((HD_B, LANES), MXU_DTYPE)
    per_g = H_B // KVH_B
    for e, hs in enumerate(halves):
        blk = 2 * i + e
        ip = jnp.maximum(blk - 1, 0)
        kband = jnp.concatenate([kv_ref[rows(ip, BLOCK), 256:384], kv_ref[rows(blk, BLOCK), 256:384]], axis=0)
        ext = []
        for k in range(8):
            t = t128(qb_ref[hs, k * LANES:(k + 1) * LANES])
            for o in range(2):
                piece = t[o * HD_B:(o + 1) * HD_B, :]
                grp = (2 * k + o) // per_g
                ext.append(jnp.concatenate([piece, zero_pad] if grp == 0 else [zero_pad, piece], axis=0))
        s = _dot(kband, jnp.concatenate(ext, axis=1))
        prev_open = jnp.where(blk > 0, 0.0, NEG)
        bias = jnp.concatenate([jnp.where(b_row > b_lane, prev_open, NEG),
                                jnp.where(b_row <= b_lane, 0.0, NEG)], axis=0)
        bias2 = jnp.concatenate([bias, bias], axis=1)
        vbands = [jnp.concatenate([vbt_ref[ip][g * HD_B:(g + 1) * HD_B, :],
                                   vbt_ref[blk][g * HD_B:(g + 1) * HD_B, :]], axis=1) for g in range(KVH_B)]
        for k in range(8):
            cols = slice(2 * k * LANES, (2 * k + 2) * LANES)
            z = s[:, cols] + bias2
            m = jnp.maximum(jnp.max(z, axis=0, keepdims=True), sink[:, cols])
            ex = jnp.exp(z - m)
            inv = 1.0 / (jnp.sum(ex, axis=0, keepdims=True) + jnp.exp(sink[:, cols] - m))
            og = _dot(vbands[2 * k // per_g], ex.astype(MXU_DTYPE)) * inv
            yb_ref[hs, k * LANES:(k + 1) * LANES] = (
                jnp.concatenate([og[:, 0:LANES], og[:, LANES:2 * LANES]], axis=0).T.astype(MXU_DTYPE))


def _attn(qa, qi, qb, wi, kv, ki, sink_row, *, batch, seq):
    n = qa.shape[0]
    nstep = seq // QB
    n_sel = min(K_MAX, seq // 4)
    blk = lambda w: pl.BlockSpec((QB, w), lambda b, i: (b * nstep + i, 0))
    per_b = lambda w: pl.BlockSpec((seq, w), lambda b, i: (b, 0))
    return pl.pallas_call(
        functools.partial(_attn_kernel, n_sel=n_sel),
        grid=(batch, nstep),
        in_specs=[blk(1024), blk(1024), blk(1024), blk(LANES), per_b(512), per_b(LANES),
                  pl.BlockSpec((1, H_B * LANES), lambda b, i: (0, 0))],
        out_specs=[blk(1024), blk(1024)],
        out_shape=[jax.ShapeDtypeStruct((n, 1024), MXU_DTYPE)] * 2,
        scratch_shapes=[pltpu.VMEM((nstep, HD_A, KC), MXU_DTYPE),
                        pltpu.VMEM((seq // BLOCK, LANES, BLOCK), MXU_DTYPE),
                        pltpu.VMEM((HD_A, H_A * QB), MXU_DTYPE),
                        pltpu.VMEM((H_I * D_I, QB), MXU_DTYPE),
                        pltpu.VMEM((seq, QB), i32),
                        pltpu.VMEM((seq, QB), jnp.int16),
                        pltpu.VMEM((seq, QB), jnp.int16),
                        pltpu.VMEM((1, H_A * QB), f32),
                        pltpu.VMEM((1, H_A * QB), f32)]
                       + [pltpu.VMEM((HD_A, QB), f32)] * H_A
                       + [pltpu.VMEM((KC, QB), f32)] * (2 * H_A),

        compiler_params=_params("arbitrary", "arbitrary"),
        name="mixers",
    )(qa, qi, qb, wi, kv, ki, sink_row)


def _out_kernel(h_ref, ya_ref, yb_ref, ga_ref, gb_ref, gt_ref, woa_ref, wob_ref, wout_ref, o_ref, *, tn, tc):
    ya, yb, gate = ya_ref[...], yb_ref[...], gt_ref[0]
    d = o_ref.shape[1]
    for c in range(d // tn):
        sl = slice(c * tn, (c + 1) * tn)
        merged = (ga_ref[:, sl].astype(f32) * _dot(ya, woa_ref[:, sl])
                  + gb_ref[:, sl].astype(f32) * _dot(yb, wob_ref[:, sl])).astype(MXU_DTYPE)
        for e in range(d // tc):
            oc = slice(e * tc, (e + 1) * tc)
            part = gate[:, oc] * _dot(merged, wout_ref[sl, oc])
            if c == 0:
                o_ref[:, oc] = h_ref[:, oc] + part
            else:
                o_ref[:, oc] += part


def _out_proj(h, ya, yb, ga, gb, gate, woa, wob, wout, layer, *, seq, tm, tn, tc):
    n, d = h.shape
    k = ya.shape[1]
    per_b = seq // tm
    resident = lambda r, c: pl.BlockSpec((None, r, c), lambda i: (layer, 0, 0), pipeline_mode=pl.Buffered(1))
    return pl.pallas_call(
        functools.partial(_out_kernel, tn=tn, tc=tc),
        grid=(n // tm,),
        in_specs=[pl.BlockSpec((tm, d), lambda i: (i, 0)),
                  pl.BlockSpec((tm, k), lambda i: (i, 0)),
                  pl.BlockSpec((tm, k), lambda i: (i, 0)),
                  pl.BlockSpec((tm, d), lambda i: (i, 0)),
                  pl.BlockSpec((tm, d), lambda i: (i, 0)),
                  pl.BlockSpec((1, 1, d), lambda i: (i // per_b, 0, 0)),
                  resident(k, d), resident(k, d), resident(d, d)],
        out_specs=pl.BlockSpec((tm, d), lambda i: (i, 0)),
        out_shape=jax.ShapeDtypeStruct((n, d), f32),
        compiler_params=_params("arbitrary"),
        name="out_proj",
    )(h, ya, yb, ga, gb, gate, woa, wob, wout)


def _tiles(seq, d_ff):
    tm = min(seq, 512)
    tm_ffn = min(seq, 1024)
    tf = 512 if d_ff % 512 == 0 else 256
    return tm, tm_ffn, tf


def _split_w_in(w_in, d):
    sizes = (H_A * HD_A, HD_A, HD_A, H_I * D_I, D_I, H_I, H_B * HD_B, KVH_B * HD_B, KVH_B * HD_B, d, d)
    offs = [0]
    for s in sizes:
        offs.append(offs[-1] + s)
    seg = lambda k: w_in[:, :, offs[k]:offs[k + 1]]
    pad = lambda a, w: jnp.pad(a, ((0, 0), (0, 0), (0, w - a.shape[-1])))
    q = jnp.concatenate([seg(0), seg(3), seg(6)], axis=-1)
    gates = jnp.concatenate([seg(9), seg(10)], axis=-1)
    sel = jnp.concatenate([seg(1), seg(2), seg(7), seg(8), pad(seg(4), LANES), pad(seg(5), LANES)], axis=-1)
    return q.astype(MXU_DTYPE), gates.astype(MXU_DTYPE), sel.astype(MXU_DTYPE)


def kernel(x, c, positions, ada_w, ada_b, norm_ffn1_g, ffn1_w_gate, ffn1_w_up, ffn1_w_down, norm_mix_g, w_in, qn_a_g, kn_a_g, qn_b_g, kn_b_g, sinks, w_o_a, w_o_b, w_out, norm_ffn2_g, ffn2_w_gate, ffn2_w_up, ffn2_w_down):
    batch, seq, d = x.shape
    depth = ada_w.shape[0]
    d_ff = ffn1_w_gate.shape[-1]
    n = batch * seq
    assert seq % QB == 0 and d % 1024 == 0
    tm, tm_ffn, tf = _tiles(seq, d_ff)
    tn = min(d, 512)

    mod = _ada_mod(c, ada_w, ada_b).reshape(depth, batch, N_MOD, 1, d)
    rope = _rope_tables(positions)
    wq, wgate, wsel = _split_w_in(w_in, d)
    cast = lambda w: w.astype(MXU_DTYPE)
    w1 = (cast(ffn1_w_gate), cast(ffn1_w_up), cast(ffn1_w_down))
    w2 = (cast(ffn2_w_gate), cast(ffn2_w_up), cast(ffn2_w_down))
    woa, wob, wout = cast(w_o_a), cast(w_o_b), cast(w_out)
    gains = jnp.stack([qn_a_g, kn_a_g, jnp.tile(qn_b_g, (1, LANES // HD_B)),
                       jnp.tile(kn_b_g, (1, LANES // HD_B))], axis=1)
    sink_rows = jnp.repeat(sinks, LANES, axis=1).reshape(depth, 1, H_B * LANES)

    h = x.reshape(n, d)
    for l in range(depth):
        m = [mod[l, :, k] for k in range(N_MOD)]
        h = _ffn(h, m[0], m[1], m[2], norm_ffn1_g[l][None], *w1, l, seq=seq, tm=tm_ffn, rows=tm, tf=tf, tn=tn)
        qa, qi, qb, ga, gb, kv, ki, wi = _proj(h, m[3], m[4], norm_mix_g[l][None], wq, wgate, wsel,
                                               gains[l], rope, l, seq=seq, tm=tm)
        ya, yb = _attn(qa, qi, qb, wi, kv, ki, sink_rows[l], batch=batch, seq=seq)
        h = _out_proj(h, ya, yb, ga, gb, m[5], woa, wob, wout, l, seq=seq, tm=tm, tn=2 * tn, tc=tn)
        h = _ffn(h, m[6], m[7], m[8], norm_ffn2_g[l][None], *w2, l, seq=seq, tm=tm_ffn, rows=tm, tf=tf, tn=tn)
    return h.reshape(batch, seq, d)
```

```python
import functools

import jax
import jax.numpy as jnp
from jax import lax
from jax.experimental import pallas as pl
from jax.experimental.pallas import tpu as pltpu

H_A, HD_A = 8, 128
H_I, D_I, D_I_ROPE = 16, 64, 32
K_MAX = 256
H_B, HD_B, KVH_B = 16, 64, 2
BLOCK = 128
ROPE_THETA = 10000.0
EPS = 1e-6
NEG = -1e30
N_MOD = 9

LANES = 128
V7X_VMEM_LIMIT_BYTES = 60 * 1024 * 1024
MXU_DTYPE = jnp.bfloat16

INT_MIN = -2147483648
M_INIT = -1e29
LOG2E = 1.4426950408889634
SEL_COLS = 768
N_BIG = 3 * 1024

f32 = jnp.float32
i32 = jnp.int32


def _sigmoid(x):
    return 0.5 * jnp.tanh(0.5 * x) + 0.5


def _dot(a, b):
    return jnp.dot(a, b, preferred_element_type=f32)


def _params(*sem):
    return pltpu.CompilerParams(dimension_semantics=sem, vmem_limit_bytes=V7X_VMEM_LIMIT_BYTES)


def _ada_kernel(c_ref, w_ref, b_ref, o_ref):
    c = c_ref[...]
    act = (c * jax.nn.sigmoid(c)).astype(MXU_DTYPE)
    o_ref[0] = _dot(act, w_ref[0].astype(MXU_DTYPE)) + b_ref[0]


def _ada_mod(c, ada_w, ada_b):
    L, D, N = ada_w.shape
    B = c.shape[0]
    tn = min(N, 1024)
    return pl.pallas_call(
        _ada_kernel,
        grid=(L, N // tn),
        in_specs=[pl.BlockSpec((B, D), lambda l, j: (0, 0)),
                  pl.BlockSpec((1, D, tn), lambda l, j: (l, 0, j)),
                  pl.BlockSpec((1, 1, tn), lambda l, j: (l, 0, j))],
        out_specs=pl.BlockSpec((1, B, tn), lambda l, j: (l, 0, j)),
        out_shape=jax.ShapeDtypeStruct((L, B, N), f32),
        compiler_params=_params("arbitrary", "arbitrary"),
        name="ada_mod",
    )(c, ada_w, ada_b.reshape(L, 1, N))


def _rope_kernel(pos_ref, freq_ref, sign_ref, o_ref):
    ang = pos_ref[...].astype(f32) * freq_ref[...]
    lane = lax.broadcasted_iota(i32, ang.shape, 1)

    def rep(v, n):
        return jnp.where(lane < n, v, pltpu.roll(v, n, 1))

    for t, (trig, fill) in enumerate(((jnp.cos(ang), 1.0), (jnp.sin(ang), 0.0))):
        a = rep(trig, HD_A // 2)
        b = rep(rep(pltpu.roll(trig, LANES - HD_A // 2, 1), HD_B // 2), HD_B)
        i_ = rep(pltpu.roll(trig, LANES - HD_A // 2 - HD_B // 2, 1), D_I_ROPE // 2)
        i_ = rep(jnp.where(lane < D_I_ROPE, i_, fill), D_I)
        for k, tab in enumerate((a, b, i_)):
            if t:
                tab = tab * sign_ref[k:k + 1, :]
            o_ref[:, (2 * k + t) * LANES:(2 * k + t + 1) * LANES] = tab


def _rope_rows():
    lane = jnp.arange(LANES)

    def inv_freq(dim):
        return ROPE_THETA ** (-jnp.arange(0, dim, 2, dtype=f32) / dim)

    freq = jnp.concatenate([inv_freq(HD_A), inv_freq(HD_B), inv_freq(D_I_ROPE),
                            jnp.zeros((LANES - (HD_A + HD_B + D_I_ROPE) // 2,), f32)])[None]
    sa = jnp.where(lane % HD_A < HD_A // 2, -1.0, 1.0)
    sb = jnp.where(lane % HD_B < HD_B // 2, -1.0, 1.0)
    si = jnp.where(lane % D_I < D_I_ROPE // 2, -1.0, 1.0)
    return freq.astype(f32), jnp.stack([sa, sb, si]).astype(f32)


def _rope_tables(positions):
    n = positions.size
    tm = min(n, 1024)
    freq, sign = _rope_rows()
    return pl.pallas_call(
        _rope_kernel,
        grid=(n // tm,),
        in_specs=[pl.BlockSpec((tm, 1), lambda i: (i, 0)),
                  pl.BlockSpec((1, LANES), lambda i: (0, 0)),
                  pl.BlockSpec((3, LANES), lambda i: (0, 0))],
        out_specs=pl.BlockSpec((tm, 6 * LANES), lambda i: (i, 0)),
        out_shape=jax.ShapeDtypeStruct((n, 6 * LANES), f32),
        compiler_params=_params("arbitrary"),
        name="rope_tables",
    )(positions.reshape(n, 1), freq, sign)


NORM_SLAB = 16


def _norm_modulate_to(u_ref, h_ref, g, shift, scale, copy_ref=None):
    geff = g * (1.0 + scale)
    for s in range(h_ref.shape[0] // NORM_SLAB):
        rs = slice(s * NORM_SLAB, (s + 1) * NORM_SLAB)
        x = h_ref[rs, :]
        ms = jnp.mean(x * x, axis=-1, keepdims=True)
        u_ref[rs, :] = (x * lax.rsqrt(ms + EPS) * geff + shift).astype(u_ref.dtype)
        if copy_ref is not None:
            copy_ref[rs, :] = x


def _head_norm(x, gain, lane, head_dim):
    sq = x * x
    if head_dim == LANES:
        ms = jnp.mean(sq, axis=-1, keepdims=True)
    else:
        lo = lane < head_dim
        s_lo = jnp.sum(jnp.where(lo, sq, 0.0), axis=-1, keepdims=True)
        s_hi = jnp.sum(jnp.where(lo, 0.0, sq), axis=-1, keepdims=True)
        ms = jnp.where(lo, s_lo, s_hi) * (1.0 / head_dim)
    return x * lax.rsqrt(ms + EPS) * gain


def _rotate(y, cos, sin, lane, half, period):
    if 2 * half == LANES:
        partner = pltpu.roll(y, half, 1)
    else:
        fwd = pltpu.roll(y, LANES - half, 1)
        bwd = pltpu.roll(y, half, 1)
        partner = jnp.where(lane % period < half, fwd, bwd)
    return y * cos + partner * sin


def _ffn_kernel(h_ref, sh_ref, sc_ref, gt_ref, g_ref, wg_ref, wu_ref, wd_ref, o_ref, u_ref, *, rows, tn):
    j = pl.program_id(1)

    @pl.when(j == 0)
    def _():
        _norm_modulate_to(u_ref, h_ref, g_ref[...], sh_ref[0], sc_ref[0], copy_ref=o_ref)

    half_gate = 0.5 * gt_ref[0]
    for r in range(u_ref.shape[0] // rows):
        rs = slice(r * rows, (r + 1) * rows)
        u = u_ref[rs, :]
        a = _dot(u, wg_ref[...])
        b = _dot(u, wu_ref[...])
        act = (a * _sigmoid(a) * b).astype(MXU_DTYPE)
        for c in range(o_ref.shape[1] // tn):
            sl = slice(c * tn, (c + 1) * tn)
            o_ref[rs, sl] += half_gate[:, sl] * _dot(act, wd_ref[:, sl])


def _ffn(h, shift, scale, gate, g, wg, wu, wd, layer, *, seq, tm, rows, tf, tn):
    n, d = h.shape
    f = wg.shape[-1]
    per_b = seq // tm
    row = pl.BlockSpec((1, 1, d), lambda i, j: (i // per_b, 0, 0))
    return pl.pallas_call(
        functools.partial(_ffn_kernel, rows=rows, tn=tn),
        grid=(n // tm, f // tf),
        in_specs=[pl.BlockSpec((tm, d), lambda i, j: (jnp.minimum(i + jnp.where(j > 0, 1, 0), n // tm - 1), 0)),
                  row, row, row,
                  pl.BlockSpec((1, d), lambda i, j: (0, 0)),
                  pl.BlockSpec((None, d, tf), lambda i, j: (layer, 0, j)),
                  pl.BlockSpec((None, d, tf), lambda i, j: (layer, 0, j)),
                  pl.BlockSpec((None, tf, d), lambda i, j: (layer, j, 0))],
        out_specs=pl.BlockSpec((tm, d), lambda i, j: (i, 0)),
        out_shape=jax.ShapeDtypeStruct((n, d), f32),
        scratch_shapes=[pltpu.VMEM((tm, d), MXU_DTYPE)],
        compiler_params=_params("arbitrary", "arbitrary"),
        name="ffn",
    )(h, shift, scale, gate, g, wg, wu, wd)


def _proj_kernel(h_ref, sh_ref, sc_ref, g_ref, wq_ref, wgate_lo_ref, wgate_hi_ref, wsel_ref, gains_ref, rope_ref,
                 qa_ref, qi_ref, qb_ref, ga_ref, gb_ref, kv_ref, ki_ref, wi_ref, u_ref, ra_ref, rb_ref):
    j = pl.program_id(1)
    tm = h_ref.shape[0]
    tn = ra_ref.shape[1]
    lane = lax.broadcasted_iota(i32, (tm, LANES), 1)

    def table(k):
        return rope_ref[:, k * LANES:(k + 1) * LANES]

    def gain(k):
        return gains_ref[k:k + 1, :]

    def rope_a(y):
        return _rotate(y, table(0), table(1), lane, HD_A // 2, HD_A)

    def rope_b(y):
        return _rotate(y, table(2), table(3), lane, HD_B // 2, HD_B)

    def rope_i(y):
        return _rotate(y, table(4), table(5), lane, D_I_ROPE // 2, D_I)

    def q_dot(c):
        return _dot(u_ref[...], wq_ref[:, c * tn:(c + 1) * tn])

    def gate_dot():
        u = u_ref[...]
        return jnp.concatenate([_dot(u, wgate_lo_ref[...]), _dot(u, wgate_hi_ref[...])], axis=1)

    @pl.when(j == 0)
    def _():
        _norm_modulate_to(u_ref, h_ref, g_ref[...], sh_ref[0], sc_ref[0])
        sel = _dot(u_ref[...], wsel_ref[...])
        ra_ref[...] = q_dot(0)
        ka = rope_a(_head_norm(sel[:, 0:128], gain(1), lane, HD_A))
        kb = rope_b(_head_norm(sel[:, 256:384], gain(3), lane, HD_B))
        kv_ref[:, 0:128] = ka.astype(MXU_DTYPE)
        kv_ref[:, 128:256] = sel[:, 128:256].astype(MXU_DTYPE)
        kv_ref[:, 256:384] = kb.astype(MXU_DTYPE)
        kv_ref[:, 384:512] = sel[:, 384:512].astype(MXU_DTYPE)
        ki_ref[...] = rope_i(sel[:, 512:640]).astype(MXU_DTYPE)
        wi_ref[...] = sel[:, 640:768] * (H_I ** -0.5 * D_I ** -0.5)

    @pl.when(j == 1)
    def _():
        rb_ref[...] = q_dot(1)
        for k in range(8):
            y = _head_norm(ra_ref[:, k * LANES:(k + 1) * LANES], gain(0), lane, HD_A)
            qa_ref[:, k * LANES:(k + 1) * LANES] = rope_a(y).astype(MXU_DTYPE)

    @pl.when(j == 2)
    def _():
        ra_ref[...] = q_dot(2)
        for k in range(8):
            qi_ref[:, k * LANES:(k + 1) * LANES] = rope_i(rb_ref[:, k * LANES:(k + 1) * LANES]).astype(MXU_DTYPE)

    @pl.when(j == 3)
    def _():
        rb_ref[...] = gate_dot()
        for k in range(8):
            y = _head_norm(ra_ref[:, k * LANES:(k + 1) * LANES], gain(2), lane, HD_B)
            qb_ref[:, k * LANES:(k + 1) * LANES] = (rope_b(y) * (HD_B ** -0.5)).astype(MXU_DTYPE)

    per_gate = ga_ref.shape[1] // tn

    def put_gate(c, res):
        ref = ga_ref if c < per_gate else gb_ref
        c = c % per_gate
        ref[:, c * tn:(c + 1) * tn] = _sigmoid(res).astype(MXU_DTYPE)

    @pl.when(j == 4)
    def _():
        put_gate(1, gate_dot())
        put_gate(0, rb_ref[...])

    for c in range(2, 2 * per_gate):
        @pl.when(j == 3 + c)
        def _(c=c):
            put_gate(c, gate_dot())


def _proj(h, shift, scale, g, wq, wgate, wsel, gains, rope, layer, *, seq, tm):
    n, d = h.shape
    d_gate = wgate.shape[-1] // 2
    per_b = seq // tm
    tn = 1024
    n_q = N_BIG // tn
    n_gate = 2 * d_gate // tn
    nt = n // tm
    row = pl.BlockSpec((1, 1, d), lambda i, j: (i // per_b, 0, 0))
    tile = lambda w: pl.BlockSpec((tm, w), lambda i, j: (i, 0))

    def tile_until(w, last_use):
        return pl.BlockSpec((tm, w), lambda i, j: (jnp.minimum(i + jnp.where(j > last_use, 1, 0), nt - 1), 0))

    def gate_chunk(half):
        return pl.BlockSpec((None, d, tn // 2), lambda i, j: (
            layer, 0, 2 * jnp.where(j < n_q, n_gate - 1, j - n_q) + half))

    resident = lambda w: pl.BlockSpec((None, d, w), lambda i, j: (layer, 0, 0), pipeline_mode=pl.Buffered(1))
    bf = lambda w: jax.ShapeDtypeStruct((n, w), MXU_DTYPE)
    return pl.pallas_call(
        _proj_kernel,
        grid=(nt, n_q + n_gate),
        in_specs=[tile_until(d, 0), row, row,
                  pl.BlockSpec((1, d), lambda i, j: (0, 0)),
                  resident(N_BIG), gate_chunk(0), gate_chunk(1),
                  resident(SEL_COLS),
                  pl.BlockSpec((4, LANES), lambda i, j: (0, 0)),
                  tile_until(6 * LANES, n_q)],
        out_specs=[tile_until(1024, 1), tile_until(1024, 2), tile_until(1024, 3),
                   tile_until(d_gate, max(4, 2 + d_gate // tn)), tile(d_gate),
                   tile_until(512, 0), tile_until(LANES, 0), tile_until(LANES, 0)],
        out_shape=[bf(1024), bf(1024), bf(1024), bf(d_gate), bf(d_gate), bf(512), bf(LANES),
                   jax.ShapeDtypeStruct((n, LANES), f32)],
        scratch_shapes=[pltpu.VMEM((tm, d), MXU_DTYPE),
                        pltpu.VMEM((tm, tn), f32), pltpu.VMEM((tm, tn), f32)],
        compiler_params=_params("arbitrary", "arbitrary"),
        name="mix_proj",
    )(h, shift, scale, g, wq, wgate, wgate, wsel, gains, rope)


QB = 256
KC = 256
I16_MIN = -32768
PACK = 16


def _attn_kernel(qa_ref, qi_ref, qb_ref, wi_ref, kv_ref, ki_ref, sink_ref,
                 ya_ref, yb_ref,
                 vat_ref, vbt_ref, qat_ref, qit_ref, key_ref, hi_ref, lo_ref, m_ref, l_ref, *head_refs,
                 n_sel):
    acc_refs, za_refs, zb_refs = head_refs[:H_A], head_refs[H_A:2 * H_A], head_refs[2 * H_A:]
    i = pl.program_id(1)
    nstep = pl.num_programs(1)
    nchunk = i + 1
    halves = (slice(0, BLOCK), slice(BLOCK, QB))

    def rows(c, size=KC):
        return pl.ds(pl.multiple_of(c * size, size), size)

    def t128(x):
        return x.astype(f32).T.astype(MXU_DTYPE)

    @pl.when(i == 0)
    def _():
        def body(c, carry):
            b0, b1 = rows(2 * c, BLOCK), rows(2 * c + 1, BLOCK)
            vat_ref[c] = jnp.concatenate([t128(kv_ref[b0, 128:256]), t128(kv_ref[b1, 128:256])], axis=1)
            vbt_ref[2 * c] = t128(kv_ref[b0, 384:512])
            vbt_ref[2 * c + 1] = t128(kv_ref[b1, 384:512])
            return carry
        lax.fori_loop(0, nstep, body, 0)

    for k in range(8):
        sl = slice(k * LANES, (k + 1) * LANES)
        for e, hs in enumerate(halves):
            qat_ref[:, k * QB + e * BLOCK:k * QB + (e + 1) * BLOCK] = t128(qa_ref[hs, sl])
            qit_ref[sl, hs] = t128(qi_ref[hs, sl])
    w_t = jnp.concatenate([wi_ref[hs, :].T for hs in halves], axis=1)

    q_lane = lax.broadcasted_iota(i32, (BLOCK, QB), 1)
    k_row = lax.broadcasted_iota(i32, (BLOCK, QB), 0)
    q_pos = i * QB + q_lane

    def idx_body(c, carry):
        for e in range(KC // BLOCK):
            cc = (KC // BLOCK) * c + e
            r = rows(cc, BLOCK)
            kc = ki_ref[r, 0:D_I]
            acc = jnp.zeros((BLOCK, QB), f32)
            for h in range(H_I):
                d = _dot(kc, qit_ref[h * D_I:(h + 1) * D_I, :])
                acc = acc + w_t[h:h + 1, :] * jnp.maximum(d, 0.0)
            bits = lax.bitcast_convert_type(acc + 0.0, i32)
            key = jnp.where(bits >= 0, bits, bits ^ 0x7FFFFFFF)
            key = jnp.where(cc * BLOCK + k_row <= q_pos, key, INT_MIN)
            key_ref[r, :] = key
            hi_ref[r, :] = (key >> 16).astype(jnp.int16)
            lo_ref[r, :] = ((key & 0xFFFF) + I16_MIN).astype(jnp.int16)
        return carry
    lax.fori_loop(0, nchunk, idx_body, 0)

    need = jnp.minimum(n_sel, q_pos[0:1, :] + 1).astype(f32)
    one, zero = jnp.ones((), jnp.bfloat16), jnp.zeros((), jnp.bfloat16)

    n_total = key_ref.shape[0] // KC
    n_half = max(n_total // 2, 1)
    span = jnp.where(nchunk <= n_half, n_half, n_total)

    def pad_body(c, carry):
        hi_ref[rows(c), :] = jnp.full((KC, QB), I16_MIN, jnp.int16)
        lo_ref[rows(c), :] = jnp.full((KC, QB), I16_MIN, jnp.int16)
        return carry
    lax.fori_loop(nchunk, span, pad_body, 0)

    def search(n_chunks):
        def count_ge(ref, cand):
            cnt = None
            for c in range(n_chunks):
                ind = jnp.where(ref[c * KC:(c + 1) * KC, :] >= cand, one, zero)
                parts = [ind[g * PACK:(g + 1) * PACK] for g in range(KC // PACK)]
                while len(parts) > 1:
                    parts = [a + b for a, b in zip(parts[0::2], parts[1::2])]
                cnt = parts[0] if cnt is None else cnt + parts[0]
            return jnp.sum(cnt.astype(f32), axis=0, keepdims=True)

        def bisect(ref, want):
            def body(k, t_u):
                cand_u = t_u | lax.shift_left(jnp.int32(1), 15 - k)
                ok = count_ge(ref, (cand_u + I16_MIN).astype(jnp.int16)) >= want
                return jnp.where(ok, cand_u, t_u)
            return lax.fori_loop(0, 16, body, jnp.zeros((1, QB), i32))

        hi_u = bisect(hi_ref, need)
        hi_t = (hi_u + I16_MIN).astype(jnp.int16)
        above = jnp.where(hi_u == 0xFFFF, 0.0,
                          count_ge(hi_ref, (jnp.minimum(hi_u + 1, 0xFFFF) + I16_MIN).astype(jnp.int16)))
        for c in range(n_chunks):
            cs = slice(c * KC, (c + 1) * KC)
            lo_ref[cs, :] = jnp.where(hi_ref[cs, :] == hi_t, lo_ref[cs, :], jnp.int16(I16_MIN))
        lo_u = bisect(lo_ref, need - above)
        return ((hi_u + I16_MIN) * 65536 + lo_u,
                above + count_ge(lo_ref, (lo_u + I16_MIN).astype(jnp.int16)))

    if n_half == n_total:
        thr, picked = search(n_total)
    else:
        thr, picked = lax.cond(nchunk <= n_half, lambda: search(n_half), lambda: search(n_total))


    @pl.when(jnp.max(picked - need) > 0.0)
    def _():
        def gt_body(c, cnt):
            return cnt + jnp.sum(jnp.where(key_ref[rows(c), :] > thr, 1.0, 0.0), axis=0, keepdims=True)
        allowed = need - lax.fori_loop(0, nchunk, gt_body, jnp.zeros((1, QB), f32))
        tri = (lax.broadcasted_iota(i32, (KC, KC), 0) >= lax.broadcasted_iota(i32, (KC, KC), 1))
        tri = jnp.where(tri, 1.0, 0.0).astype(jnp.bfloat16)

        def fix_body(c, seen):
            k = key_ref[rows(c), :]
            tie = jnp.where(k == thr, 1.0, 0.0)
            rank = _dot(tri, tie.astype(jnp.bfloat16)) + seen
            key_ref[rows(c), :] = jnp.where((tie > 0.0) & (rank > allowed), thr - 1, k)
            return seen + jnp.sum(tie, axis=0, keepdims=True)
        lax.fori_loop(0, nchunk, fix_body, jnp.zeros((1, QB), f32))

    m_ref[...] = jnp.full(m_ref.shape, M_INIT, f32)
    l_ref[...] = jnp.zeros(l_ref.shape, f32)
    for acc_ref in acc_refs:
        acc_ref[...] = jnp.zeros(acc_ref.shape, f32)
    c_a = HD_A ** -0.5 * LOG2E

    def logits_of(c, h):
        return _dot(kv_ref[rows(c), 0:128], qat_ref[:, h * QB:(h + 1) * QB])

    def chunk_step(c, z_in, z_out):
        nxt = jnp.minimum(c + 1, nchunk - 1)
        vt = vat_ref[c]
        bias = jnp.where(key_ref[rows(c), :] >= thr, 0.0, NEG)
        m_all, l_all = m_ref[...], l_ref[...]
        m_out, l_out = [], []
        for h in range(H_A):
            hs = slice(h * QB, (h + 1) * QB)
            z = z_in[h][...] + bias
            m_old = m_all[:, hs]
            m_new = jnp.maximum(m_old, jnp.max(z, axis=0, keepdims=True))
            z_out[h][...] = logits_of(nxt, h)
            alpha = jnp.exp2((m_old - m_new) * c_a)
            p = jnp.exp2((z - m_new) * c_a)
            l_out.append(alpha * l_all[:, hs] + jnp.sum(p, axis=0, keepdims=True))
            acc_refs[h][...] = alpha * acc_refs[h][...] + _dot(vt, p.astype(MXU_DTYPE))
            m_out.append(m_new)
        m_ref[...] = jnp.concatenate(m_out, axis=1)
        l_ref[...] = jnp.concatenate(l_out, axis=1)

    for h in range(H_A):
        za_refs[h][...] = logits_of(0, h)

    def att_body(k2, carry):
        chunk_step(2 * k2, za_refs, zb_refs)

        @pl.when(2 * k2 + 1 < nchunk)
        def _():
            chunk_step(2 * k2 + 1, zb_refs, za_refs)
        return carry
    lax.fori_loop(0, (nchunk + 1) // 2, att_body, 0)

    inv_l = 1.0 / l_ref[...]
    for k in range(H_A):
        ya_t = acc_refs[k][...] * inv_l[:, k * QB:(k + 1) * QB]
        for e, hs in enumerate(halves):
            ya_ref[hs, k * LANES:(k + 1) * LANES] = ya_t[:, e * BLOCK:(e + 1) * BLOCK].T.astype(MXU_DTYPE)

    b_lane = lax.broadcasted_iota(i32, (BLOCK, LANES), 1)
    b_row = lax.broadcasted_iota(i32, (BLOCK, LANES), 0)
    sink = sink_ref[...]
    zero_pad = jnp.zeros((HD_B, LANES), MXU_DTYPE)
    per_g = H_B // KVH_B
    for e, hs in enumerate(halves):
        blk = 2 * i + e
        ip = jnp.maximum(blk - 1, 0)
        kband = jnp.concatenate([kv_ref[rows(ip, BLOCK), 256:384], kv_ref[rows(blk, BLOCK), 256:384]], axis=0)
        ext = []
        for k in range(8):
            t = t128(qb_ref[hs, k * LANES:(k + 1) * LANES])
            for o in range(2):
                piece = t[o * HD_B:(o + 1) * HD_B, :]
                grp = (2 * k + o) // per_g
                ext.append(jnp.concatenate([piece, zero_pad] if grp == 0 else [zero_pad, piece], axis=0))
        s = _dot(kband, jnp.concatenate(ext, axis=1))
        prev_open = jnp.where(blk > 0, 0.0, NEG)
        bias = jnp.concatenate([jnp.where(b_row > b_lane, prev_open, NEG),
                                jnp.where(b_row <= b_lane, 0.0, NEG)], axis=0)
        bias2 = jnp.concatenate([bias, bias], axis=1)
        vbands = [jnp.concatenate([vbt_ref[ip][g * HD_B:(g + 1) * HD_B, :],
                                   vbt_ref[blk][g * HD_B:(g + 1) * HD_B, :]], axis=1) for g in range(KVH_B)]
        for k in range(8):
            cols = slice(2 * k * LANES, (2 * k + 2) * LANES)
            z = s[:, cols] + bias2
            m = jnp.maximum(jnp.max(z, axis=0, keepdims=True), sink[:, cols])
            ex = jnp.exp(z - m)
            inv = 1.0 / (jnp.sum(ex, axis=0, keepdims=True) + jnp.exp(sink[:, cols] - m))
            og = _dot(vbands[2 * k // per_g], ex.astype(MXU_DTYPE)) * inv
            yb_ref[hs, k * LANES:(k + 1) * LANES] = (
                jnp.concatenate([og[:, 0:LANES], og[:, LANES:2 * LANES]], axis=0).T.astype(MXU_DTYPE))


def _attn(qa, qi, qb, wi, kv, ki, sink_row, *, batch, seq):
    n = qa.shape[0]
    nstep = seq // QB
    n_sel = min(K_MAX, seq // 4)
    blk = lambda w: pl.BlockSpec((QB, w), lambda b, i: (b * nstep + i, 0))
    per_b = lambda w: pl.BlockSpec((seq, w), lambda b, i: (b, 0))
    return pl.pallas_call(
        functools.partial(_attn_kernel, n_sel=n_sel),
        grid=(batch, nstep),
        in_specs=[blk(1024), blk(1024), blk(1024), blk(LANES), per_b(512), per_b(LANES),
                  pl.BlockSpec((1, H_B * LANES), lambda b, i: (0, 0))],
        out_specs=[blk(1024), blk(1024)],
        out_shape=[jax.ShapeDtypeStruct((n, 1024), MXU_DTYPE)] * 2,
        scratch_shapes=[pltpu.VMEM((nstep, HD_A, KC), MXU_DTYPE),
                        pltpu.VMEM((seq // BLOCK, LANES, BLOCK), MXU_DTYPE),
                        pltpu.VMEM((HD_A, H_A * QB), MXU_DTYPE),
                        pltpu.VMEM((H_I * D_I, QB), MXU_DTYPE),
                        pltpu.VMEM((seq, QB), i32),
                        pltpu.VMEM((seq, QB), jnp.int16),
                        pltpu.VMEM((seq, QB), jnp.int16),
                        pltpu.VMEM((1, H_A * QB), f32),
                        pltpu.VMEM((1, H_A * QB), f32)]
                       + [pltpu.VMEM((HD_A, QB), f32)] * H_A
                       + [pltpu.VMEM((KC, QB), f32)] * (2 * H_A),

        compiler_params=_params("arbitrary", "arbitrary"),
        name="mixers",
    )(qa, qi, qb, wi, kv, ki, sink_row)


def _out_kernel(h_ref, ya_ref, yb_ref, ga_ref, gb_ref, gt_ref, woa_ref, wob_ref, wout_ref, o_ref, *, tn, tc):
    ya, yb, gate = ya_ref[...], yb_ref[...], gt_ref[0]
    d = o_ref.shape[1]
    for c in range(d // tn):
        sl = slice(c * tn, (c + 1) * tn)
        merged = (ga_ref[:, sl].astype(f32) * _dot(ya, woa_ref[:, sl])
                  + gb_ref[:, sl].astype(f32) * _dot(yb, wob_ref[:, sl])).astype(MXU_DTYPE)
        for e in range(d // tc):
            oc = slice(e * tc, (e + 1) * tc)
            part = gate[:, oc] * _dot(merged, wout_ref[sl, oc])
            if c == 0:
                o_ref[:, oc] = h_ref[:, oc] + part
            else:
                o_ref[:, oc] += part


def _out_proj(h, ya, yb, ga, gb, gate, woa, wob, wout, layer, *, seq, tm, tn, tc):
    n, d = h.shape
    k = ya.shape[1]
    per_b = seq // tm
    resident = lambda r, c: pl.BlockSpec((None, r, c), lambda i: (layer, 0, 0), pipeline_mode=pl.Buffered(1))
    return pl.pallas_call(
        functools.partial(_out_kernel, tn=tn, tc=tc),
        grid=(n // tm,),
        in_specs=[pl.BlockSpec((tm, d), lambda i: (i, 0)),
                  pl.BlockSpec((tm, k), lambda i: (i, 0)),
                  pl.BlockSpec((tm, k), lambda i: (i, 0)),
                  pl.BlockSpec((tm, d), lambda i: (i, 0)),
                  pl.BlockSpec((tm, d), lambda i: (i, 0)),
                  pl.BlockSpec((1, 1, d), lambda i: (i // per_b, 0, 0)),
                  resident(k, d), resident(k, d), resident(d, d)],
        out_specs=pl.BlockSpec((tm, d), lambda i: (i, 0)),
        out_shape=jax.ShapeDtypeStruct((n, d), f32),
        compiler_params=_params("arbitrary"),
        name="out_proj",
    )(h, ya, yb, ga, gb, gate, woa, wob, wout)


def _tiles(seq, d_ff):
    tm = min(seq, 512)
    tm_ffn = min(seq, 1024)
    tf = 512 if d_ff % 512 == 0 else 256
    return tm, tm_ffn, tf


def _split_w_in(w_in, d):
    sizes = (H_A * HD_A, HD_A, HD_A, H_I * D_I, D_I, H_I, H_B * HD_B, KVH_B * HD_B, KVH_B * HD_B, d, d)
    offs = [0]
    for s in sizes:
        offs.append(offs[-1] + s)
    seg = lambda k: w_in[:, :, offs[k]:offs[k + 1]]
    pad = lambda a, w: jnp.pad(a, ((0, 0), (0, 0), (0, w - a.shape[-1])))
    q = jnp.concatenate([seg(0), seg(3), seg(6)], axis=-1)
    gates = jnp.concatenate([seg(9), seg(10)], axis=-1)
    sel = jnp.concatenate([seg(1), seg(2), seg(7), seg(8), pad(seg(4), LANES), pad(seg(5), LANES)], axis=-1)
    return q.astype(MXU_DTYPE), gates.astype(MXU_DTYPE), sel.astype(MXU_DTYPE)


def kernel(x, c, positions, ada_w, ada_b, norm_ffn1_g, ffn1_w_gate, ffn1_w_up, ffn1_w_down, norm_mix_g, w_in, qn_a_g, kn_a_g, qn_b_g, kn_b_g, sinks, w_o_a, w_o_b, w_out, norm_ffn2_g, ffn2_w_gate, ffn2_w_up, ffn2_w_down):
    batch, seq, d = x.shape
    depth = ada_w.shape[0]
    d_ff = ffn1_w_gate.shape[-1]
    n = batch * seq
    assert seq % QB == 0 and d % 1024 == 0
    tm, tm_ffn, tf = _tiles(seq, d_ff)
    tn = min(d, 512)

    mod = _ada_mod(c, ada_w, ada_b).reshape(depth, batch, N_MOD, 1, d)
    rope = _rope_tables(positions)
    wq, wgate, wsel = _split_w_in(w_in, d)
    cast = lambda w: w.astype(MXU_DTYPE)
    w1 = (cast(ffn1_w_gate), cast(ffn1_w_up), cast(ffn1_w_down))
    w2 = (cast(ffn2_w_gate), cast(ffn2_w_up), cast(ffn2_w_down))
    woa, wob, wout = cast(w_o_a), cast(w_o_b), cast(w_out)
    gains = jnp.stack([qn_a_g, kn_a_g, jnp.tile(qn_b_g, (1, LANES // HD_B)),
                       jnp.tile(kn_b_g, (1, LANES // HD_B))], axis=1)
    sink_rows = jnp.repeat(sinks, LANES, axis=1).reshape(depth, 1, H_B * LANES)

    h = x.reshape(n, d)
    for l in range(depth):
        m = [mod[l, :, k] for k in range(N_MOD)]
        h = _ffn(h, m[0], m[1], m[2], norm_ffn1_g[l][None], *w1, l, seq=seq, tm=tm_ffn, rows=tm, tf=tf, tn=tn)
        qa, qi, qb, ga, gb, kv, ki, wi = _proj(h, m[3], m[4], norm_mix_g[l][None], wq, wgate, wsel,
                                               gains[l], rope, l, seq=seq, tm=tm)
        ya, yb = _attn(qa, qi, qb, wi, kv, ki, sink_rows[l], batch=batch, seq=seq)
        h = _out_proj(h, ya, yb, ga, gb, m[5], woa, wob, wout, l, seq=seq, tm=tm, tn=2 * tn, tc=tn)
        h = _ffn(h, m[6], m[7], m[8], norm_ffn2_g[l][None], *w2, l, seq=seq, tm=tm_ffn, rows=tm, tf=tf, tn=tn)
    return h.reshape(batch, seq, d)
```

```python
import functools

import jax
import jax.numpy as jnp
from jax import lax
from jax.experimental import pallas as pl
from jax.experimental.pallas import tpu as pltpu

H_A, HD_A = 8, 128
H_I, D_I, D_I_ROPE = 16, 64, 32
K_MAX = 256
H_B, HD_B, KVH_B = 16, 64, 2
BLOCK = 128
ROPE_THETA = 10000.0
EPS = 1e-6
NEG = -1e30
N_MOD = 9

LANES = 128
V7X_VMEM_LIMIT_BYTES = 60 * 1024 * 1024
MXU_DTYPE = jnp.bfloat16

INT_MIN = -2147483648
M_INIT = -1e29
LOG2E = 1.4426950408889634
SEL_COLS = 768
N_BIG = 3 * 1024

f32 = jnp.float32
i32 = jnp.int32


def _sigmoid(x):
    return 0.5 * jnp.tanh(0.5 * x) + 0.5


def _dot(a, b):
    return jnp.dot(a, b, preferred_element_type=f32)


def _params(*sem):
    return pltpu.CompilerParams(dimension_semantics=sem, vmem_limit_bytes=V7X_VMEM_LIMIT_BYTES)


def _ada_kernel(c_ref, w_ref, b_ref, o_ref):
    c = c_ref[...]
    act = (c * jax.nn.sigmoid(c)).astype(MXU_DTYPE)
    o_ref[0] = _dot(act, w_ref[0].astype(MXU_DTYPE)) + b_ref[0]


def _ada_mod(c, ada_w, ada_b):
    L, D, N = ada_w.shape
    B = c.shape[0]
    tn = min(N, 1024)
    return pl.pallas_call(
        _ada_kernel,
        grid=(L, N // tn),
        in_specs=[pl.BlockSpec((B, D), lambda l, j: (0, 0)),
                  pl.BlockSpec((1, D, tn), lambda l, j: (l, 0, j)),
                  pl.BlockSpec((1, 1, tn), lambda l, j: (l, 0, j))],
        out_specs=pl.BlockSpec((1, B, tn), lambda l, j: (l, 0, j)),
        out_shape=jax.ShapeDtypeStruct((L, B, N), f32),
        compiler_params=_params("arbitrary", "arbitrary"),
        name="ada_mod",
    )(c, ada_w, ada_b.reshape(L, 1, N))


def _rope_kernel(pos_ref, freq_ref, sign_ref, o_ref):
    ang = pos_ref[...].astype(f32) * freq_ref[...]
    lane = lax.broadcasted_iota(i32, ang.shape, 1)

    def rep(v, n):
        return jnp.where(lane < n, v, pltpu.roll(v, n, 1))

    for t, (trig, fill) in enumerate(((jnp.cos(ang), 1.0), (jnp.sin(ang), 0.0))):
        a = rep(trig, HD_A // 2)
        b = rep(rep(pltpu.roll(trig, LANES - HD_A // 2, 1), HD_B // 2), HD_B)
        i_ = rep(pltpu.roll(trig, LANES - HD_A // 2 - HD_B // 2, 1), D_I_ROPE // 2)
        i_ = rep(jnp.where(lane < D_I_ROPE, i_, fill), D_I)
        for k, tab in enumerate((a, b, i_)):
            if t:
                tab = tab * sign_ref[k:k + 1, :]
            o_ref[:, (2 * k + t) * LANES:(2 * k + t + 1) * LANES] = tab


def _rope_rows():
    lane = jnp.arange(LANES)

    def inv_freq(dim):
        return ROPE_THETA ** (-jnp.arange(0, dim, 2, dtype=f32) / dim)

    freq = jnp.concatenate([inv_freq(HD_A), inv_freq(HD_B), inv_freq(D_I_ROPE),
                            jnp.zeros((LANES - (HD_A + HD_B + D_I_ROPE) // 2,), f32)])[None]
    sa = jnp.where(lane % HD_A < HD_A // 2, -1.0, 1.0)
    sb = jnp.where(lane % HD_B < HD_B // 2, -1.0, 1.0)
    si = jnp.where(lane % D_I < D_I_ROPE // 2, -1.0, 1.0)
    return freq.astype(f32), jnp.stack([sa, sb, si]).astype(f32)


def _rope_tables(positions):
    n = positions.size
    tm = min(n, 1024)
    freq, sign = _rope_rows()
    return pl.pallas_call(
        _rope_kernel,
        grid=(n // tm,),
        in_specs=[pl.BlockSpec((tm, 1), lambda i: (i, 0)),
                  pl.BlockSpec((1, LANES), lambda i: (0, 0)),
                  pl.BlockSpec((3, LANES), lambda i: (0, 0))],
        out_specs=pl.BlockSpec((tm, 6 * LANES), lambda i: (i, 0)),
        out_shape=jax.ShapeDtypeStruct((n, 6 * LANES), f32),
        compiler_params=_params("arbitrary"),
        name="rope_tables",
    )(positions.reshape(n, 1), freq, sign)


NORM_SLAB = 16


def _norm_modulate_to(u_ref, h_ref, g, shift, scale, copy_ref=None):
    geff = g * (1.0 + scale)
    for s in range(h_ref.shape[0] // NORM_SLAB):
        rs = slice(s * NORM_SLAB, (s + 1) * NORM_SLAB)
        x = h_ref[rs, :]
        ms = jnp.mean(x * x, axis=-1, keepdims=True)
        u_ref[rs, :] = (x * lax.rsqrt(ms + EPS) * geff + shift).astype(u_ref.dtype)
        if copy_ref is not None:
            copy_ref[rs, :] = x


def _head_norm(x, gain, lane, head_dim):
    sq = x * x
    if head_dim == LANES:
        ms = jnp.mean(sq, axis=-1, keepdims=True)
    else:
        lo = lane < head_dim
        s_lo = jnp.sum(jnp.where(lo, sq, 0.0), axis=-1, keepdims=True)
        s_hi = jnp.sum(jnp.where(lo, 0.0, sq), axis=-1, keepdims=True)
        ms = jnp.where(lo, s_lo, s_hi) * (1.0 / head_dim)
    return x * lax.rsqrt(ms + EPS) * gain


def _rotate(y, cos, sin, lane, half, period):
    if 2 * half == LANES:
        partner = pltpu.roll(y, half, 1)
    else:
        fwd = pltpu.roll(y, LANES - half, 1)
        bwd = pltpu.roll(y, half, 1)
        partner = jnp.where(lane % period < half, fwd, bwd)
    return y * cos + partner * sin


def _ffn_kernel(h_ref, sh_ref, sc_ref, gt_ref, g_ref, wg_ref, wu_ref, wd_ref, o_ref, u_ref, *, rows, tn):
    j = pl.program_id(1)

    @pl.when(j == 0)
    def _():
        _norm_modulate_to(u_ref, h_ref, g_ref[...], sh_ref[0], sc_ref[0], copy_ref=o_ref)

    half_gate = 0.5 * gt_ref[0]
    for r in range(u_ref.shape[0] // rows):
        rs = slice(r * rows, (r + 1) * rows)
        u = u_ref[rs, :]
        a = _dot(u, wg_ref[...])
        b = _dot(u, wu_ref[...])
        act = (a * _sigmoid(a) * b).astype(MXU_DTYPE)
        for c in range(o_ref.shape[1] // tn):
            sl = slice(c * tn, (c + 1) * tn)
            o_ref[rs, sl] += half_gate[:, sl] * _dot(act, wd_ref[:, sl])


def _ffn(h, shift, scale, gate, g, wg, wu, wd, layer, *, seq, tm, rows, tf, tn):
    n, d = h.shape
    f = wg.shape[-1]
    per_b = seq // tm
    row = pl.BlockSpec((1, 1, d), lambda i, j: (i // per_b, 0, 0))
    return pl.pallas_call(
        functools.partial(_ffn_kernel, rows=rows, tn=tn),
        grid=(n // tm, f // tf),
        in_specs=[pl.BlockSpec((tm, d), lambda i, j: (jnp.minimum(i + jnp.where(j > 0, 1, 0), n // tm - 1), 0)),
                  row, row, row,
                  pl.BlockSpec((1, d), lambda i, j: (0, 0)),
                  pl.BlockSpec((None, d, tf), lambda i, j: (layer, 0, j)),
                  pl.BlockSpec((None, d, tf), lambda i, j: (layer, 0, j)),
                  pl.BlockSpec((None, tf, d), lambda i, j: (layer, j, 0))],
        out_specs=pl.BlockSpec((tm, d), lambda i, j: (i, 0)),
        out_shape=jax.ShapeDtypeStruct((n, d), f32),
        scratch_shapes=[pltpu.VMEM((tm, d), MXU_DTYPE)],
        compiler_params=_params("arbitrary", "arbitrary"),
        name="ffn",
    )(h, shift, scale, gate, g, wg, wu, wd)


def _proj_kernel(h_ref, sh_ref, sc_ref, g_ref, wq_ref, wgate_lo_ref, wgate_hi_ref, wsel_ref, gains_ref, rope_ref,
                 qa_ref, qi_ref, qb_ref, ga_ref, gb_ref, kv_ref, ki_ref, wi_ref, u_ref, ra_ref, rb_ref):
    j = pl.program_id(1)
    tm = h_ref.shape[0]
    tn = ra_ref.shape[1]
    lane = lax.broadcasted_iota(i32, (tm, LANES), 1)

    def table(k):
        return rope_ref[:, k * LANES:(k + 1) * LANES]

    def gain(k):
        return gains_ref[k:k + 1, :]

    def rope_a(y):
        return _rotate(y, table(0), table(1), lane, HD_A // 2, HD_A)

    def rope_b(y):
        return _rotate(y, table(2), table(3), lane, HD_B // 2, HD_B)

    def rope_i(y):
        return _rotate(y, table(4), table(5), lane, D_I_ROPE // 2, D_I)

    def q_dot(c):
        return _dot(u_ref[...], wq_ref[:, c * tn:(c + 1) * tn])

    def gate_dot():
        u = u_ref[...]
        return jnp.concatenate([_dot(u, wgate_lo_ref[...]), _dot(u, wgate_hi_ref[...])], axis=1)

    @pl.when(j == 0)
    def _():
        _norm_modulate_to(u_ref, h_ref, g_ref[...], sh_ref[0], sc_ref[0])
        sel = _dot(u_ref[...], wsel_ref[...])
        ra_ref[...] = q_dot(0)
        ka = rope_a(_head_norm(sel[:, 0:128], gain(1), lane, HD_A))
        kb = rope_b(_head_norm(sel[:, 256:384], gain(3), lane, HD_B))
        kv_ref[:, 0:128] = ka.astype(MXU_DTYPE)
        kv_ref[:, 128:256] = sel[:, 128:256].astype(MXU_DTYPE)
        kv_ref[:, 256:384] = kb.astype(MXU_DTYPE)
        kv_ref[:, 384:512] = sel[:, 384:512].astype(MXU_DTYPE)
        ki_ref[...] = rope_i(sel[:, 512:640]).astype(MXU_DTYPE)
        wi_ref[...] = sel[:, 640:768] * (H_I ** -0.5 * D_I ** -0.5)

    @pl.when(j == 1)
    def _():
        rb_ref[...] = q_dot(1)
        for k in range(8):
            y = _head_norm(ra_ref[:, k * LANES:(k + 1) * LANES], gain(0), lane, HD_A)
            qa_ref[:, k * LANES:(k + 1) * LANES] = rope_a(y).astype(MXU_DTYPE)

    @pl.when(j == 2)
    def _():
        ra_ref[...] = q_dot(2)
        for k in range(8):
            qi_ref[:, k * LANES:(k + 1) * LANES] = rope_i(rb_ref[:, k * LANES:(k + 1) * LANES]).astype(MXU_DTYPE)

    @pl.when(j == 3)
    def _():
        rb_ref[...] = gate_dot()
        for k in range(8):
            y = _head_norm(ra_ref[:, k * LANES:(k + 1) * LANES], gain(2), lane, HD_B)
            qb_ref[:, k * LANES:(k + 1) * LANES] = (rope_b(y) * (HD_B ** -0.5)).astype(MXU_DTYPE)

    per_gate = ga_ref.shape[1] // tn

    def put_gate(c, res):
        ref = ga_ref if c < per_gate else gb_ref
        c = c % per_gate
        ref[:, c * tn:(c + 1) * tn] = _sigmoid(res).astype(MXU_DTYPE)

    @pl.when(j == 4)
    def _():
        put_gate(1, gate_dot())
        put_gate(0, rb_ref[...])

    for c in range(2, 2 * per_gate):
        @pl.when(j == 3 + c)
        def _(c=c):
            put_gate(c, gate_dot())


def _proj(h, shift, scale, g, wq, wgate, wsel, gains, rope, layer, *, seq, tm):
    n, d = h.shape
    d_gate = wgate.shape[-1] // 2
    per_b = seq // tm
    tn = 1024
    n_q = N_BIG // tn
    n_gate = 2 * d_gate // tn
    nt = n // tm
    row = pl.BlockSpec((1, 1, d), lambda i, j: (i // per_b, 0, 0))
    tile = lambda w: pl.BlockSpec((tm, w), lambda i, j: (i, 0))

    def tile_until(w, last_use):
        return pl.BlockSpec((tm, w), lambda i, j: (jnp.minimum(i + jnp.where(j > last_use, 1, 0), nt - 1), 0))

    def gate_chunk(half):
        return pl.BlockSpec((None, d, tn // 2), lambda i, j: (
            layer, 0, 2 * jnp.where(j < n_q, n_gate - 1, j - n_q) + half))

    resident = lambda w: pl.BlockSpec((None, d, w), lambda i, j: (layer, 0, 0), pipeline_mode=pl.Buffered(1))
    bf = lambda w: jax.ShapeDtypeStruct((n, w), MXU_DTYPE)
    return pl.pallas_call(
        _proj_kernel,
        grid=(nt, n_q + n_gate),
        in_specs=[tile_until(d, 1), row, row,
                  pl.BlockSpec((1, d), lambda i, j: (0, 0)),
                  resident(N_BIG), gate_chunk(0), gate_chunk(1),
                  resident(SEL_COLS),
                  pl.BlockSpec((4, LANES), lambda i, j: (0, 0)),
                  tile(6 * LANES)],
        out_specs=[tile(1024), tile(1024), tile(1024), tile(d_gate), tile(d_gate),
                   tile(512), tile(LANES), tile(LANES)],
        out_shape=[bf(1024), bf(1024), bf(1024), bf(d_gate), bf(d_gate), bf(512), bf(LANES),
                   jax.ShapeDtypeStruct((n, LANES), f32)],
        scratch_shapes=[pltpu.VMEM((tm, d), MXU_DTYPE),
                        pltpu.VMEM((tm, tn), f32), pltpu.VMEM((tm, tn), f32)],
        compiler_params=_params("arbitrary", "arbitrary"),
        name="mix_proj",
    )(h, shift, scale, g, wq, wgate, wgate, wsel, gains, rope)


QB = 256
KC = 256
I16_MIN = -32768
PACK = 16


def _attn_kernel(qa_ref, qi_ref, qb_ref, wi_ref, kv_ref, ki_ref, sink_ref,
                 ya_ref, yb_ref,
                 vat_ref, vbt_ref, qat_ref, qit_ref, key_ref, hi_ref, lo_ref, m_ref, l_ref, *head_refs,
                 n_sel):
    acc_refs, za_refs, zb_refs = head_refs[:H_A], head_refs[H_A:2 * H_A], head_refs[2 * H_A:]
    i = pl.program_id(1)
    nstep = pl.num_programs(1)
    nchunk = i + 1
    halves = (slice(0, BLOCK), slice(BLOCK, QB))

    def rows(c, size=KC):
        return pl.ds(pl.multiple_of(c * size, size), size)

    def t128(x):
        return x.astype(f32).T.astype(MXU_DTYPE)

    @pl.when(i == 0)
    def _():
        def body(c, carry):
            b0, b1 = rows(2 * c, BLOCK), rows(2 * c + 1, BLOCK)
            vat_ref[c] = jnp.concatenate([t128(kv_ref[b0, 128:256]), t128(kv_ref[b1, 128:256])], axis=1)
            vbt_ref[2 * c] = t128(kv_ref[b0, 384:512])
            vbt_ref[2 * c + 1] = t128(kv_ref[b1, 384:512])
            return carry
        lax.fori_loop(0, nstep, body, 0)

    for k in range(8):
        sl = slice(k * LANES, (k + 1) * LANES)
        for e, hs in enumerate(halves):
            qat_ref[:, k * QB + e * BLOCK:k * QB + (e + 1) * BLOCK] = t128(qa_ref[hs, sl])
            qit_ref[sl, hs] = t128(qi_ref[hs, sl])
    w_t = jnp.concatenate([wi_ref[hs, :].T for hs in halves], axis=1)

    q_lane = lax.broadcasted_iota(i32, (BLOCK, QB), 1)
    k_row = lax.broadcasted_iota(i32, (BLOCK, QB), 0)
    q_pos = i * QB + q_lane

    def idx_body(c, carry):
        for e in range(KC // BLOCK):
            cc = (KC // BLOCK) * c + e
            r = rows(cc, BLOCK)
            kc = ki_ref[r, 0:D_I]
            acc = jnp.zeros((BLOCK, QB), f32)
            for h in range(H_I):
                d = _dot(kc, qit_ref[h * D_I:(h + 1) * D_I, :])
                acc = acc + w_t[h:h + 1, :] * jnp.maximum(d, 0.0)
            bits = lax.bitcast_convert_type(acc + 0.0, i32)
            key = jnp.where(bits >= 0, bits, bits ^ 0x7FFFFFFF)
            key = jnp.where(cc * BLOCK + k_row <= q_pos, key, INT_MIN)
            key_ref[r, :] = key
            hi_ref[r, :] = (key >> 16).astype(jnp.int16)
            lo_ref[r, :] = ((key & 0xFFFF) + I16_MIN).astype(jnp.int16)
        return carry
    lax.fori_loop(0, nchunk, idx_body, 0)

    need = jnp.minimum(n_sel, q_pos[0:1, :] + 1).astype(f32)
    one, zero = jnp.ones((), jnp.bfloat16), jnp.zeros((), jnp.bfloat16)

    n_total = key_ref.shape[0] // KC
    n_half = max(n_total // 2, 1)
    span = jnp.where(nchunk <= n_half, n_half, n_total)

    def pad_body(c, carry):
        hi_ref[rows(c), :] = jnp.full((KC, QB), I16_MIN, jnp.int16)
        lo_ref[rows(c), :] = jnp.full((KC, QB), I16_MIN, jnp.int16)
        return carry
    lax.fori_loop(nchunk, span, pad_body, 0)

    def search(n_chunks):
        def count_ge(ref, cand):
            cnt = None
            for c in range(n_chunks):
                ind = jnp.where(ref[c * KC:(c + 1) * KC, :] >= cand, one, zero)
                parts = [ind[g * PACK:(g + 1) * PACK] for g in range(KC // PACK)]
                while len(parts) > 1:
                    parts = [a + b for a, b in zip(parts[0::2], parts[1::2])]
                cnt = parts[0] if cnt is None else cnt + parts[0]
            return jnp.sum(cnt.astype(f32), axis=0, keepdims=True)

        def bisect(ref, want):
            def body(k, t_u):
                cand_u = t_u | lax.shift_left(jnp.int32(1), 15 - k)
                ok = count_ge(ref, (cand_u + I16_MIN).astype(jnp.int16)) >= want
                return jnp.where(ok, cand_u, t_u)
            return lax.fori_loop(0, 16, body, jnp.zeros((1, QB), i32))

        hi_u = bisect(hi_ref, need)
        hi_t = (hi_u + I16_MIN).astype(jnp.int16)
        above = jnp.where(hi_u == 0xFFFF, 0.0,
                          count_ge(hi_ref, (jnp.minimum(hi_u + 1, 0xFFFF) + I16_MIN).astype(jnp.int16)))
        for c in range(n_chunks):
            cs = slice(c * KC, (c + 1) * KC)
            lo_ref[cs, :] = jnp.where(hi_ref[cs, :] == hi_t, lo_ref[cs, :], jnp.int16(I16_MIN))
        lo_u = bisect(lo_ref, need - above)
        return ((hi_u + I16_MIN) * 65536 + lo_u,
                above + count_ge(lo_ref, (lo_u + I16_MIN).astype(jnp.int16)))

    if n_half == n_total:
        thr, picked = search(n_total)
    else:
        thr, picked = lax.cond(nchunk <= n_half, lambda: search(n_half), lambda: search(n_total))


    @pl.when(jnp.max(picked - need) > 0.0)
    def _():
        def gt_body(c, cnt):
            return cnt + jnp.sum(jnp.where(key_ref[rows(c), :] > thr, 1.0, 0.0), axis=0, keepdims=True)
        allowed = need - lax.fori_loop(0, nchunk, gt_body, jnp.zeros((1, QB), f32))
        tri = (lax.broadcasted_iota(i32, (KC, KC), 0) >= lax.broadcasted_iota(i32, (KC, KC), 1))
        tri = jnp.where(tri, 1.0, 0.0).astype(jnp.bfloat16)

        def fix_body(c, seen):
            k = key_ref[rows(c), :]
            tie = jnp.where(k == thr, 1.0, 0.0)
            rank = _dot(tri, tie.astype(jnp.bfloat16)) + seen
            key_ref[rows(c), :] = jnp.where((tie > 0.0) & (rank > allowed), thr - 1, k)
            return seen + jnp.sum(tie, axis=0, keepdims=True)
        lax.fori_loop(0, nchunk, fix_body, jnp.zeros((1, QB), f32))

    m_ref[...] = jnp.full(m_ref.shape, M_INIT, f32)
    l_ref[...] = jnp.zeros(l_ref.shape, f32)
    for acc_ref in acc_refs:
        acc_ref[...] = jnp.zeros(acc_ref.shape, f32)
    c_a = HD_A ** -0.5 * LOG2E

    def logits_of(c, h):
        return _dot(kv_ref[rows(c), 0:128], qat_ref[:, h * QB:(h + 1) * QB])

    def chunk_step(c, z_in, z_out):
        nxt = jnp.minimum(c + 1, nchunk - 1)
        vt = vat_ref[c]
        bias = jnp.where(key_ref[rows(c), :] >= thr, 0.0, NEG)
        m_all, l_all = m_ref[...], l_ref[...]
        m_out, l_out = [], []
        for h in range(H_A):
            hs = slice(h * QB, (h + 1) * QB)
            z = z_in[h][...] + bias
            m_old = m_all[:, hs]
            m_new = jnp.maximum(m_old, jnp.max(z, axis=0, keepdims=True))
            z_out[h][...] = logits_of(nxt, h)
            alpha = jnp.exp2((m_old - m_new) * c_a)
            p = jnp.exp2((z - m_new) * c_a)
            l_out.append(alpha * l_all[:, hs] + jnp.sum(p, axis=0, keepdims=True))
            acc_refs[h][...] = alpha * acc_refs[h][...] + _dot(vt, p.astype(MXU_DTYPE))
            m_out.append(m_new)
        m_ref[...] = jnp.concatenate(m_out, axis=1)
        l_ref[...] = jnp.concatenate(l_out, axis=1)

    for h in range(H_A):
        za_refs[h][...] = logits_of(0, h)

    def att_body(k2, carry):
        chunk_step(2 * k2, za_refs, zb_refs)

        @pl.when(2 * k2 + 1 < nchunk)
        def _():
            chunk_step(2 * k2 + 1, zb_refs, za_refs)
        return carry
    lax.fori_loop(0, (nchunk + 1) // 2, att_body, 0)

    inv_l = 1.0 / l_ref[...]
    for k in range(H_A):
        ya_t = acc_refs[k][...] * inv_l[:, k * QB:(k + 1) * QB]
        for e, hs in enumerate(halves):
            ya_ref[hs, k * LANES:(k + 1) * LANES] = ya_t[:, e * BLOCK:(e + 1) * BLOCK].T.astype(MXU_DTYPE)

    b_lane = lax.broadcasted_iota(i32, (BLOCK, LANES), 1)
    b_row = lax.broadcasted_iota(i32, (BLOCK, LANES), 0)
    sink = sink_ref[...]
    zero_pad = jnp.zeros((HD_B, LANES), MXU_DTYPE)
    per_g = H_B // KVH_B
    for e, hs in enumerate(halves):
        blk = 2 * i + e
        ip = jnp.maximum(blk - 1, 0)
        kband = jnp.concatenate([kv_ref[rows(ip, BLOCK), 256:384], kv_ref[rows(blk, BLOCK), 256:384]], axis=0)
        ext = []
        for k in range(8):
            t = t128(qb_ref[hs, k * LANES:(k + 1) * LANES])
            for o in range(2):
                piece = t[o * HD_B:(o + 1) * HD_B, :]
                grp = (2 * k + o) // per_g
                ext.append(jnp.concatenate([piece, zero_pad] if grp == 0 else [zero_pad, piece], axis=0))
        s = _dot(kband, jnp.concatenate(ext, axis=1))
        prev_open = jnp.where(blk > 0, 0.0, NEG)
        bias = jnp.concatenate([jnp.where(b_row > b_lane, prev_open, NEG),
                                jnp.where(b_row <= b_lane, 0.0, NEG)], axis=0)
        bias2 = jnp.concatenate([bias, bias], axis=1)
        vbands = [jnp.concatenate([vbt_ref[ip][g * HD_B:(g + 1) * HD_B, :],
                                   vbt_ref[blk][g * HD_B:(g + 1) * HD_B, :]], axis=1) for g in range(KVH_B)]
        for k in range(8):
            cols = slice(2 * k * LANES, (2 * k + 2) * LANES)
            z = s[:, cols] + bias2
            m = jnp.maximum(jnp.max(z, axis=0, keepdims=True), sink[:, cols])
            ex = jnp.exp(z - m)
            inv = 1.0 / (jnp.sum(ex, axis=0, keepdims=True) + jnp.exp(sink[:, cols] - m))
            og = _dot(vbands[2 * k // per_g], ex.astype(MXU_DTYPE)) * inv
            yb_ref[hs, k * LANES:(k + 1) * LANES] = (
                jnp.concatenate([og[:, 0:LANES], og[:, LANES:2 * LANES]], axis=0).T.astype(MXU_DTYPE))


def _attn(qa, qi, qb, wi, kv, ki, sink_row, *, batch, seq):
    n = qa.shape[0]
    nstep = seq // QB
    n_sel = min(K_MAX, seq // 4)
    blk = lambda w: pl.BlockSpec((QB, w), lambda b, i: (b * nstep + i, 0))
    per_b = lambda w: pl.BlockSpec((seq, w), lambda b, i: (b, 0))
    return pl.pallas_call(
        functools.partial(_attn_kernel, n_sel=n_sel),
        grid=(batch, nstep),
        in_specs=[blk(1024), blk(1024), blk(1024), blk(LANES), per_b(512), per_b(LANES),
                  pl.BlockSpec((1, H_B * LANES), lambda b, i: (0, 0))],
        out_specs=[blk(1024), blk(1024)],
        out_shape=[jax.ShapeDtypeStruct((n, 1024), MXU_DTYPE)] * 2,
        scratch_shapes=[pltpu.VMEM((nstep, HD_A, KC), MXU_DTYPE),
                        pltpu.VMEM((seq // BLOCK, LANES, BLOCK), MXU_DTYPE),
                        pltpu.VMEM((HD_A, H_A * QB), MXU_DTYPE),
                        pltpu.VMEM((H_I * D_I, QB), MXU_DTYPE),
                        pltpu.VMEM((seq, QB), i32),
                        pltpu.VMEM((seq, QB), jnp.int16),
                        pltpu.VMEM((seq, QB), jnp.int16),
                        pltpu.VMEM((1, H_A * QB), f32),
                        pltpu.VMEM((1, H_A * QB), f32)]
                       + [pltpu.VMEM((HD_A, QB), f32)] * H_A
                       + [pltpu.VMEM((KC, QB), f32)] * (2 * H_A),

        compiler_params=_params("arbitrary", "arbitrary"),
        name="mixers",
    )(qa, qi, qb, wi, kv, ki, sink_row)


def _out_kernel(h_ref, ya_ref, yb_ref, ga_ref, gb_ref, gt_ref, woa_ref, wob_ref, wout_ref, o_ref, *, tn, tc):
    ya, yb, gate = ya_ref[...], yb_ref[...], gt_ref[0]
    d = o_ref.shape[1]
    for c in range(d // tn):
        sl = slice(c * tn, (c + 1) * tn)
        merged = (ga_ref[:, sl].astype(f32) * _dot(ya, woa_ref[:, sl])
                  + gb_ref[:, sl].astype(f32) * _dot(yb, wob_ref[:, sl])).astype(MXU_DTYPE)
        for e in range(d // tc):
            oc = slice(e * tc, (e + 1) * tc)
            part = gate[:, oc] * _dot(merged, wout_ref[sl, oc])
            if c == 0:
                o_ref[:, oc] = h_ref[:, oc] + part
            else:
                o_ref[:, oc] += part


def _out_proj(h, ya, yb, ga, gb, gate, woa, wob, wout, layer, *, seq, tm, tn, tc):
    n, d = h.shape
    k = ya.shape[1]
    per_b = seq // tm
    resident = lambda r, c: pl.BlockSpec((None, r, c), lambda i: (layer, 0, 0), pipeline_mode=pl.Buffered(1))
    return pl.pallas_call(
        functools.partial(_out_kernel, tn=tn, tc=tc),
        grid=(n // tm,),
        in_specs=[pl.BlockSpec((tm, d), lambda i: (i, 0)),
                  pl.BlockSpec((tm, k), lambda i: (i, 0)),
                  pl.BlockSpec((tm, k), lambda i: (i, 0)),
                  pl.BlockSpec((tm, d), lambda i: (i, 0)),
                  pl.BlockSpec((tm, d), lambda i: (i, 0)),
                  pl.BlockSpec((1, 1, d), lambda i: (i // per_b, 0, 0)),
                  resident(k, d), resident(k, d), resident(d, d)],
        out_specs=pl.BlockSpec((tm, d), lambda i: (i, 0)),
        out_shape=jax.ShapeDtypeStruct((n, d), f32),
        compiler_params=_params("arbitrary"),
        name="out_proj",
    )(h, ya, yb, ga, gb, gate, woa, wob, wout)


def _tiles(seq, d_ff):
    tm = min(seq, 512)
    tm_ffn = min(seq, 1024)
    tf = 512 if d_ff % 512 == 0 else 256
    return tm, tm_ffn, tf


def _split_w_in(w_in, d):
    sizes = (H_A * HD_A, HD_A, HD_A, H_I * D_I, D_I, H_I, H_B * HD_B, KVH_B * HD_B, KVH_B * HD_B, d, d)
    offs = [0]
    for s in sizes:
        offs.append(offs[-1] + s)
    seg = lambda k: w_in[:, :, offs[k]:offs[k + 1]]
    pad = lambda a, w: jnp.pad(a, ((0, 0), (0, 0), (0, w - a.shape[-1])))
    q = jnp.concatenate([seg(0), seg(3), seg(6)], axis=-1)
    gates = jnp.concatenate([seg(9), seg(10)], axis=-1)
    sel = jnp.concatenate([seg(1), seg(2), seg(7), seg(8), pad(seg(4), LANES), pad(seg(5), LANES)], axis=-1)
    return q.astype(MXU_DTYPE), gates.astype(MXU_DTYPE), sel.astype(MXU_DTYPE)


def kernel(x, c, positions, ada_w, ada_b, norm_ffn1_g, ffn1_w_gate, ffn1_w_up, ffn1_w_down, norm_mix_g, w_in, qn_a_g, kn_a_g, qn_b_g, kn_b_g, sinks, w_o_a, w_o_b, w_out, norm_ffn2_g, ffn2_w_gate, ffn2_w_up, ffn2_w_down):
    batch, seq, d = x.shape
    depth = ada_w.shape[0]
    d_ff = ffn1_w_gate.shape[-1]
    n = batch * seq
    assert seq % QB == 0 and d % 1024 == 0
    tm, tm_ffn, tf = _tiles(seq, d_ff)
    tn = min(d, 512)

    mod = _ada_mod(c, ada_w, ada_b).reshape(depth, batch, N_MOD, 1, d)
    rope = _rope_tables(positions)
    wq, wgate, wsel = _split_w_in(w_in, d)
    cast = lambda w: w.astype(MXU_DTYPE)
    w1 = (cast(ffn1_w_gate), cast(ffn1_w_up), cast(ffn1_w_down))
    w2 = (cast(ffn2_w_gate), cast(ffn2_w_up), cast(ffn2_w_down))
    woa, wob, wout = cast(w_o_a), cast(w_o_b), cast(w_out)
    gains = jnp.stack([qn_a_g, kn_a_g, jnp.tile(qn_b_g, (1, LANES // HD_B)),
                       jnp.tile(kn_b_g, (1, LANES // HD_B))], axis=1)
    sink_rows = jnp.repeat(sinks, LANES, axis=1).reshape(depth, 1, H_B * LANES)

    h = x.reshape(n, d)
    for l in range(depth):
        m = [mod[l, :, k] for k in range(N_MOD)]
        h = _ffn(h, m[0], m[1], m[2], norm_ffn1_g[l][None], *w1, l, seq=seq, tm=tm_ffn, rows=tm, tf=tf, tn=tn)
        qa, qi, qb, ga, gb, kv, ki, wi = _proj(h, m[3], m[4], norm_mix_g[l][None], wq, wgate, wsel,
                                               gains[l], rope, l, seq=seq, tm=tm)
        ya, yb = _attn(qa, qi, qb, wi, kv, ki, sink_rows[l], batch=batch, seq=seq)
        h = _out_proj(h, ya, yb, ga, gb, m[5], woa, wob, wout, l, seq=seq, tm=tm, tn=2 * tn, tc=tn)
        h = _ffn(h, m[6], m[7], m[8], norm_ffn2_g[l][None], *w2, l, seq=seq, tm=tm_ffn, rows=tm, tf=tf, tn=tn)
    return h.reshape(batch, seq, d)
```

```python
import functools

import jax
import jax.numpy as jnp
from jax import lax
from jax.experimental import pallas as pl
from jax.experimental.pallas import tpu as pltpu

H_A, HD_A = 8, 128
H_I, D_I, D_I_ROPE = 16, 64, 32
K_MAX = 256
H_B, HD_B, KVH_B = 16, 64, 2
BLOCK = 128
ROPE_THETA = 10000.0
EPS = 1e-6
NEG = -1e30
N_MOD = 9

LANES = 128
V7X_VMEM_LIMIT_BYTES = 60 * 1024 * 1024
MXU_DTYPE = jnp.bfloat16

INT_MIN = -2147483648
M_INIT = -1e29
LOG2E = 1.4426950408889634
SEL_COLS = 768
N_BIG = 3 * 1024

f32 = jnp.float32
i32 = jnp.int32


def _sigmoid(x):
    return 0.5 * jnp.tanh(0.5 * x) + 0.5


def _dot(a, b):
    return jnp.dot(a, b, preferred_element_type=f32)


def _params(*sem):
    return pltpu.CompilerParams(dimension_semantics=sem, vmem_limit_bytes=V7X_VMEM_LIMIT_BYTES)


def _ada_kernel(c_ref, w_ref, b_ref, o_ref):
    c = c_ref[...]
    act = (c * jax.nn.sigmoid(c)).astype(MXU_DTYPE)
    o_ref[0] = _dot(act, w_ref[0].astype(MXU_DTYPE)) + b_ref[0]


def _ada_mod(c, ada_w, ada_b):
    L, D, N = ada_w.shape
    B = c.shape[0]
    tn = min(N, 1024)
    return pl.pallas_call(
        _ada_kernel,
        grid=(L, N // tn),
        in_specs=[pl.BlockSpec((B, D), lambda l, j: (0, 0)),
                  pl.BlockSpec((1, D, tn), lambda l, j: (l, 0, j)),
                  pl.BlockSpec((1, 1, tn), lambda l, j: (l, 0, j))],
        out_specs=pl.BlockSpec((1, B, tn), lambda l, j: (l, 0, j)),
        out_shape=jax.ShapeDtypeStruct((L, B, N), f32),
        compiler_params=_params("arbitrary", "arbitrary"),
        name="ada_mod",
    )(c, ada_w, ada_b.reshape(L, 1, N))


def _rope_kernel(pos_ref, freq_ref, sign_ref, o_ref):
    ang = pos_ref[...].astype(f32) * freq_ref[...]
    lane = lax.broadcasted_iota(i32, ang.shape, 1)

    def rep(v, n):
        return jnp.where(lane < n, v, pltpu.roll(v, n, 1))

    for t, (trig, fill) in enumerate(((jnp.cos(ang), 1.0), (jnp.sin(ang), 0.0))):
        a = rep(trig, HD_A // 2)
        b = rep(rep(pltpu.roll(trig, LANES - HD_A // 2, 1), HD_B // 2), HD_B)
        i_ = rep(pltpu.roll(trig, LANES - HD_A // 2 - HD_B // 2, 1), D_I_ROPE // 2)
        i_ = rep(jnp.where(lane < D_I_ROPE, i_, fill), D_I)
        for k, tab in enumerate((a, b, i_)):
            if t:
                tab = tab * sign_ref[k:k + 1, :]
            o_ref[:, (2 * k + t) * LANES:(2 * k + t + 1) * LANES] = tab


def _rope_rows():
    lane = jnp.arange(LANES)

    def inv_freq(dim):
        return ROPE_THETA ** (-jnp.arange(0, dim, 2, dtype=f32) / dim)

    freq = jnp.concatenate([inv_freq(HD_A), inv_freq(HD_B), inv_freq(D_I_ROPE),
                            jnp.zeros((LANES - (HD_A + HD_B + D_I_ROPE) // 2,), f32)])[None]
    sa = jnp.where(lane % HD_A < HD_A // 2, -1.0, 1.0)
    sb = jnp.where(lane % HD_B < HD_B // 2, -1.0, 1.0)
    si = jnp.where(lane % D_I < D_I_ROPE // 2, -1.0, 1.0)
    return freq.astype(f32), jnp.stack([sa, sb, si]).astype(f32)


def _rope_tables(positions):
    n = positions.size
    tm = min(n, 1024)
    freq, sign = _rope_rows()
    return pl.pallas_call(
        _rope_kernel,
        grid=(n // tm,),
        in_specs=[pl.BlockSpec((tm, 1), lambda i: (i, 0)),
                  pl.BlockSpec((1, LANES), lambda i: (0, 0)),
                  pl.BlockSpec((3, LANES), lambda i: (0, 0))],
        out_specs=pl.BlockSpec((tm, 6 * LANES), lambda i: (i, 0)),
        out_shape=jax.ShapeDtypeStruct((n, 6 * LANES), f32),
        compiler_params=_params("arbitrary"),
        name="rope_tables",
    )(positions.reshape(n, 1), freq, sign)


NORM_SLAB = 16


def _norm_modulate_to(u_ref, h_ref, g, shift, scale, copy_ref=None):
    geff = g * (1.0 + scale)
    for s in range(h_ref.shape[0] // NORM_SLAB):
        rs = slice(s * NORM_SLAB, (s + 1) * NORM_SLAB)
        x = h_ref[rs, :]
        ms = jnp.mean(x * x, axis=-1, keepdims=True)
        u_ref[rs, :] = (x * lax.rsqrt(ms + EPS) * geff + shift).astype(u_ref.dtype)
        if copy_ref is not None:
            copy_ref[rs, :] = x


def _head_norm(x, gain, lane, head_dim):
    sq = x * x
    if head_dim == LANES:
        ms = jnp.mean(sq, axis=-1, keepdims=True)
    else:
        lo = lane < head_dim
        s_lo = jnp.sum(jnp.where(lo, sq, 0.0), axis=-1, keepdims=True)
        s_hi = jnp.sum(jnp.where(lo, 0.0, sq), axis=-1, keepdims=True)
        ms = jnp.where(lo, s_lo, s_hi) * (1.0 / head_dim)
    return x * lax.rsqrt(ms + EPS) * gain


def _rotate(y, cos, sin, lane, half, period):
    if 2 * half == LANES:
        partner = pltpu.roll(y, half, 1)
    else:
        fwd = pltpu.roll(y, LANES - half, 1)
        bwd = pltpu.roll(y, half, 1)
        partner = jnp.where(lane % period < half, fwd, bwd)
    return y * cos + partner * sin


def _ffn_kernel(h_ref, sh_ref, sc_ref, gt_ref, g_ref, wg_ref, wu_ref, wd_ref, o_ref, u_ref, *, rows, tn):
    j = pl.program_id(1)

    @pl.when(j == 0)
    def _():
        _norm_modulate_to(u_ref, h_ref, g_ref[...], sh_ref[0], sc_ref[0], copy_ref=o_ref)

    half_gate = 0.5 * gt_ref[0]
    for r in range(u_ref.shape[0] // rows):
        rs = slice(r * rows, (r + 1) * rows)
        u = u_ref[rs, :]
        a = _dot(u, wg_ref[...])
        b = _dot(u, wu_ref[...])
        act = (a * _sigmoid(a) * b).astype(MXU_DTYPE)
        for c in range(o_ref.shape[1] // tn):
            sl = slice(c * tn, (c + 1) * tn)
            o_ref[rs, sl] += half_gate[:, sl] * _dot(act, wd_ref[:, sl])


def _ffn(h, shift, scale, gate, g, wg, wu, wd, layer, *, seq, tm, rows, tf, tn):
    n, d = h.shape
    f = wg.shape[-1]
    per_b = seq // tm
    row = pl.BlockSpec((1, 1, d), lambda i, j: (i // per_b, 0, 0))
    return pl.pallas_call(
        functools.partial(_ffn_kernel, rows=rows, tn=tn),
        grid=(n // tm, f // tf),
        in_specs=[pl.BlockSpec((tm, d), lambda i, j: (jnp.minimum(i + jnp.where(j > 0, 1, 0), n // tm - 1), 0)),
                  row, row, row,
                  pl.BlockSpec((1, d), lambda i, j: (0, 0)),
                  pl.BlockSpec((None, d, tf), lambda i, j: (layer, 0, j)),
                  pl.BlockSpec((None, d, tf), lambda i, j: (layer, 0, j)),
                  pl.BlockSpec((None, tf, d), lambda i, j: (layer, j, 0))],
        out_specs=pl.BlockSpec((tm, d), lambda i, j: (i, 0)),
        out_shape=jax.ShapeDtypeStruct((n, d), f32),
        scratch_shapes=[pltpu.VMEM((tm, d), MXU_DTYPE)],
        compiler_params=_params("arbitrary", "arbitrary"),
        name="ffn",
    )(h, shift, scale, gate, g, wg, wu, wd)


def _proj_kernel(h_ref, sh_ref, sc_ref, g_ref, wq_ref, wgate_lo_ref, wgate_hi_ref, wsel_ref, gains_ref, rope_ref,
                 qa_ref, qi_ref, qb_ref, ga_ref, gb_ref, kv_ref, ki_ref, wi_ref, u_ref, ra_ref, rb_ref):
    j = pl.program_id(1)
    tm = h_ref.shape[0]
    tn = ra_ref.shape[1]
    lane = lax.broadcasted_iota(i32, (tm, LANES), 1)

    def table(k):
        return rope_ref[:, k * LANES:(k + 1) * LANES]

    def gain(k):
        return gains_ref[k:k + 1, :]

    def rope_a(y):
        return _rotate(y, table(0), table(1), lane, HD_A // 2, HD_A)

    def rope_b(y):
        return _rotate(y, table(2), table(3), lane, HD_B // 2, HD_B)

    def rope_i(y):
        return _rotate(y, table(4), table(5), lane, D_I_ROPE // 2, D_I)

    def q_dot(c):
        return _dot(u_ref[...], wq_ref[:, c * tn:(c + 1) * tn])

    def gate_dot():
        u = u_ref[...]
        return jnp.concatenate([_dot(u, wgate_lo_ref[...]), _dot(u, wgate_hi_ref[...])], axis=1)

    @pl.when(j == 0)
    def _():
        _norm_modulate_to(u_ref, h_ref, g_ref[...], sh_ref[0], sc_ref[0])
        sel = _dot(u_ref[...], wsel_ref[...])
        ra_ref[...] = q_dot(0)
        ka = rope_a(_head_norm(sel[:, 0:128], gain(1), lane, HD_A))
        kb = rope_b(_head_norm(sel[:, 256:384], gain(3), lane, HD_B))
        kv_ref[:, 0:128] = ka.astype(MXU_DTYPE)
        kv_ref[:, 128:256] = sel[:, 128:256].astype(MXU_DTYPE)
        kv_ref[:, 256:384] = kb.astype(MXU_DTYPE)
        kv_ref[:, 384:512] = sel[:, 384:512].astype(MXU_DTYPE)
        ki_ref[...] = rope_i(sel[:, 512:640]).astype(MXU_DTYPE)
        wi_ref[...] = sel[:, 640:768] * (H_I ** -0.5 * D_I ** -0.5)

    @pl.when(j == 1)
    def _():
        rb_ref[...] = q_dot(1)
        for k in range(8):
            y = _head_norm(ra_ref[:, k * LANES:(k + 1) * LANES], gain(0), lane, HD_A)
            qa_ref[:, k * LANES:(k + 1) * LANES] = rope_a(y).astype(MXU_DTYPE)

    @pl.when(j == 2)
    def _():
        ra_ref[...] = q_dot(2)
        for k in range(8):
            qi_ref[:, k * LANES:(k + 1) * LANES] = rope_i(rb_ref[:, k * LANES:(k + 1) * LANES]).astype(MXU_DTYPE)

    @pl.when(j == 3)
    def _():
        rb_ref[...] = gate_dot()
        for k in range(8):
            y = _head_norm(ra_ref[:, k * LANES:(k + 1) * LANES], gain(2), lane, HD_B)
            qb_ref[:, k * LANES:(k + 1) * LANES] = (rope_b(y) * (HD_B ** -0.5)).astype(MXU_DTYPE)

    per_gate = ga_ref.shape[1] // tn

    def put_gate(c, res):
        ref = ga_ref if c < per_gate else gb_ref
        c = c % per_gate
        ref[:, c * tn:(c + 1) * tn] = _sigmoid(res).astype(MXU_DTYPE)

    @pl.when(j == 4)
    def _():
        put_gate(1, gate_dot())
        put_gate(0, rb_ref[...])

    for c in range(2, 2 * per_gate):
        @pl.when(j == 3 + c)
        def _(c=c):
            put_gate(c, gate_dot())


def _proj(h, shift, scale, g, wq, wgate, wsel, gains, rope, layer, *, seq, tm):
    n, d = h.shape
    d_gate = wgate.shape[-1] // 2
    per_b = seq // tm
    tn = 1024
    n_q = N_BIG // tn
    n_gate = 2 * d_gate // tn
    nt = n // tm
    row = pl.BlockSpec((1, 1, d), lambda i, j: (i // per_b, 0, 0))
    tile = lambda w: pl.BlockSpec((tm, w), lambda i, j: (i, 0))

    def tile_until(w, last_use):
        return pl.BlockSpec((tm, w), lambda i, j: (jnp.minimum(i + jnp.where(j > last_use, 1, 0), nt - 1), 0))

    def gate_chunk(half):
        return pl.BlockSpec((None, d, tn // 2), lambda i, j: (
            layer, 0, 2 * jnp.where(j < n_q, n_gate - 1, j - n_q) + half))

    resident = lambda w: pl.BlockSpec((None, d, w), lambda i, j: (layer, 0, 0), pipeline_mode=pl.Buffered(1))
    bf = lambda w: jax.ShapeDtypeStruct((n, w), MXU_DTYPE)
    return pl.pallas_call(
        _proj_kernel,
        grid=(nt, n_q + n_gate),
        in_specs=[tile_until(d, 0), row, row,
                  pl.BlockSpec((1, d), lambda i, j: (0, 0)),
                  resident(N_BIG), gate_chunk(0), gate_chunk(1),
                  resident(SEL_COLS),
                  pl.BlockSpec((4, LANES), lambda i, j: (0, 0)),
                  tile_until(6 * LANES, n_q)],
        out_specs=[tile(1024), tile(1024), tile(1024), tile(d_gate), tile(d_gate),
                   tile(512), tile(LANES), tile(LANES)],
        out_shape=[bf(1024), bf(1024), bf(1024), bf(d_gate), bf(d_gate), bf(512), bf(LANES),
                   jax.ShapeDtypeStruct((n, LANES), f32)],
        scratch_shapes=[pltpu.VMEM((tm, d), MXU_DTYPE),
                        pltpu.VMEM((tm, tn), f32), pltpu.VMEM((tm, tn), f32)],
        compiler_params=_params("arbitrary", "arbitrary"),
        name="mix_proj",
    )(h, shift, scale, g, wq, wgate, wgate, wsel, gains, rope)


QB = 256
KC = 256
I16_MIN = -32768
PACK = 16


def _attn_kernel(qa_ref, qi_ref, qb_ref, wi_ref, kv_ref, ki_ref, sink_ref,
                 ya_ref, yb_ref,
                 vat_ref, vbt_ref, qat_ref, qit_ref, key_ref, hi_ref, lo_ref, m_ref, l_ref, *head_refs,
                 n_sel):
    acc_refs, za_refs, zb_refs = head_refs[:H_A], head_refs[H_A:2 * H_A], head_refs[2 * H_A:]
    i = pl.program_id(1)
    nstep = pl.num_programs(1)
    nchunk = i + 1
    halves = (slice(0, BLOCK), slice(BLOCK, QB))

    def rows(c, size=KC):
        return pl.ds(pl.multiple_of(c * size, size), size)

    def t128(x):
        return x.astype(f32).T.astype(MXU_DTYPE)

    @pl.when(i == 0)
    def _():
        def body(c, carry):
            b0, b1 = rows(2 * c, BLOCK), rows(2 * c + 1, BLOCK)
            vat_ref[c] = jnp.concatenate([t128(kv_ref[b0, 128:256]), t128(kv_ref[b1, 128:256])], axis=1)
            vbt_ref[2 * c] = t128(kv_ref[b0, 384:512])
            vbt_ref[2 * c + 1] = t128(kv_ref[b1, 384:512])
            return carry
        lax.fori_loop(0, nstep, body, 0)

    for k in range(8):
        sl = slice(k * LANES, (k + 1) * LANES)
        for e, hs in enumerate(halves):
            qat_ref[:, k * QB + e * BLOCK:k * QB + (e + 1) * BLOCK] = t128(qa_ref[hs, sl])
            qit_ref[sl, hs] = t128(qi_ref[hs, sl])
    w_t = jnp.concatenate([wi_ref[hs, :].T for hs in halves], axis=1)

    q_lane = lax.broadcasted_iota(i32, (BLOCK, QB), 1)
    k_row = lax.broadcasted_iota(i32, (BLOCK, QB), 0)
    q_pos = i * QB + q_lane

    def idx_body(c, carry):
        for e in range(KC // BLOCK):
            cc = (KC // BLOCK) * c + e
            r = rows(cc, BLOCK)
            kc = ki_ref[r, 0:D_I]
            acc = jnp.zeros((BLOCK, QB), f32)
            for h in range(H_I):
                d = _dot(kc, qit_ref[h * D_I:(h + 1) * D_I, :])
                acc = acc + w_t[h:h + 1, :] * jnp.maximum(d, 0.0)
            bits = lax.bitcast_convert_type(acc + 0.0, i32)
            key = jnp.where(bits >= 0, bits, bits ^ 0x7FFFFFFF)
            key = jnp.where(cc * BLOCK + k_row <= q_pos, key, INT_MIN)
            key_ref[r, :] = key
            hi_ref[r, :] = (key >> 16).astype(jnp.int16)
            lo_ref[r, :] = ((key & 0xFFFF) + I16_MIN).astype(jnp.int16)
        return carry
    lax.fori_loop(0, nchunk, idx_body, 0)

    need = jnp.minimum(n_sel, q_pos[0:1, :] + 1).astype(f32)
    one, zero = jnp.ones((), jnp.bfloat16), jnp.zeros((), jnp.bfloat16)

    n_total = key_ref.shape[0] // KC
    n_half = max(n_total // 2, 1)
    span = jnp.where(nchunk <= n_half, n_half, n_total)

    def pad_body(c, carry):
        hi_ref[rows(c), :] = jnp.full((KC, QB), I16_MIN, jnp.int16)
        lo_ref[rows(c), :] = jnp.full((KC, QB), I16_MIN, jnp.int16)
        return carry
    lax.fori_loop(nchunk, span, pad_body, 0)

    def search(n_chunks):
        def count_ge(ref, cand):
            cnt = None
            for c in range(n_chunks):
                ind = jnp.where(ref[c * KC:(c + 1) * KC, :] >= cand, one, zero)
                parts = [ind[g * PACK:(g + 1) * PACK] for g in range(KC // PACK)]
                while len(parts) > 1:
                    parts = [a + b for a, b in zip(parts[0::2], parts[1::2])]
                cnt = parts[0] if cnt is None else cnt + parts[0]
            return jnp.sum(cnt.astype(f32), axis=0, keepdims=True)

        def bisect(ref, want):
            def body(k, t_u):
                cand_u = t_u | lax.shift_left(jnp.int32(1), 15 - k)
                ok = count_ge(ref, (cand_u + I16_MIN).astype(jnp.int16)) >= want
                return jnp.where(ok, cand_u, t_u)
            return lax.fori_loop(0, 16, body, jnp.zeros((1, QB), i32))

        hi_u = bisect(hi_ref, need)
        hi_t = (hi_u + I16_MIN).astype(jnp.int16)
        above = jnp.where(hi_u == 0xFFFF, 0.0,
                          count_ge(hi_ref, (jnp.minimum(hi_u + 1, 0xFFFF) + I16_MIN).astype(jnp.int16)))
        for c in range(n_chunks):
            cs = slice(c * KC, (c + 1) * KC)
            lo_ref[cs, :] = jnp.where(hi_ref[cs, :] == hi_t, lo_ref[cs, :], jnp.int16(I16_MIN))
        lo_u = bisect(lo_ref, need - above)
        return ((hi_u + I16_MIN) * 65536 + lo_u,
                above + count_ge(lo_ref, (lo_u + I16_MIN).astype(jnp.int16)))

    def searched():
        if n_half == n_total:
            return search(n_total)
        return lax.cond(nchunk <= n_half, lambda: search(n_half), lambda: search(n_total))

    thr, picked = lax.cond(nchunk * KC <= n_sel,
                           lambda: (jnp.full((1, QB), INT_MIN + 1, i32), need), searched)


    @pl.when(jnp.max(picked - need) > 0.0)
    def _():
        def gt_body(c, cnt):
            return cnt + jnp.sum(jnp.where(key_ref[rows(c), :] > thr, 1.0, 0.0), axis=0, keepdims=True)
        allowed = need - lax.fori_loop(0, nchunk, gt_body, jnp.zeros((1, QB), f32))
        tri = (lax.broadcasted_iota(i32, (KC, KC), 0) >= lax.broadcasted_iota(i32, (KC, KC), 1))
        tri = jnp.where(tri, 1.0, 0.0).astype(jnp.bfloat16)

        def fix_body(c, seen):
            k = key_ref[rows(c), :]
            tie = jnp.where(k == thr, 1.0, 0.0)
            rank = _dot(tri, tie.astype(jnp.bfloat16)) + seen
            key_ref[rows(c), :] = jnp.where((tie > 0.0) & (rank > allowed), thr - 1, k)
            return seen + jnp.sum(tie, axis=0, keepdims=True)
        lax.fori_loop(0, nchunk, fix_body, jnp.zeros((1, QB), f32))

    m_ref[...] = jnp.full(m_ref.shape, M_INIT, f32)
    l_ref[...] = jnp.zeros(l_ref.shape, f32)
    for acc_ref in acc_refs:
        acc_ref[...] = jnp.zeros(acc_ref.shape, f32)
    c_a = HD_A ** -0.5 * LOG2E

    def logits_of(c, h):
        return _dot(kv_ref[rows(c), 0:128], qat_ref[:, h * QB:(h + 1) * QB])

    def chunk_step(c, z_in, z_out):
        nxt = jnp.minimum(c + 1, nchunk - 1)
        vt = vat_ref[c]
        bias = jnp.where(key_ref[rows(c), :] >= thr, 0.0, NEG)
        m_all, l_all = m_ref[...], l_ref[...]
        m_out, l_out = [], []
        for h in range(H_A):
            hs = slice(h * QB, (h + 1) * QB)
            z = z_in[h][...] + bias
            m_old = m_all[:, hs]
            m_new = jnp.maximum(m_old, jnp.max(z, axis=0, keepdims=True))
            z_out[h][...] = logits_of(nxt, h)
            alpha = jnp.exp2((m_old - m_new) * c_a)
            p = jnp.exp2((z - m_new) * c_a)
            l_out.append(alpha * l_all[:, hs] + jnp.sum(p, axis=0, keepdims=True))
            acc_refs[h][...] = alpha * acc_refs[h][...] + _dot(vt, p.astype(MXU_DTYPE))
            m_out.append(m_new)
        m_ref[...] = jnp.concatenate(m_out, axis=1)
        l_ref[...] = jnp.concatenate(l_out, axis=1)

    for h in range(H_A):
        za_refs[h][...] = logits_of(0, h)

    def att_body(k2, carry):
        chunk_step(2 * k2, za_refs, zb_refs)

        @pl.when(2 * k2 + 1 < nchunk)
        def _():
            chunk_step(2 * k2 + 1, zb_refs, za_refs)
        return carry
    lax.fori_loop(0, (nchunk + 1) // 2, att_body, 0)

    inv_l = 1.0 / l_ref[...]
    for k in range(H_A):
        ya_t = acc_refs[k][...] * inv_l[:, k * QB:(k + 1) * QB]
        for e, hs in enumerate(halves):
            ya_ref[hs, k * LANES:(k + 1) * LANES] = ya_t[:, e * BLOCK:(e + 1) * BLOCK].T.astype(MXU_DTYPE)

    b_lane = lax.broadcasted_iota(i32, (BLOCK, LANES), 1)
    b_row = lax.broadcasted_iota(i32, (BLOCK, LANES), 0)
    sink = sink_ref[...]
    zero_pad = jnp.zeros((HD_B, LANES), MXU_DTYPE)
    per_g = H_B // KVH_B
    for e, hs in enumerate(halves):
        blk = 2 * i + e
        ip = jnp.maximum(blk - 1, 0)
        kband = jnp.concatenate([kv_ref[rows(ip, BLOCK), 256:384], kv_ref[rows(blk, BLOCK), 256:384]], axis=0)
        ext = []
        for k in range(8):
            t = t128(qb_ref[hs, k * LANES:(k + 1) * LANES])
            for o in range(2):
                piece = t[o * HD_B:(o + 1) * HD_B, :]
                grp = (2 * k + o) // per_g
                ext.append(jnp.concatenate([piece, zero_pad] if grp == 0 else [zero_pad, piece], axis=0))
        s = _dot(kband, jnp.concatenate(ext, axis=1))
        prev_open = jnp.where(blk > 0, 0.0, NEG)
        bias = jnp.concatenate([jnp.where(b_row > b_lane, prev_open, NEG),
                                jnp.where(b_row <= b_lane, 0.0, NEG)], axis=0)
        bias2 = jnp.concatenate([bias, bias], axis=1)
        vbands = [jnp.concatenate([vbt_ref[ip][g * HD_B:(g + 1) * HD_B, :],
                                   vbt_ref[blk][g * HD_B:(g + 1) * HD_B, :]], axis=1) for g in range(KVH_B)]
        for k in range(8):
            cols = slice(2 * k * LANES, (2 * k + 2) * LANES)
            z = s[:, cols] + bias2
            m = jnp.maximum(jnp.max(z, axis=0, keepdims=True), sink[:, cols])
            ex = jnp.exp(z - m)
            inv = 1.0 / (jnp.sum(ex, axis=0, keepdims=True) + jnp.exp(sink[:, cols] - m))
            og = _dot(vbands[2 * k // per_g], ex.astype(MXU_DTYPE)) * inv
            yb_ref[hs, k * LANES:(k + 1) * LANES] = (
                jnp.concatenate([og[:, 0:LANES], og[:, LANES:2 * LANES]], axis=0).T.astype(MXU_DTYPE))


def _attn(qa, qi, qb, wi, kv, ki, sink_row, *, batch, seq):
    n = qa.shape[0]
    nstep = seq // QB
    n_sel = min(K_MAX, seq // 4)
    blk = lambda w: pl.BlockSpec((QB, w), lambda b, i: (b * nstep + i, 0))
    per_b = lambda w: pl.BlockSpec((seq, w), lambda b, i: (b, 0))
    return pl.pallas_call(
        functools.partial(_attn_kernel, n_sel=n_sel),
        grid=(batch, nstep),
        in_specs=[blk(1024), blk(1024), blk(1024), blk(LANES), per_b(512), per_b(LANES),
                  pl.BlockSpec((1, H_B * LANES), lambda b, i: (0, 0))],
        out_specs=[blk(1024), blk(1024)],
        out_shape=[jax.ShapeDtypeStruct((n, 1024), MXU_DTYPE)] * 2,
        scratch_shapes=[pltpu.VMEM((nstep, HD_A, KC), MXU_DTYPE),
                        pltpu.VMEM((seq // BLOCK, LANES, BLOCK), MXU_DTYPE),
                        pltpu.VMEM((HD_A, H_A * QB), MXU_DTYPE),
                        pltpu.VMEM((H_I * D_I, QB), MXU_DTYPE),
                        pltpu.VMEM((seq, QB), i32),
                        pltpu.VMEM((seq, QB), jnp.int16),
                        pltpu.VMEM((seq, QB), jnp.int16),
                        pltpu.VMEM((1, H_A * QB), f32),
                        pltpu.VMEM((1, H_A * QB), f32)]
                       + [pltpu.VMEM((HD_A, QB), f32)] * H_A
                       + [pltpu.VMEM((KC, QB), f32)] * (2 * H_A),

        compiler_params=_params("arbitrary", "arbitrary"),
        name="mixers",
    )(qa, qi, qb, wi, kv, ki, sink_row)


def _out_kernel(h_ref, ya_ref, yb_ref, ga_ref, gb_ref, gt_ref, woa_ref, wob_ref, wout_ref, o_ref, *, tn, tc):
    ya, yb, gate = ya_ref[...], yb_ref[...], gt_ref[0]
    d = o_ref.shape[1]
    for c in range(d // tn):
        sl = slice(c * tn, (c + 1) * tn)
        merged = (ga_ref[:, sl].astype(f32) * _dot(ya, woa_ref[:, sl])
                  + gb_ref[:, sl].astype(f32) * _dot(yb, wob_ref[:, sl])).astype(MXU_DTYPE)
        for e in range(d // tc):
            oc = slice(e * tc, (e + 1) * tc)
            part = gate[:, oc] * _dot(merged, wout_ref[sl, oc])
            if c == 0:
                o_ref[:, oc] = h_ref[:, oc] + part
            else:
                o_ref[:, oc] += part


def _out_proj(h, ya, yb, ga, gb, gate, woa, wob, wout, layer, *, seq, tm, tn, tc):
    n, d = h.shape
    k = ya.shape[1]
    per_b = seq // tm
    resident = lambda r, c: pl.BlockSpec((None, r, c), lambda i: (layer, 0, 0), pipeline_mode=pl.Buffered(1))
    return pl.pallas_call(
        functools.partial(_out_kernel, tn=tn, tc=tc),
        grid=(n // tm,),
        in_specs=[pl.BlockSpec((tm, d), lambda i: (i, 0)),
                  pl.BlockSpec((tm, k), lambda i: (i, 0)),
                  pl.BlockSpec((tm, k), lambda i: (i, 0)),
                  pl.BlockSpec((tm, d), lambda i: (i, 0)),
                  pl.BlockSpec((tm, d), lambda i: (i, 0)),
                  pl.BlockSpec((1, 1, d), lambda i: (i // per_b, 0, 0)),
                  resident(k, d), resident(k, d), resident(d, d)],
        out_specs=pl.BlockSpec((tm, d), lambda i: (i, 0)),
        out_shape=jax.ShapeDtypeStruct((n, d), f32),
        compiler_params=_params("arbitrary"),
        name="out_proj",
    )(h, ya, yb, ga, gb, gate, woa, wob, wout)


def _tiles(seq, d_ff):
    tm = min(seq, 512)
    tm_ffn = min(seq, 1024)
    tf = 512 if d_ff % 512 == 0 else 256
    return tm, tm_ffn, tf


def _split_w_in(w_in, d):
    sizes = (H_A * HD_A, HD_A, HD_A, H_I * D_I, D_I, H_I, H_B * HD_B, KVH_B * HD_B, KVH_B * HD_B, d, d)
    offs = [0]
    for s in sizes:
        offs.append(offs[-1] + s)
    seg = lambda k: w_in[:, :, offs[k]:offs[k + 1]].astype(MXU_DTYPE)
    pad = lambda a, w: jnp.pad(a, ((0, 0), (0, 0), (0, w - a.shape[-1])))
    q = jnp.concatenate([seg(0), seg(3), seg(6)], axis=-1)
    gates = jnp.concatenate([seg(9), seg(10)], axis=-1)
    sel = jnp.concatenate([seg(1), seg(2), seg(7), seg(8), pad(seg(4), LANES), pad(seg(5), LANES)], axis=-1)
    return q.astype(MXU_DTYPE), gates.astype(MXU_DTYPE), sel.astype(MXU_DTYPE)


def kernel(x, c, positions, ada_w, ada_b, norm_ffn1_g, ffn1_w_gate, ffn1_w_up, ffn1_w_down, norm_mix_g, w_in, qn_a_g, kn_a_g, qn_b_g, kn_b_g, sinks, w_o_a, w_o_b, w_out, norm_ffn2_g, ffn2_w_gate, ffn2_w_up, ffn2_w_down):
    batch, seq, d = x.shape
    depth = ada_w.shape[0]
    d_ff = ffn1_w_gate.shape[-1]
    n = batch * seq
    assert seq % QB == 0 and d % 1024 == 0
    tm, tm_ffn, tf = _tiles(seq, d_ff)
    tn = min(d, 512)

    mod = _ada_mod(c, ada_w, ada_b).reshape(depth, batch, N_MOD, 1, d)
    rope = _rope_tables(positions)
    wq, wgate, wsel = _split_w_in(w_in, d)
    cast = lambda w: w.astype(MXU_DTYPE)
    w1 = (cast(ffn1_w_gate), cast(ffn1_w_up), cast(ffn1_w_down))
    w2 = (cast(ffn2_w_gate), cast(ffn2_w_up), cast(ffn2_w_down))
    woa, wob, wout = cast(w_o_a), cast(w_o_b), cast(w_out)
    gains = jnp.stack([qn_a_g, kn_a_g, jnp.tile(qn_b_g, (1, LANES // HD_B)),
                       jnp.tile(kn_b_g, (1, LANES // HD_B))], axis=1)
    sink_rows = jnp.repeat(sinks, LANES, axis=1).reshape(depth, 1, H_B * LANES)

    h = x.reshape(n, d)
    for l in range(depth):
        m = [mod[l, :, k] for k in range(N_MOD)]
        h = _ffn(h, m[0], m[1], m[2], norm_ffn1_g[l][None], *w1, l, seq=seq, tm=tm_ffn, rows=tm, tf=tf, tn=tn)
        qa, qi, qb, ga, gb, kv, ki, wi = _proj(h, m[3], m[4], norm_mix_g[l][None], wq, wgate, wsel,
                                               gains[l], rope, l, seq=seq, tm=tm)
        ya, yb = _attn(qa, qi, qb, wi, kv, ki, sink_rows[l], batch=batch, seq=seq)
        h = _out_proj(h, ya, yb, ga, gb, m[5], woa, wob, wout, l, seq=seq, tm=tm, tn=2 * tn, tc=tn)
        h = _ffn(h, m[6], m[7], m[8], norm_ffn2_g[l][None], *w2, l, seq=seq, tm=tm_ffn, rows=tm, tf=tf, tn=tn)
    return h.reshape(batch, seq, d)
```

```python
import functools

import jax
import jax.numpy as jnp
from jax import lax
from jax.experimental import pallas as pl
from jax.experimental.pallas import tpu as pltpu

H_A, HD_A = 8, 128
H_I, D_I, D_I_ROPE = 16, 64, 32
K_MAX = 256
H_B, HD_B, KVH_B = 16, 64, 2
BLOCK = 128
ROPE_THETA = 10000.0
EPS = 1e-6
NEG = -1e30
N_MOD = 9

LANES = 128
V7X_VMEM_LIMIT_BYTES = 60 * 1024 * 1024
MXU_DTYPE = jnp.bfloat16

INT_MIN = -2147483648
M_INIT = -1e29
LOG2E = 1.4426950408889634
SEL_COLS = 768
N_BIG = 3 * 1024

f32 = jnp.float32
i32 = jnp.int32


def _sigmoid(x):
    return 0.5 * jnp.tanh(0.5 * x) + 0.5


def _dot(a, b):
    return jnp.dot(a, b, preferred_element_type=f32)


def _params(*sem):
    return pltpu.CompilerParams(dimension_semantics=sem, vmem_limit_bytes=V7X_VMEM_LIMIT_BYTES)


def _ada_kernel(c_ref, w_ref, b_ref, o_ref):
    c = c_ref[...]
    act = (c * jax.nn.sigmoid(c)).astype(MXU_DTYPE)
    o_ref[0] = _dot(act, w_ref[0].astype(MXU_DTYPE)) + b_ref[0]


def _ada_mod(c, ada_w, ada_b):
    L, D, N = ada_w.shape
    B = c.shape[0]
    tn = min(N, 1024)
    return pl.pallas_call(
        _ada_kernel,
        grid=(L, N // tn),
        in_specs=[pl.BlockSpec((B, D), lambda l, j: (0, 0)),
                  pl.BlockSpec((1, D, tn), lambda l, j: (l, 0, j)),
                  pl.BlockSpec((1, 1, tn), lambda l, j: (l, 0, j))],
        out_specs=pl.BlockSpec((1, B, tn), lambda l, j: (l, 0, j)),
        out_shape=jax.ShapeDtypeStruct((L, B, N), f32),
        compiler_params=_params("arbitrary", "arbitrary"),
        name="ada_mod",
    )(c, ada_w, ada_b.reshape(L, 1, N))


def _rope_kernel(pos_ref, freq_ref, sign_ref, o_ref):
    ang = pos_ref[...].astype(f32) * freq_ref[...]
    lane = lax.broadcasted_iota(i32, ang.shape, 1)

    def rep(v, n):
        return jnp.where(lane < n, v, pltpu.roll(v, n, 1))

    for t, (trig, fill) in enumerate(((jnp.cos(ang), 1.0), (jnp.sin(ang), 0.0))):
        a = rep(trig, HD_A // 2)
        b = rep(rep(pltpu.roll(trig, LANES - HD_A // 2, 1), HD_B // 2), HD_B)
        i_ = rep(pltpu.roll(trig, LANES - HD_A // 2 - HD_B // 2, 1), D_I_ROPE // 2)
        i_ = rep(jnp.where(lane < D_I_ROPE, i_, fill), D_I)
        for k, tab in enumerate((a, b, i_)):
            if t:
                tab = tab * sign_ref[k:k + 1, :]
            o_ref[:, (2 * k + t) * LANES:(2 * k + t + 1) * LANES] = tab


def _rope_rows():
    lane = jnp.arange(LANES)

    def inv_freq(dim):
        return ROPE_THETA ** (-jnp.arange(0, dim, 2, dtype=f32) / dim)

    freq = jnp.concatenate([inv_freq(HD_A), inv_freq(HD_B), inv_freq(D_I_ROPE),
                            jnp.zeros((LANES - (HD_A + HD_B + D_I_ROPE) // 2,), f32)])[None]
    sa = jnp.where(lane % HD_A < HD_A // 2, -1.0, 1.0)
    sb = jnp.where(lane % HD_B < HD_B // 2, -1.0, 1.0)
    si = jnp.where(lane % D_I < D_I_ROPE // 2, -1.0, 1.0)
    return freq.astype(f32), jnp.stack([sa, sb, si]).astype(f32)


def _rope_tables(positions):
    n = positions.size
    tm = min(n, 1024)
    freq, sign = _rope_rows()
    return pl.pallas_call(
        _rope_kernel,
        grid=(n // tm,),
        in_specs=[pl.BlockSpec((tm, 1), lambda i: (i, 0)),
                  pl.BlockSpec((1, LANES), lambda i: (0, 0)),
                  pl.BlockSpec((3, LANES), lambda i: (0, 0))],
        out_specs=pl.BlockSpec((tm, 6 * LANES), lambda i: (i, 0)),
        out_shape=jax.ShapeDtypeStruct((n, 6 * LANES), f32),
        compiler_params=_params("arbitrary"),
        name="rope_tables",
    )(positions.reshape(n, 1), freq, sign)


NORM_SLAB = 16


def _norm_modulate_to(u_ref, h_ref, g, shift, scale, copy_ref=None):
    geff = g * (1.0 + scale)
    for s in range(h_ref.shape[0] // NORM_SLAB):
        rs = slice(s * NORM_SLAB, (s + 1) * NORM_SLAB)
        x = h_ref[rs, :]
        ms = jnp.mean(x * x, axis=-1, keepdims=True)
        u_ref[rs, :] = (x * lax.rsqrt(ms + EPS) * geff + shift).astype(u_ref.dtype)
        if copy_ref is not None:
            copy_ref[rs, :] = x


def _head_norm(x, gain, lane, head_dim):
    sq = x * x
    if head_dim == LANES:
        ms = jnp.mean(sq, axis=-1, keepdims=True)
    else:
        lo = lane < head_dim
        s_lo = jnp.sum(jnp.where(lo, sq, 0.0), axis=-1, keepdims=True)
        s_hi = jnp.sum(jnp.where(lo, 0.0, sq), axis=-1, keepdims=True)
        ms = jnp.where(lo, s_lo, s_hi) * (1.0 / head_dim)
    return x * lax.rsqrt(ms + EPS) * gain


def _rotate(y, cos, sin, lane, half, period):
    if 2 * half == LANES:
        partner = pltpu.roll(y, half, 1)
    else:
        fwd = pltpu.roll(y, LANES - half, 1)
        bwd = pltpu.roll(y, half, 1)
        partner = jnp.where(lane % period < half, fwd, bwd)
    return y * cos + partner * sin


def _ffn_kernel(h_ref, sh_ref, sc_ref, gt_ref, g_ref, wg_ref, wu_ref, wd_ref, o_ref, u_ref, *, rows, tn):
    j = pl.program_id(1)

    @pl.when(j == 0)
    def _():
        _norm_modulate_to(u_ref, h_ref, g_ref[...], sh_ref[0], sc_ref[0], copy_ref=o_ref)

    half_gate = 0.5 * gt_ref[0]
    for r in range(u_ref.shape[0] // rows):
        rs = slice(r * rows, (r + 1) * rows)
        u = u_ref[rs, :]
        a = _dot(u, wg_ref[...])
        b = _dot(u, wu_ref[...])
        act = (a * _sigmoid(a) * b).astype(MXU_DTYPE)
        for c in range(o_ref.shape[1] // tn):
            sl = slice(c * tn, (c + 1) * tn)
            o_ref[rs, sl] += half_gate[:, sl] * _dot(act, wd_ref[:, sl])


def _ffn(h, shift, scale, gate, g, wg, wu, wd, layer, *, seq, tm, rows, tf, tn):
    n, d = h.shape
    f = wg.shape[-1]
    per_b = seq // tm
    row = pl.BlockSpec((1, 1, d), lambda i, j: (i // per_b, 0, 0))
    return pl.pallas_call(
        functools.partial(_ffn_kernel, rows=rows, tn=tn),
        grid=(n // tm, f // tf),
        in_specs=[pl.BlockSpec((tm, d), lambda i, j: (jnp.minimum(i + jnp.where(j > 0, 1, 0), n // tm - 1), 0)),
                  row, row, row,
                  pl.BlockSpec((1, d), lambda i, j: (0, 0)),
                  pl.BlockSpec((None, d, tf), lambda i, j: (layer, 0, j)),
                  pl.BlockSpec((None, d, tf), lambda i, j: (layer, 0, j)),
                  pl.BlockSpec((None, tf, d), lambda i, j: (layer, j, 0))],
        out_specs=pl.BlockSpec((tm, d), lambda i, j: (i, 0)),
        out_shape=jax.ShapeDtypeStruct((n, d), f32),
        scratch_shapes=[pltpu.VMEM((tm, d), MXU_DTYPE)],
        compiler_params=_params("arbitrary", "arbitrary"),
        name="ffn",
    )(h, shift, scale, gate, g, wg, wu, wd)


def _proj_kernel(h_ref, sh_ref, sc_ref, g_ref, wq_ref, wgate_lo_ref, wgate_hi_ref, wsel_ref, gains_ref, rope_ref,
                 qa_ref, qi_ref, qb_ref, ga_ref, gb_ref, kv_ref, ki_ref, wi_ref, u_ref, ra_ref, rb_ref):
    j = pl.program_id(1)
    tm = h_ref.shape[0]
    tn = ra_ref.shape[1]
    lane = lax.broadcasted_iota(i32, (tm, LANES), 1)

    def table(k):
        return rope_ref[:, k * LANES:(k + 1) * LANES]

    def gain(k):
        return gains_ref[k:k + 1, :]

    def rope_a(y):
        return _rotate(y, table(0), table(1), lane, HD_A // 2, HD_A)

    def rope_b(y):
        return _rotate(y, table(2), table(3), lane, HD_B // 2, HD_B)

    def rope_i(y):
        return _rotate(y, table(4), table(5), lane, D_I_ROPE // 2, D_I)

    def q_dot(c):
        return _dot(u_ref[...], wq_ref[:, c * tn:(c + 1) * tn])

    def gate_dot():
        u = u_ref[...]
        return jnp.concatenate([_dot(u, wgate_lo_ref[...]), _dot(u, wgate_hi_ref[...])], axis=1)

    @pl.when(j == 0)
    def _():
        _norm_modulate_to(u_ref, h_ref, g_ref[...], sh_ref[0], sc_ref[0])
        sel = _dot(u_ref[...], wsel_ref[...])
        ra_ref[...] = q_dot(0)
        ka = rope_a(_head_norm(sel[:, 0:128], gain(1), lane, HD_A))
        kb = rope_b(_head_norm(sel[:, 256:384], gain(3), lane, HD_B))
        kv_ref[:, 0:128] = ka.astype(MXU_DTYPE)
        kv_ref[:, 128:256] = sel[:, 128:256].astype(MXU_DTYPE)
        kv_ref[:, 256:384] = kb.astype(MXU_DTYPE)
        kv_ref[:, 384:512] = sel[:, 384:512].astype(MXU_DTYPE)
        ki_ref[...] = rope_i(sel[:, 512:640]).astype(MXU_DTYPE)
        wi_ref[...] = sel[:, 640:768] * (H_I ** -0.5 * D_I ** -0.5)

    @pl.when(j == 1)
    def _():
        rb_ref[...] = q_dot(1)
        for k in range(8):
            y = _head_norm(ra_ref[:, k * LANES:(k + 1) * LANES], gain(0), lane, HD_A)
            qa_ref[:, k * LANES:(k + 1) * LANES] = rope_a(y).astype(MXU_DTYPE)

    @pl.when(j == 2)
    def _():
        ra_ref[...] = q_dot(2)
        for k in range(8):
            qi_ref[:, k * LANES:(k + 1) * LANES] = rope_i(rb_ref[:, k * LANES:(k + 1) * LANES]).astype(MXU_DTYPE)

    @pl.when(j == 3)
    def _():
        rb_ref[...] = gate_dot()
        for k in range(8):
            y = _head_norm(ra_ref[:, k * LANES:(k + 1) * LANES], gain(2), lane, HD_B)
            qb_ref[:, k * LANES:(k + 1) * LANES] = (rope_b(y) * (HD_B ** -0.5)).astype(MXU_DTYPE)

    per_gate = ga_ref.shape[1] // tn

    def put_gate(c, res):
        ref = ga_ref if c < per_gate else gb_ref
        c = c % per_gate
        ref[:, c * tn:(c + 1) * tn] = _sigmoid(res).astype(MXU_DTYPE)

    @pl.when(j == 4)
    def _():
        put_gate(1, gate_dot())
        put_gate(0, rb_ref[...])

    for c in range(2, 2 * per_gate):
        @pl.when(j == 3 + c)
        def _(c=c):
            put_gate(c, gate_dot())


def _proj(h, shift, scale, g, wq, wgate, wsel, gains, rope, layer, *, seq, tm):
    n, d = h.shape
    d_gate = wgate.shape[-1] // 2
    per_b = seq // tm
    tn = 1024
    n_q = N_BIG // tn
    n_gate = 2 * d_gate // tn
    nt = n // tm
    row = pl.BlockSpec((1, 1, d), lambda i, j: (i // per_b, 0, 0))
    tile = lambda w: pl.BlockSpec((tm, w), lambda i, j: (i, 0))

    def tile_until(w, last_use):
        return pl.BlockSpec((tm, w), lambda i, j: (jnp.minimum(i + jnp.where(j > last_use, 1, 0), nt - 1), 0))

    def gate_chunk(half):
        return pl.BlockSpec((None, d, tn // 2), lambda i, j: (
            layer, 0, 2 * jnp.where(j < n_q, n_gate - 1, j - n_q) + half))

    resident = lambda w: pl.BlockSpec((None, d, w), lambda i, j: (layer, 0, 0), pipeline_mode=pl.Buffered(1))
    bf = lambda w: jax.ShapeDtypeStruct((n, w), MXU_DTYPE)
    return pl.pallas_call(
        _proj_kernel,
        grid=(nt, n_q + n_gate),
        in_specs=[tile_until(d, 0), row, row,
                  pl.BlockSpec((1, d), lambda i, j: (0, 0)),
                  resident(N_BIG), gate_chunk(0), gate_chunk(1),
                  resident(SEL_COLS),
                  pl.BlockSpec((4, LANES), lambda i, j: (0, 0)),
                  tile_until(6 * LANES, n_q)],
        out_specs=[tile(1024), tile(1024), tile(1024), tile(d_gate), tile(d_gate),
                   tile(512), tile(LANES), tile(LANES)],
        out_shape=[bf(1024), bf(1024), bf(1024), bf(d_gate), bf(d_gate), bf(512), bf(LANES),
                   jax.ShapeDtypeStruct((n, LANES), f32)],
        scratch_shapes=[pltpu.VMEM((tm, d), MXU_DTYPE),
                        pltpu.VMEM((tm, tn), f32), pltpu.VMEM((tm, tn), f32)],
        compiler_params=_params("arbitrary", "arbitrary"),
        name="mix_proj",
    )(h, shift, scale, g, wq, wgate, wgate, wsel, gains, rope)


QB = 256
KC = 256
I16_MIN = -32768
PACK = 16
SPAN_STEP = 2


def _attn_kernel(qa_ref, qi_ref, qb_ref, wi_ref, kv_ref, ki_ref, sink_ref,
                 ya_ref, yb_ref,
                 vat_ref, vbt_ref, qat_ref, qit_ref, key_ref, hi_ref, lo_ref, m_ref, l_ref, *head_refs,
                 n_sel):
    acc_refs, za_refs, zb_refs = head_refs[:H_A], head_refs[H_A:2 * H_A], head_refs[2 * H_A:]
    i = pl.program_id(1)
    nstep = pl.num_programs(1)
    nchunk = i + 1
    halves = (slice(0, BLOCK), slice(BLOCK, QB))

    def rows(c, size=KC):
        return pl.ds(pl.multiple_of(c * size, size), size)

    def t128(x):
        return x.astype(f32).T.astype(MXU_DTYPE)

    @pl.when(i == 0)
    def _():
        def body(c, carry):
            b0, b1 = rows(2 * c, BLOCK), rows(2 * c + 1, BLOCK)
            vat_ref[c] = jnp.concatenate([t128(kv_ref[b0, 128:256]), t128(kv_ref[b1, 128:256])], axis=1)
            vbt_ref[2 * c] = t128(kv_ref[b0, 384:512])
            vbt_ref[2 * c + 1] = t128(kv_ref[b1, 384:512])
            return carry
        lax.fori_loop(0, nstep, body, 0)

    for k in range(8):
        sl = slice(k * LANES, (k + 1) * LANES)
        for e, hs in enumerate(halves):
            qat_ref[:, k * QB + e * BLOCK:k * QB + (e + 1) * BLOCK] = t128(qa_ref[hs, sl])
            qit_ref[sl, hs] = t128(qi_ref[hs, sl])
    w_t = jnp.concatenate([wi_ref[hs, :].T for hs in halves], axis=1)

    q_lane = lax.broadcasted_iota(i32, (BLOCK, QB), 1)
    k_row = lax.broadcasted_iota(i32, (BLOCK, QB), 0)
    q_pos = i * QB + q_lane

    def idx_body(c, carry):
        for e in range(KC // BLOCK):
            cc = (KC // BLOCK) * c + e
            r = rows(cc, BLOCK)
            kc = ki_ref[r, 0:D_I]
            acc = jnp.zeros((BLOCK, QB), f32)
            for h in range(H_I):
                d = _dot(kc, qit_ref[h * D_I:(h + 1) * D_I, :])
                acc = acc + w_t[h:h + 1, :] * jnp.maximum(d, 0.0)
            bits = lax.bitcast_convert_type(acc + 0.0, i32)
            key = jnp.where(bits >= 0, bits, bits ^ 0x7FFFFFFF)
            key = jnp.where(cc * BLOCK + k_row <= q_pos, key, INT_MIN)
            key_ref[r, :] = key
            hi_ref[r, :] = (key >> 16).astype(jnp.int16)
            lo_ref[r, :] = ((key & 0xFFFF) + I16_MIN).astype(jnp.int16)
        return carry
    lax.fori_loop(0, nchunk, idx_body, 0)

    need = jnp.minimum(n_sel, q_pos[0:1, :] + 1).astype(f32)
    one, zero = jnp.ones((), jnp.bfloat16), jnp.zeros((), jnp.bfloat16)

    n_total = key_ref.shape[0] // KC
    span = (nchunk + SPAN_STEP - 1) // SPAN_STEP * SPAN_STEP

    def pad_body(c, carry):
        hi_ref[rows(c), :] = jnp.full((KC, QB), I16_MIN, jnp.int16)
        lo_ref[rows(c), :] = jnp.full((KC, QB), I16_MIN, jnp.int16)
        return carry
    lax.fori_loop(nchunk, span, pad_body, 0)

    def search(n_chunks):
        def count_ge(ref, cand):
            cnt = None
            for c in range(n_chunks):
                ind = jnp.where(ref[c * KC:(c + 1) * KC, :] >= cand, one, zero)
                parts = [ind[g * PACK:(g + 1) * PACK] for g in range(KC // PACK)]
                while len(parts) > 1:
                    parts = [a + b for a, b in zip(parts[0::2], parts[1::2])]
                cnt = parts[0] if cnt is None else cnt + parts[0]
            return jnp.sum(cnt.astype(f32), axis=0, keepdims=True)

        def bisect(ref, want):
            def body(k, t_u):
                cand_u = t_u | lax.shift_left(jnp.int32(1), 15 - k)
                ok = count_ge(ref, (cand_u + I16_MIN).astype(jnp.int16)) >= want
                return jnp.where(ok, cand_u, t_u)
            return lax.fori_loop(0, 16, body, jnp.zeros((1, QB), i32))

        hi_u = bisect(hi_ref, need)
        hi_t = (hi_u + I16_MIN).astype(jnp.int16)
        above = jnp.where(hi_u == 0xFFFF, 0.0,
                          count_ge(hi_ref, (jnp.minimum(hi_u + 1, 0xFFFF) + I16_MIN).astype(jnp.int16)))
        for c in range(n_chunks):
            cs = slice(c * KC, (c + 1) * KC)
            lo_ref[cs, :] = jnp.where(hi_ref[cs, :] == hi_t, lo_ref[cs, :], jnp.int16(I16_MIN))
        lo_u = bisect(lo_ref, need - above)
        return ((hi_u + I16_MIN) * 65536 + lo_u,
                above + count_ge(lo_ref, (lo_u + I16_MIN).astype(jnp.int16)))

    def searched(lo=1, hi=n_total // SPAN_STEP):
        if lo == hi:
            return search(lo * SPAN_STEP)
        mid = (lo + hi) // 2
        return lax.cond(span <= mid * SPAN_STEP, lambda: searched(lo, mid), lambda: searched(mid + 1, hi))

    thr, picked = lax.cond(nchunk * KC <= n_sel,
                           lambda: (jnp.full((1, QB), INT_MIN + 1, i32), need), searched)


    @pl.when(jnp.max(picked - need) > 0.0)
    def _():
        def gt_body(c, cnt):
            return cnt + jnp.sum(jnp.where(key_ref[rows(c), :] > thr, 1.0, 0.0), axis=0, keepdims=True)
        allowed = need - lax.fori_loop(0, nchunk, gt_body, jnp.zeros((1, QB), f32))
        tri = (lax.broadcasted_iota(i32, (KC, KC), 0) >= lax.broadcasted_iota(i32, (KC, KC), 1))
        tri = jnp.where(tri, 1.0, 0.0).astype(jnp.bfloat16)

        def fix_body(c, seen):
            k = key_ref[rows(c), :]
            tie = jnp.where(k == thr, 1.0, 0.0)
            rank = _dot(tri, tie.astype(jnp.bfloat16)) + seen
            key_ref[rows(c), :] = jnp.where((tie > 0.0) & (rank > allowed), thr - 1, k)
            return seen + jnp.sum(tie, axis=0, keepdims=True)
        lax.fori_loop(0, nchunk, fix_body, jnp.zeros((1, QB), f32))

    m_ref[...] = jnp.full(m_ref.shape, M_INIT, f32)
    l_ref[...] = jnp.zeros(l_ref.shape, f32)
    for acc_ref in acc_refs:
        acc_ref[...] = jnp.zeros(acc_ref.shape, f32)
    c_a = HD_A ** -0.5 * LOG2E

    def logits_of(c, h):
        return _dot(kv_ref[rows(c), 0:128], qat_ref[:, h * QB:(h + 1) * QB])

    def chunk_step(c, z_in, z_out):
        nxt = jnp.minimum(c + 1, nchunk - 1)
        vt = vat_ref[c]
        bias = jnp.where(key_ref[rows(c), :] >= thr, 0.0, NEG)
        m_all, l_all = m_ref[...], l_ref[...]
        m_out, l_out = [], []
        for h in range(H_A):
            hs = slice(h * QB, (h + 1) * QB)
            z = z_in[h][...] + bias
            m_old = m_all[:, hs]
            m_new = jnp.maximum(m_old, jnp.max(z, axis=0, keepdims=True))
            z_out[h][...] = logits_of(nxt, h)
            alpha = jnp.exp2((m_old - m_new) * c_a)
            p = jnp.exp2((z - m_new) * c_a)
            l_out.append(alpha * l_all[:, hs] + jnp.sum(p, axis=0, keepdims=True))
            acc_refs[h][...] = alpha * acc_refs[h][...] + _dot(vt, p.astype(MXU_DTYPE))
            m_out.append(m_new)
        m_ref[...] = jnp.concatenate(m_out, axis=1)
        l_ref[...] = jnp.concatenate(l_out, axis=1)

    for h in range(H_A):
        za_refs[h][...] = logits_of(0, h)

    def att_body(k2, carry):
        chunk_step(2 * k2, za_refs, zb_refs)

        @pl.when(2 * k2 + 1 < nchunk)
        def _():
            chunk_step(2 * k2 + 1, zb_refs, za_refs)
        return carry
    lax.fori_loop(0, (nchunk + 1) // 2, att_body, 0)

    inv_l = 1.0 / l_ref[...]
    for k in range(H_A):
        ya_t = acc_refs[k][...] * inv_l[:, k * QB:(k + 1) * QB]
        for e, hs in enumerate(halves):
            ya_ref[hs, k * LANES:(k + 1) * LANES] = ya_t[:, e * BLOCK:(e + 1) * BLOCK].T.astype(MXU_DTYPE)

    b_lane = lax.broadcasted_iota(i32, (BLOCK, LANES), 1)
    b_row = lax.broadcasted_iota(i32, (BLOCK, LANES), 0)
    sink = sink_ref[...]
    zero_pad = jnp.zeros((HD_B, LANES), MXU_DTYPE)
    per_g = H_B // KVH_B
    for e, hs in enumerate(halves):
        blk = 2 * i + e
        ip = jnp.maximum(blk - 1, 0)
        kband = jnp.concatenate([kv_ref[rows(ip, BLOCK), 256:384], kv_ref[rows(blk, BLOCK), 256:384]], axis=0)
        ext = []
        for k in range(8):
            t = t128(qb_ref[hs, k * LANES:(k + 1) * LANES])
            for o in range(2):
                piece = t[o * HD_B:(o + 1) * HD_B, :]
                grp = (2 * k + o) // per_g
                ext.append(jnp.concatenate([piece, zero_pad] if grp == 0 else [zero_pad, piece], axis=0))
        s = _dot(kband, jnp.concatenate(ext, axis=1))
        prev_open = jnp.where(blk > 0, 0.0, NEG)
        bias = jnp.concatenate([jnp.where(b_row > b_lane, prev_open, NEG),
                                jnp.where(b_row <= b_lane, 0.0, NEG)], axis=0)
        bias2 = jnp.concatenate([bias, bias], axis=1)
        vbands = [jnp.concatenate([vbt_ref[ip][g * HD_B:(g + 1) * HD_B, :],
                                   vbt_ref[blk][g * HD_B:(g + 1) * HD_B, :]], axis=1) for g in range(KVH_B)]
        for k in range(8):
            cols = slice(2 * k * LANES, (2 * k + 2) * LANES)
            z = s[:, cols] + bias2
            m = jnp.maximum(jnp.max(z, axis=0, keepdims=True), sink[:, cols])
            ex = jnp.exp(z - m)
            inv = 1.0 / (jnp.sum(ex, axis=0, keepdims=True) + jnp.exp(sink[:, cols] - m))
            og = _dot(vbands[2 * k // per_g], ex.astype(MXU_DTYPE)) * inv
            yb_ref[hs, k * LANES:(k + 1) * LANES] = (
                jnp.concatenate([og[:, 0:LANES], og[:, LANES:2 * LANES]], axis=0).T.astype(MXU_DTYPE))


def _attn(qa, qi, qb, wi, kv, ki, sink_row, *, batch, seq):
    n = qa.shape[0]
    nstep = seq // QB
    n_sel = min(K_MAX, seq // 4)
    blk = lambda w: pl.BlockSpec((QB, w), lambda b, i: (b * nstep + i, 0))
    per_b = lambda w: pl.BlockSpec((seq, w), lambda b, i: (b, 0))
    return pl.pallas_call(
        functools.partial(_attn_kernel, n_sel=n_sel),
        grid=(batch, nstep),
        in_specs=[blk(1024), blk(1024), blk(1024), blk(LANES), per_b(512), per_b(LANES),
                  pl.BlockSpec((1, H_B * LANES), lambda b, i: (0, 0))],
        out_specs=[blk(1024), blk(1024)],
        out_shape=[jax.ShapeDtypeStruct((n, 1024), MXU_DTYPE)] * 2,
        scratch_shapes=[pltpu.VMEM((nstep, HD_A, KC), MXU_DTYPE),
                        pltpu.VMEM((seq // BLOCK, LANES, BLOCK), MXU_DTYPE),
                        pltpu.VMEM((HD_A, H_A * QB), MXU_DTYPE),
                        pltpu.VMEM((H_I * D_I, QB), MXU_DTYPE),
                        pltpu.VMEM((seq, QB), i32),
                        pltpu.VMEM((seq, QB), jnp.int16),
                        pltpu.VMEM((seq, QB), jnp.int16),
                        pltpu.VMEM((1, H_A * QB), f32),
                        pltpu.VMEM((1, H_A * QB), f32)]
                       + [pltpu.VMEM((HD_A, QB), f32)] * H_A
                       + [pltpu.VMEM((KC, QB), f32)] * (2 * H_A),

        compiler_params=_params("arbitrary", "arbitrary"),
        name="mixers",
    )(qa, qi, qb, wi, kv, ki, sink_row)


def _out_kernel(h_ref, ya_ref, yb_ref, ga_ref, gb_ref, gt_ref, woa_ref, wob_ref, wout_ref, o_ref, *, tn, tc):
    ya, yb, gate = ya_ref[...], yb_ref[...], gt_ref[0]
    d = o_ref.shape[1]
    for c in range(d // tn):
        sl = slice(c * tn, (c + 1) * tn)
        merged = (ga_ref[:, sl].astype(f32) * _dot(ya, woa_ref[:, sl])
                  + gb_ref[:, sl].astype(f32) * _dot(yb, wob_ref[:, sl])).astype(MXU_DTYPE)
        for e in range(d // tc):
            oc = slice(e * tc, (e + 1) * tc)
            part = gate[:, oc] * _dot(merged, wout_ref[sl, oc])
            if c == 0:
                o_ref[:, oc] = h_ref[:, oc] + part
            else:
                o_ref[:, oc] += part


def _out_proj(h, ya, yb, ga, gb, gate, woa, wob, wout, layer, *, seq, tm, tn, tc):
    n, d = h.shape
    k = ya.shape[1]
    per_b = seq // tm
    resident = lambda r, c: pl.BlockSpec((None, r, c), lambda i: (layer, 0, 0), pipeline_mode=pl.Buffered(1))
    return pl.pallas_call(
        functools.partial(_out_kernel, tn=tn, tc=tc),
        grid=(n // tm,),
        in_specs=[pl.BlockSpec((tm, d), lambda i: (i, 0)),
                  pl.BlockSpec((tm, k), lambda i: (i, 0)),
                  pl.BlockSpec((tm, k), lambda i: (i, 0)),
                  pl.BlockSpec((tm, d), lambda i: (i, 0)),
                  pl.BlockSpec((tm, d), lambda i: (i, 0)),
                  pl.BlockSpec((1, 1, d), lambda i: (i // per_b, 0, 0)),
                  resident(k, d), resident(k, d), resident(d, d)],
        out_specs=pl.BlockSpec((tm, d), lambda i: (i, 0)),
        out_shape=jax.ShapeDtypeStruct((n, d), f32),
        compiler_params=_params("arbitrary"),
        name="out_proj",
    )(h, ya, yb, ga, gb, gate, woa, wob, wout)


def _tiles(seq, d_ff):
    tm = min(seq, 512)
    tm_ffn = min(seq, 1024)
    tf = 512 if d_ff % 512 == 0 else 256
    return tm, tm_ffn, tf


def _split_w_in(w_in, d):
    sizes = (H_A * HD_A, HD_A, HD_A, H_I * D_I, D_I, H_I, H_B * HD_B, KVH_B * HD_B, KVH_B * HD_B, d, d)
    offs = [0]
    for s in sizes:
        offs.append(offs[-1] + s)
    seg = lambda k: w_in[:, :, offs[k]:offs[k + 1]].astype(MXU_DTYPE)
    pad = lambda a, w: jnp.pad(a, ((0, 0), (0, 0), (0, w - a.shape[-1])))
    q = jnp.concatenate([seg(0), seg(3), seg(6)], axis=-1)
    gates = jnp.concatenate([seg(9), seg(10)], axis=-1)
    sel = jnp.concatenate([seg(1), seg(2), seg(7), seg(8), pad(seg(4), LANES), pad(seg(5), LANES)], axis=-1)
    return q.astype(MXU_DTYPE), gates.astype(MXU_DTYPE), sel.astype(MXU_DTYPE)


def kernel(x, c, positions, ada_w, ada_b, norm_ffn1_g, ffn1_w_gate, ffn1_w_up, ffn1_w_down, norm_mix_g, w_in, qn_a_g, kn_a_g, qn_b_g, kn_b_g, sinks, w_o_a, w_o_b, w_out, norm_ffn2_g, ffn2_w_gate, ffn2_w_up, ffn2_w_down):
    batch, seq, d = x.shape
    depth = ada_w.shape[0]
    d_ff = ffn1_w_gate.shape[-1]
    n = batch * seq
    assert seq % QB == 0 and d % 1024 == 0
    tm, tm_ffn, tf = _tiles(seq, d_ff)
    tn = min(d, 512)

    mod = _ada_mod(c, ada_w, ada_b).reshape(depth, batch, N_MOD, 1, d)
    rope = _rope_tables(positions)
    wq, wgate, wsel = _split_w_in(w_in, d)
    cast = lambda w: w.astype(MXU_DTYPE)
    w1 = (cast(ffn1_w_gate), cast(ffn1_w_up), cast(ffn1_w_down))
    w2 = (cast(ffn2_w_gate), cast(ffn2_w_up), cast(ffn2_w_down))
    woa, wob, wout = cast(w_o_a), cast(w_o_b), cast(w_out)
    gains = jnp.stack([qn_a_g, kn_a_g, jnp.tile(qn_b_g, (1, LANES // HD_B)),
                       jnp.tile(kn_b_g, (1, LANES // HD_B))], axis=1)
    sink_rows = jnp.repeat(sinks, LANES, axis=1).reshape(depth, 1, H_B * LANES)

    h = x.reshape(n, d)
    for l in range(depth):
        m = [mod[l, :, k] for k in range(N_MOD)]
        h = _ffn(h, m[0], m[1], m[2], norm_ffn1_g[l][None], *w1, l, seq=seq, tm=tm_ffn, rows=tm, tf=tf, tn=tn)
        qa, qi, qb, ga, gb, kv, ki, wi = _proj(h, m[3], m[4], norm_mix_g[l][None], wq, wgate, wsel,
                                               gains[l], rope, l, seq=seq, tm=tm)
        ya, yb = _attn(qa, qi, qb, wi, kv, ki, sink_rows[l], batch=batch, seq=seq)
        h = _out_proj(h, ya, yb, ga, gb, m[5], woa, wob, wout, l, seq=seq, tm=tm, tn=2 * tn, tc=tn)
        h = _ffn(h, m[6], m[7], m[8], norm_ffn2_g[l][None], *w2, l, seq=seq, tm=tm_ffn, rows=tm, tf=tf, tn=tn)
    return h.reshape(batch, seq, d)
```
